```python
import jax, jax.numpy as jnp
from jax import lax
import numpy as np

D_MODEL = 1024
BATCH = 4
SEQ = 4096
DEPTH = 4

D_MIX = D_MODEL
N_MIXERS = 4
GROUP_W = D_MIX // N_MIXERS
POOL_WINDOWS = (2, 4, 8, 16)
POOL_GROUPS = len(POOL_WINDOWS)
POOL_CG = GROUP_W // POOL_GROUPS
CONV_HEADS = 4
CONV_WIDTH = 31
SG_CHUNK = 128
SG_HEADS = 4
SC_HEADS = 4
SC_WIDTH = 3
PROJ_SPLITS = 8
PROJ_IN = PROJ_SPLITS * GROUP_W
N_EXPERTS = 16
N_EXPERT_GROUPS = 4
EXPERTS_PER_GROUP = N_EXPERTS // N_EXPERT_GROUPS
TOP_K = 2
D_EXPERT = D_MODEL // 4
N_MOD = 6
EPS = 1e-6
LN_EPS = 1e-5

kernel_name = "hybrid_parallel_pool_conv_sgu_shortconv_moe"


def rms_norm(x, g):
    xf = x.astype(jnp.float32)
    y = xf * lax.rsqrt(jnp.mean(xf * xf, axis=-1, keepdims=True) + EPS)
    return (y * g.astype(jnp.float32)).astype(x.dtype)


def group_rms_norm(y, g):
    B, S, _ = y.shape
    yf = y.astype(jnp.float32).reshape(B, S, N_MIXERS, GROUP_W)
    yf = yf * lax.rsqrt(jnp.mean(yf * yf, axis=-1, keepdims=True) + EPS)
    return (yf.reshape(B, S, N_MIXERS * GROUP_W) * g.astype(jnp.float32)).astype(y.dtype)


def layer_norm(x, g, b):
    xf = x.astype(jnp.float32)
    mu = jnp.mean(xf, axis=-1, keepdims=True)
    var = jnp.mean(jnp.square(xf - mu), axis=-1, keepdims=True)
    y = (xf - mu) * lax.rsqrt(var + LN_EPS)
    return (y * g.astype(jnp.float32) + b.astype(jnp.float32)).astype(x.dtype)


def causal_depthwise_conv(x, w):
    K = w.shape[0]
    return lax.conv_general_dilated(
        x, w[:, None, :].astype(x.dtype), window_strides=(1,), padding=[(K - 1, 0)],
        dimension_numbers=('NWC', 'WIO', 'NWC'), feature_group_count=x.shape[-1])


def pool_mixer(a, pool_w, pool_scale):
    B, S, C = a.shape
    af = a.astype(jnp.float32)
    cs = lax.cumsum(af, axis=1).reshape(B, S, POOL_GROUPS, POOL_CG)
    a_g = af.reshape(B, S, POOL_GROUPS, POOL_CG)
    pos1 = jnp.arange(1, S + 1, dtype=jnp.float32)[None, :, None]
    pooled = []
    for gi, w in enumerate(POOL_WINDOWS):
        c_g = cs[:, :, gi]
        prev = jnp.pad(c_g[:, :-w], ((0, 0), (w, 0), (0, 0)))
        pooled.append((c_g - prev) / jnp.minimum(pos1, float(w)))
    pooled = jnp.stack(pooled, axis=2)
    d = (pooled - a_g).astype(a.dtype)
    y = jnp.einsum('bsgc,gcd->bsgd', d, pool_w).reshape(B, S, C)
    return y * pool_scale


def conformer_conv(val, gate, dw, db, ln_g, ln_b, pw, pb):
    h = val * jax.nn.sigmoid(gate)
    h = causal_depthwise_conv(h, dw) + db
    h = jax.nn.silu(layer_norm(h, ln_g, ln_b))
    return h @ pw + pb


def spatial_gating(u, v, ln_g, ln_b, w_s, b_s):
    B, S, C = u.shape
    v = layer_norm(v, ln_g, ln_b)
    vc = v.reshape(B, S // SG_CHUNK, SG_CHUNK, SG_HEADS, C // SG_HEADS)
    mask = jnp.tril(jnp.ones((SG_CHUNK, SG_CHUNK), dtype=bool))
    w = jnp.where(mask[None], w_s, 0)
    mixed = jnp.einsum('hpq,bnqhc->bnphc', w, vc) + b_s.T[None, None, :, :, None]
    return u * mixed.reshape(B, S, C)


def short_gated_conv(bg, cg, xv, w):
    return bg * causal_depthwise_conv(cg * xv, w)


def moe(h, router_w, router_bias, w_gate, w_up, w_down):
    B, S, D = h.shape
    t = h.reshape(B * S, D)
    scores = jax.nn.sigmoid((t @ router_w).astype(jnp.float32))
    sel = scores + router_bias.astype(jnp.float32)
    grp = sel.reshape(-1, N_EXPERT_GROUPS, EXPERTS_PER_GROUP)
    grp_score = jnp.sum(lax.top_k(grp, TOP_K)[0], axis=-1)
    g_idx = jnp.argmax(grp_score, axis=-1)
    in_group = (jnp.arange(N_EXPERTS) // EXPERTS_PER_GROUP)[None, :] == g_idx[:, None]
    masked = jnp.where(in_group, sel, -jnp.inf)
    _, e_idx = lax.top_k(masked, TOP_K)
    w_sel = jnp.take_along_axis(scores, e_idx, axis=-1)
    w_sel = w_sel / jnp.sum(w_sel, axis=-1, keepdims=True)
    combine = jnp.sum(jax.nn.one_hot(e_idx, N_EXPERTS, dtype=jnp.float32) * w_sel[..., None], axis=1)
    hg = jnp.einsum('td,edf->tef', t, w_gate)
    hu = jnp.einsum('td,edf->tef', t, w_up)
    act = jax.nn.silu(hg) * hu * combine[..., None].astype(t.dtype)
    y = jnp.einsum('tef,efd->td', act, w_down)
    return y.reshape(B, S, D)


def setup_inputs(seed: int = 0) -> dict:
    key = jax.random.key(seed)
    ks = jax.random.split(key, 28)
    f32 = jnp.float32
    L, D, G = DEPTH, D_MODEL, GROUP_W

    def nrm(k, shape, scale):
        return jax.random.normal(k, shape, f32) * scale

    def gain(k, shape):
        return 1.0 + 0.05 * jax.random.normal(k, shape, f32)

    return {
        "x": nrm(ks[0], (BATCH, SEQ, D), 1.0),
        "c": nrm(ks[1], (BATCH, D), 1.0),
        "norm1_g": gain(ks[2], (L, D)),
        "norm2_g": gain(ks[3], (L, D)),
        "w_ada": nrm(ks[4], (L, D, N_MOD * D), 0.5 * D ** -0.5),
        "b_ada": nrm(ks[5], (L, N_MOD * D), 0.02),
        "w_in": nrm(ks[6], (L, D, PROJ_IN), D ** -0.5),
        "pool_w": nrm(ks[7], (L, POOL_GROUPS, POOL_CG, POOL_CG), POOL_CG ** -0.5),
        "pool_scale": gain(ks[8], (L, G)),
        "conv_dw": nrm(ks[9], (L, CONV_WIDTH, G), CONV_WIDTH ** -0.5),
        "conv_db": nrm(ks[10], (L, G), 0.02),
        "conv_ln_g": gain(ks[11], (L, G)),
        "conv_ln_b": nrm(ks[12], (L, G), 0.02),
        "conv_pw": nrm(ks[13], (L, G, G), G ** -0.5),
        "conv_pb": nrm(ks[14], (L, G), 0.02),
        "sg_ln_g": gain(ks[15], (L, G)),
        "sg_ln_b": nrm(ks[16], (L, G), 0.02),
        "sg_w": nrm(ks[17], (L, SG_HEADS, SG_CHUNK, SG_CHUNK), SG_CHUNK ** -0.5),
        "sg_b": gain(ks[18], (L, SG_HEADS, SG_CHUNK)),
        "sc_w": nrm(ks[19], (L, SC_WIDTH, G), SC_WIDTH ** -0.5),
        "out_norm_g": gain(ks[20], (L, D)),
        "w_out": nrm(ks[21], (L, D, D), D ** -0.5),
        "router_w": nrm(ks[22], (D, N_EXPERTS), D ** -0.5),
        "router_bias": nrm(ks[23], (N_EXPERTS,), 0.01),
        "exp_w_gate": nrm(ks[24], (L, N_EXPERTS, D, D_EXPERT), D ** -0.5),
        "exp_w_up": nrm(ks[25], (L, N_EXPERTS, D, D_EXPERT), D ** -0.5),
        "exp_w_down": nrm(ks[26], (L, N_EXPERTS, D_EXPERT, D), D_EXPERT ** -0.5),
        "final_g": gain(ks[27], (D,)),
    }


def reference(x, c, norm1_g, norm2_g, w_ada, b_ada, w_in, pool_w, pool_scale,
              conv_dw, conv_db, conv_ln_g, conv_ln_b, conv_pw, conv_pb,
              sg_ln_g, sg_ln_b, sg_w, sg_b, sc_w, out_norm_g, w_out,
              router_w, router_bias, exp_w_gate, exp_w_up, exp_w_down, final_g):
    c_act = jax.nn.silu(c)
    for l in range(DEPTH):
        mod = c_act @ w_ada[l] + b_ada[l]
        sh1, sc1, g1, sh2, sc2, g2 = jnp.split(mod, N_MOD, axis=-1)

        h = rms_norm(x, norm1_g[l]) * (1 + sc1[:, None]) + sh1[:, None]
        p = h @ w_in[l]
        a, b_val, b_gate, u, v, d_b, d_c, d_x = jnp.split(p, PROJ_SPLITS, axis=-1)
        y_a = pool_mixer(a, pool_w[l], pool_scale[l])
        y_b = conformer_conv(b_val, b_gate, conv_dw[l], conv_db[l], conv_ln_g[l], conv_ln_b[l],
                             conv_pw[l], conv_pb[l])
        y_c = spatial_gating(u, v, sg_ln_g[l], sg_ln_b[l], sg_w[l], sg_b[l])
        y_d = short_gated_conv(d_b, d_c, d_x, sc_w[l])
        y = group_rms_norm(jnp.concatenate([y_a, y_b, y_c, y_d], axis=-1), out_norm_g[l])
        x = x + g1[:, None] * (y @ w_out[l])

        h2 = rms_norm(x, norm2_g[l]) * (1 + sc2[:, None]) + sh2[:, None]
        x = x + g2[:, None] * moe(h2, router_w, router_bias, exp_w_gate[l], exp_w_up[l], exp_w_down[l])
    return rms_norm(x, final_g)
```

```python
import functools

import jax
import jax.numpy as jnp
from jax import lax
from jax.experimental import pallas as pl
from jax.experimental.pallas import tpu as pltpu

D_MODEL = 1024
GROUP_W = 256
POOL_WINDOWS = (2, 4, 8, 16)
POOL_CG = 64
CONV_WIDTH = 31
SG_CHUNK = 128
SG_HEADS = 4
SC_WIDTH = 3
PROJ_IN = 2048
N_EXPERTS = 16
N_EXPERT_GROUPS = 4
EXPERTS_PER_GROUP = 4
D_EXPERT = 256
N_MOD = 6
EPS = 1e-6
LN_EPS = 1e-5

LANES = 128
SUBLANES = 8
HALO = 32
CONV_ROWS = 32
VMEM_LIMIT = 56 * 1024 * 1024

F32 = jnp.float32
BF16 = jnp.bfloat16


def _ada_body(c_ref, w_ref, b_ref, o_ref):
    ca = jax.nn.silu(c_ref[...])
    o_ref[0] = jnp.dot(ca, w_ref[0], preferred_element_type=F32,
                       precision=lax.Precision.HIGHEST) + b_ref[0]


def _ada_mod(c, w_ada, b_ada):
    depth, d, n = w_ada.shape
    b = c.shape[0]
    tn = 1024
    cp = jnp.pad(c, ((0, SUBLANES - b), (0, 0)))
    out = pl.pallas_call(
        _ada_body,
        grid=(depth, n // tn),
        in_specs=[
            pl.BlockSpec((SUBLANES, d), lambda l, j: (0, 0)),
            pl.BlockSpec((1, d, tn), lambda l, j: (l, 0, j)),
            pl.BlockSpec((1, 1, tn), lambda l, j: (l, 0, j)),
        ],
        out_specs=pl.BlockSpec((1, SUBLANES, tn), lambda l, j: (l, 0, j)),
        out_shape=jax.ShapeDtypeStruct((depth, SUBLANES, n), F32),
        compiler_params=pltpu.CompilerParams(
            dimension_semantics=("arbitrary", "arbitrary"), vmem_limit_bytes=VMEM_LIMIT),
        name="ada_mod",
    )(cp, w_ada, b_ada.reshape(depth, 1, n))
    return out[:, :b].reshape(depth, b, N_MOD, d)


def _layer_norm_rows(v, g, b):
    mu = jnp.mean(v, axis=-1, keepdims=True)
    vc = v - mu
    var = jnp.mean(vc * vc, axis=-1, keepdims=True)
    return vc * lax.rsqrt(var + LN_EPS) * g + b


def _carry_history(ext_ref, cur, j, ts, halo):
    @pl.when(j == 0)
    def _():
        ext_ref[0:halo, :] = jnp.zeros((halo, ext_ref.shape[1]), F32)

    @pl.when(j > 0)
    def _():
        ext_ref[0:halo, :] = ext_ref[ts:ts + halo, :]

    ext_ref[halo:halo + ts, :] = cur


def _carry_history_slabs(ext_ref, cur, j, ts, halo):
    nslab = ext_ref.shape[0]

    @pl.when(j == 0)
    def _():
        ext_ref[:, 0:halo, :] = jnp.zeros((nslab, halo, LANES), F32)

    @pl.when(j > 0)
    def _():
        ext_ref[:, 0:halo, :] = ext_ref[:, ts:ts + halo, :]

    for cb in range(nslab):
        ext_ref[cb, halo:halo + ts, :] = cur[:, cb * LANES:(cb + 1) * LANES]


def _first_argmax(vals):
    best = vals[0]
    idx = jnp.zeros_like(best)
    for k in range(1, len(vals)):
        better = vals[k] > best
        idx = jnp.where(better, float(k), idx)
        best = jnp.where(better, vals[k], best)
    return idx, best


def _pick(idx, vals):
    out = vals[0]
    for k in range(1, len(vals)):
        out = jnp.where(idx == float(k), vals[k], out)
    return out


def _mixer_body(has_prev, ts, *refs):
    refs = list(refs)
    x_ref = refs.pop(0)
    if has_prev:
        yprev_ref = refs.pop(0)
        g2p_ref = refs.pop(0)
    (mod_ref, n1g_ref, n2g_ref, win_ref, poolw_ref, pools_ref,
     cdw_ref, cdb_ref, clg_ref, clb_ref, cpw_ref, cpb_ref,
     slg_ref, slb_ref, sgw_ref, sgb_ref, scw_ref, ong_ref, wout_ref, rwt_ref, rb_ref,
     x1_ref, h2_ref, tok_ref,
     ea_ref, eg_ref, ed_ref) = refs

    j = pl.program_id(1)
    gw = GROUP_W

    x = x_ref[0]
    if has_prev:
        x = x + g2p_ref[0] * yprev_ref[0]

    sh1 = mod_ref[0, 0:1, :]
    sc1 = mod_ref[0, 1:2, :]
    g1 = mod_ref[0, 2:3, :]
    sh2 = mod_ref[0, 3:4, :]
    sc2 = mod_ref[0, 4:5, :]

    r1 = lax.rsqrt(jnp.mean(x * x, axis=-1, keepdims=True) + EPS)
    h = x * r1 * (n1g_ref[...] * (1.0 + sc1)) + sh1
    p = jnp.dot(h.astype(BF16), win_ref[...], preferred_element_type=F32)

    a = p[:, 0:gw]
    _carry_history(ea_ref, a, j, ts, HALO)

    def a_shift(s, lo, hi):
        return ea_ref[HALO - s:HALO - s + ts, lo:hi]

    lane = lax.broadcasted_iota(jnp.int32, (ts, LANES), 1)
    pos1 = (lax.broadcasted_iota(jnp.int32, (ts, LANES), 0) + (j * ts + 1)).astype(F32)
    a_lo = a[:, 0:LANES]
    s01 = a_lo + a_shift(1, 0, LANES)
    s03 = s01 + (a_shift(2, 0, LANES) + a_shift(3, 0, LANES))
    first = lane < POOL_CG
    num_lo = jnp.where(first, s01, s03)
    den_lo = jnp.minimum(pos1, jnp.where(first, float(POOL_WINDOWS[0]), float(POOL_WINDOWS[1])))
    a_hi = a[:, LANES:gw]
    s07 = a_hi
    for s in range(1, 8):
        s07 = s07 + a_shift(s, LANES, gw)
    s815 = a_shift(8, LANES, gw)
    for s in range(9, 16):
        s815 = s815 + a_shift(s, LANES, gw)
    num_hi = jnp.where(first, s07, s07 + s815)
    den_hi = jnp.minimum(pos1, jnp.where(first, float(POOL_WINDOWS[2]), float(POOL_WINDOWS[3])))
    d_pool = jnp.concatenate([num_lo / den_lo - a_lo, num_hi / den_hi - a_hi], axis=1)
    y_a = jnp.dot(d_pool.astype(BF16), poolw_ref[...], preferred_element_type=F32) * pools_ref[...]

    glu = p[:, gw:2 * gw] * jax.nn.sigmoid(p[:, 2 * gw:3 * gw])
    _carry_history_slabs(eg_ref, glu, j, ts, HALO)
    conv_cols = []
    for cb in range(gw // LANES):
        conv_chunks = []
        for r0 in range(0, ts, CONV_ROWS):
            acc = None
            for k in range(CONV_WIDTH):
                off = HALO - (CONV_WIDTH - 1) + k + r0
                term = eg_ref[cb, off:off + CONV_ROWS, :] * cdw_ref[k:k + 1, cb * LANES:(cb + 1) * LANES]
                acc = term if acc is None else acc + term
            conv_chunks.append(acc)
        conv_cols.append(jnp.concatenate(conv_chunks, axis=0))
    hb = jnp.concatenate(conv_cols, axis=1) + cdb_ref[...]
    hb = jax.nn.silu(_layer_norm_rows(hb, clg_ref[...], clb_ref[...]))
    y_b = jnp.dot(hb.astype(BF16), cpw_ref[...], preferred_element_type=F32) + cpb_ref[...]

    u = p[:, 3 * gw:4 * gw]
    vln = _layer_norm_rows(p[:, 4 * gw:5 * gw], slg_ref[...], slb_ref[...]).astype(BF16)
    rowi = lax.broadcasted_iota(jnp.int32, (SG_CHUNK, SG_CHUNK), 0)
    coli = lax.broadcasted_iota(jnp.int32, (SG_CHUNK, SG_CHUNK), 1)
    tril = coli <= rowi
    wcat = jnp.concatenate(
        [jnp.where(tril, sgw_ref[hh], 0.0) for hh in range(SG_HEADS)], axis=1).astype(BF16)
    lane_g = lax.broadcasted_iota(jnp.int32, (SG_CHUNK, gw), 1) // (gw // SG_HEADS)
    zero_bf = jnp.zeros((SG_CHUNK, gw), BF16)
    yc_chunks = []
    for n in range(ts // SG_CHUNK):
        vch = vln[n * SG_CHUNK:(n + 1) * SG_CHUNK, :]
        vbd = jnp.concatenate([jnp.where(lane_g == hh, vch, zero_bf) for hh in range(SG_HEADS)], axis=0)
        mixed = jnp.dot(wcat, vbd, preferred_element_type=F32) + sgb_ref[...]
        yc_chunks.append(u[n * SG_CHUNK:(n + 1) * SG_CHUNK, :] * mixed)
    y_c = jnp.concatenate(yc_chunks, axis=0)

    cx = p[:, 6 * gw:7 * gw] * p[:, 7 * gw:8 * gw]
    _carry_history(ed_ref, cx, j, ts, SUBLANES)
    conv_d = (cx * scw_ref[2:3, :]
              + ed_ref[SUBLANES - 1:SUBLANES - 1 + ts, :] * scw_ref[1:2, :]
              + ed_ref[SUBLANES - 2:SUBLANES - 2 + ts, :] * scw_ref[0:1, :])
    y_d = p[:, 5 * gw:6 * gw] * conv_d

    normed = []
    for gi, yg in enumerate((y_a, y_b, y_c, y_d)):
        rg = lax.rsqrt(jnp.mean(yg * yg, axis=-1, keepdims=True) + EPS)
        normed.append((yg * rg * ong_ref[:, gi * gw:(gi + 1) * gw]).astype(BF16))
    yn = jnp.concatenate(normed, axis=1)
    x1 = x + g1 * jnp.dot(yn, wout_ref[...], preferred_element_type=F32)
    x1_ref[0] = x1

    r2 = lax.rsqrt(jnp.mean(x1 * x1, axis=-1, keepdims=True) + EPS)
    h2 = (x1 * r2 * (n2g_ref[...] * (1.0 + sc2)) + sh2).astype(BF16)
    h2_ref[0] = h2
    logits = lax.dot_general(rwt_ref[...], h2, (((1,), (1,)), ((), ())),
                             preferred_element_type=F32)
    scores = jax.nn.sigmoid(logits)
    sel = scores + rb_ref[...]
    epg = EXPERTS_PER_GROUP
    sel_s = [sel[jj * N_EXPERT_GROUPS:(jj + 1) * N_EXPERT_GROUPS, :] for jj in range(epg)]
    sco_s = [scores[jj * N_EXPERT_GROUPS:(jj + 1) * N_EXPERT_GROUPS, :] for jj in range(epg)]
    top2 = None
    for ja in range(epg):
        for jb in range(ja + 1, epg):
            pair = sel_s[ja] + sel_s[jb]
            top2 = pair if top2 is None else jnp.maximum(top2, pair)
    gidx, _ = _first_argmax([top2[g:g + 1, :] for g in range(N_EXPERT_GROUPS)])
    sel_c = [_pick(gidx, [sel_s[jj][g:g + 1, :] for g in range(N_EXPERT_GROUPS)]) for jj in range(epg)]
    sco_c = [_pick(gidx, [sco_s[jj][g:g + 1, :] for g in range(N_EXPERT_GROUPS)]) for jj in range(epg)]
    i1, _ = _first_argmax(sel_c)
    i2, _ = _first_argmax([jnp.where(i1 == float(jj), -jnp.inf, sel_c[jj]) for jj in range(epg)])
    w1 = _pick(i1, sco_c)
    w2 = _pick(i2, sco_c)
    wsum = w1 + w2
    rows = (gidx * float(epg) + i1, gidx * float(epg) + i2, w1 / wsum, w2 / wsum)
    rid = lax.broadcasted_iota(jnp.int32, (SUBLANES, ts), 0)
    info = jnp.zeros((SUBLANES, ts), F32)
    for k, row in enumerate(rows):
        info = jnp.where(rid == k, row, info)
    info = jnp.concatenate([info, jnp.zeros((LANES - SUBLANES, ts), F32)], axis=0)
    tok_ref[0] = info.T


def _mixer_layer(x, yprev, g2prev, mod, lw, ts):
    b, s, d = x.shape
    has_prev = yprev is not None
    tile = lambda bi, j: (bi, j, 0)
    const2 = lambda bi, j: (0, 0)
    const3 = lambda bi, j: (0, 0, 0)
    per_b = lambda bi, j: (bi, 0, 0)

    args = [x]
    in_specs = [pl.BlockSpec((1, ts, d), tile)]
    if has_prev:
        args += [yprev, g2prev]
        in_specs += [pl.BlockSpec((1, ts, d), tile), pl.BlockSpec((1, 1, d), per_b)]
    args += [mod]
    in_specs += [pl.BlockSpec((1, N_MOD, d), per_b)]
    for name in ("n1g", "n2g", "w_in", "pool_w", "pool_scale", "conv_dw", "conv_db", "conv_ln_g",
                 "conv_ln_b", "conv_pw", "conv_pb", "sg_ln_g", "sg_ln_b", "sg_w", "sg_b", "sc_w",
                 "out_norm_g", "w_out", "router_wt", "router_b"):
        arr = lw[name]
        args.append(arr)
        in_specs.append(pl.BlockSpec(arr.shape, const3 if arr.ndim == 3 else const2))

    out_shape = (jax.ShapeDtypeStruct((b, s, d), F32),
                 jax.ShapeDtypeStruct((b, s, d), BF16),
                 jax.ShapeDtypeStruct((b, s, LANES), F32))
    out_specs = (pl.BlockSpec((1, ts, d), tile),
                 pl.BlockSpec((1, ts, d), tile),
                 pl.BlockSpec((1, ts, LANES), tile))
    scratch = [pltpu.VMEM((HALO + ts, GROUP_W), F32),
               pltpu.VMEM((GROUP_W // LANES, HALO + ts, LANES), F32),
               pltpu.VMEM((SUBLANES + ts, GROUP_W), F32)]
    return pl.pallas_call(
        functools.partial(_mixer_body, has_prev, ts),
        grid=(b, s // ts),
        in_specs=in_specs,
        out_specs=out_specs,
        out_shape=out_shape,
        scratch_shapes=scratch,
        compiler_params=pltpu.CompilerParams(
            dimension_semantics=("arbitrary", "arbitrary"), vmem_limit_bytes=VMEM_LIMIT),
        name="mixer_layer",
    )(*args)


def _moe_body(h_ref, tok_ref, wgu_ref, wd_ref, y_ref):
    e = pl.program_id(1)

    @pl.when(e == 0)
    def _():
        y_ref[...] = jnp.zeros_like(y_ref)

    hgu = jnp.dot(h_ref[...], wgu_ref[0], preferred_element_type=F32)
    info = tok_ref[...]
    ef = e.astype(F32)
    comb = (jnp.where(info[:, 0:1] == ef, info[:, 2:3], 0.0)
            + jnp.where(info[:, 1:2] == ef, info[:, 3:4], 0.0))
    act = jax.nn.silu(hgu[:, :D_EXPERT]) * hgu[:, D_EXPERT:] * comb
    y_ref[...] += jnp.dot(act.astype(BF16), wd_ref[0], preferred_element_type=F32)


def _moe_dense(h2, tok, wgu, wd, tm):
    t, d = h2.shape
    return pl.pallas_call(
        _moe_body,
        grid=(t // tm, N_EXPERTS),
        in_specs=[
            pl.BlockSpec((tm, d), lambda i, e: (i, 0)),
            pl.BlockSpec((tm, LANES), lambda i, e: (i, 0)),
            pl.BlockSpec((1, d, 2 * D_EXPERT), lambda i, e: (e, 0, 0)),
            pl.BlockSpec((1, D_EXPERT, d), lambda i, e: (e, 0, 0)),
        ],
        out_specs=pl.BlockSpec((tm, d), lambda i, e: (i, 0)),
        out_shape=jax.ShapeDtypeStruct((t, d), F32),
        compiler_params=pltpu.CompilerParams(
            dimension_semantics=("arbitrary", "arbitrary"), vmem_limit_bytes=VMEM_LIMIT),
        name="moe_dense",
    )(h2, tok, wgu, wd)


def _final_body(x_ref, y_ref, g2_ref, g_ref, o_ref):
    x = x_ref[0] + g2_ref[0] * y_ref[0]
    r = lax.rsqrt(jnp.mean(x * x, axis=-1, keepdims=True) + EPS)
    o_ref[0] = x * r * g_ref[...]


def _final_norm(x, y, g2, final_g, ts):
    b, s, d = x.shape
    tile = lambda bi, j: (bi, j, 0)
    return pl.pallas_call(
        _final_body,
        grid=(b, s // ts),
        in_specs=[pl.BlockSpec((1, ts, d), tile), pl.BlockSpec((1, ts, d), tile),
                  pl.BlockSpec((1, 1, d), lambda bi, j: (bi, 0, 0)),
                  pl.BlockSpec((1, d), lambda bi, j: (0, 0))],
        out_specs=pl.BlockSpec((1, ts, d), tile),
        out_shape=jax.ShapeDtypeStruct((b, s, d), F32),
        compiler_params=pltpu.CompilerParams(
            dimension_semantics=("arbitrary", "arbitrary"), vmem_limit_bytes=VMEM_LIMIT),
        name="final_norm",
    )(x, y, g2, final_g.reshape(1, d))


def _block_diag(blocks):
    g, n, m = blocks.shape
    out = jnp.zeros((g * n, g * m), blocks.dtype)
    for i in range(g):
        out = out.at[i * n:(i + 1) * n, i * m:(i + 1) * m].set(blocks[i])
    return out


def _layer_weights(l, norm1_g, norm2_g, w_in, pool_w, pool_scale, conv_dw, conv_db, conv_ln_g,
                   conv_ln_b, conv_pw, conv_pb, sg_ln_g, sg_ln_b, sg_w, sg_b, sc_w, out_norm_g,
                   w_out, router_wt, router_b):
    row = lambda v: v.reshape(1, -1)
    return {
        "n1g": row(norm1_g[l]), "n2g": row(norm2_g[l]),
        "w_in": w_in[l].astype(BF16),
        "pool_w": _block_diag(pool_w[l]).astype(BF16),
        "pool_scale": row(pool_scale[l]),
        "conv_dw": jnp.pad(conv_dw[l], ((0, 1), (0, 0))),
        "conv_db": row(conv_db[l]), "conv_ln_g": row(conv_ln_g[l]), "conv_ln_b": row(conv_ln_b[l]),
        "conv_pw": conv_pw[l].astype(BF16), "conv_pb": row(conv_pb[l]),
        "sg_ln_g": row(sg_ln_g[l]), "sg_ln_b": row(sg_ln_b[l]),
        "sg_w": sg_w[l],
        "sg_b": jnp.repeat(sg_b[l].T, GROUP_W // SG_HEADS, axis=1),
        "sc_w": jnp.pad(sc_w[l], ((0, SUBLANES - SC_WIDTH), (0, 0))),
        "out_norm_g": row(out_norm_g[l]),
        "w_out": w_out[l].astype(BF16),
        "router_wt": router_wt, "router_b": router_b,
    }


def kernel(x, c, norm1_g, norm2_g, w_ada, b_ada, w_in, pool_w, pool_scale, conv_dw, conv_db, conv_ln_g, conv_ln_b, conv_pw, conv_pb, sg_ln_g, sg_ln_b, sg_w, sg_b, sc_w, out_norm_g, w_out, router_w, router_bias, exp_w_gate, exp_w_up, exp_w_down, final_g):
    b, s, d = x.shape
    depth = w_in.shape[0]
    ts = 512
    tm = 1024

    mods = _ada_mod(c, w_ada, b_ada)

    perm = lambda v: v.reshape(N_EXPERT_GROUPS, EXPERTS_PER_GROUP, -1).transpose(1, 0, 2).reshape(
        N_EXPERTS, -1)
    router_wt = perm(router_w.T).astype(BF16)
    router_b = perm(router_bias.reshape(N_EXPERTS, 1))

    xcur, yprev, g2prev = x, None, None
    for l in range(depth):
        lw = _layer_weights(l, norm1_g, norm2_g, w_in, pool_w, pool_scale, conv_dw, conv_db,
                            conv_ln_g, conv_ln_b, conv_pw, conv_pb, sg_ln_g, sg_ln_b, sg_w, sg_b,
                            sc_w, out_norm_g, w_out, router_wt, router_b)
        x1, h2, tok = _mixer_layer(xcur, yprev, g2prev, mods[l], lw, ts)
        wgu = jnp.concatenate([exp_w_gate[l], exp_w_up[l]], axis=-1).astype(BF16)
        wd = exp_w_down[l].astype(BF16)
        y = _moe_dense(h2.reshape(b * s, d), tok.reshape(b * s, LANES), wgu, wd, tm)
        xcur, yprev, g2prev = x1, y.reshape(b, s, d), mods[l][:, 5:6, :]
    return _final_norm(xcur, yprev, g2prev, final_g, ts)
```

```python
import functools

import jax
import jax.numpy as jnp
from jax import lax
from jax.experimental import pallas as pl
from jax.experimental.pallas import tpu as pltpu

D_MODEL = 1024
GROUP_W = 256
POOL_WINDOWS = (2, 4, 8, 16)
POOL_CG = 64
CONV_WIDTH = 31
SG_CHUNK = 128
SG_HEADS = 4
SC_WIDTH = 3
N_EXPERTS = 16
N_EXPERT_GROUPS = 4
EXPERTS_PER_GROUP = 4
PAIRS_PER_GROUP = 6
N_CLASSES = N_EXPERT_GROUPS * PAIRS_PER_GROUP
D_EXPERT = 256
N_MOD = 6
EPS = 1e-6
LN_EPS = 1e-5

LANES = 128
SUBLANES = 8
CONV_HALO = 32
POOL_HALO = 16
SC_HALO = 8
CONV_ROWS = 32
INFO_CLS, INFO_WA, INFO_WB = 0, 2, 3
VMEM_LIMIT = 56 * 1024 * 1024

F32 = jnp.float32
BF16 = jnp.bfloat16
I32 = jnp.int32


def _ada_body(c_ref, w_ref, b_ref, o_ref):
    ca = jax.nn.silu(c_ref[...])
    o_ref[0] = jnp.dot(ca, w_ref[0], preferred_element_type=F32,
                       precision=lax.Precision.HIGHEST) + b_ref[0]


def _ada_mod(c, w_ada, b_ada):
    depth, d, n = w_ada.shape
    b = c.shape[0]
    tn = 1024
    cp = jnp.pad(c, ((0, SUBLANES - b), (0, 0)))
    out = pl.pallas_call(
        _ada_body,
        grid=(depth, n // tn),
        in_specs=[
            pl.BlockSpec((SUBLANES, d), lambda l, j: (0, 0)),
            pl.BlockSpec((1, d, tn), lambda l, j: (l, 0, j)),
            pl.BlockSpec((1, 1, tn), lambda l, j: (l, 0, j)),
        ],
        out_specs=pl.BlockSpec((1, SUBLANES, tn), lambda l, j: (l, 0, j)),
        out_shape=jax.ShapeDtypeStruct((depth, SUBLANES, n), F32),
        compiler_params=pltpu.CompilerParams(
            dimension_semantics=("arbitrary", "arbitrary"), vmem_limit_bytes=VMEM_LIMIT),
        name="ada_mod",
    )(cp, w_ada, b_ada.reshape(depth, 1, n))
    return out[:, :b].reshape(depth, b, N_MOD, d)


def _layer_norm_rows(v, g, b):
    mu = jnp.mean(v, axis=-1, keepdims=True)
    vc = v - mu
    var = jnp.mean(vc * vc, axis=-1, keepdims=True)
    return vc * lax.rsqrt(var + LN_EPS) * g + b


def _carry_history(ext_ref, cur, j, ts, halo):
    nslab = ext_ref.shape[0]

    @pl.when(j == 0)
    def _():
        ext_ref[:, 0:halo, :] = jnp.zeros((nslab, halo, LANES), F32)

    @pl.when(j > 0)
    def _():
        ext_ref[:, 0:halo, :] = ext_ref[:, ts:ts + halo, :]

    for cb in range(nslab):
        ext_ref[cb, halo:halo + ts, :] = cur[:, cb * LANES:(cb + 1) * LANES]


def _first_argmax(vals):
    best = vals[0]
    idx = jnp.zeros_like(best)
    for k in range(1, len(vals)):
        better = vals[k] > best
        idx = jnp.where(better, float(k), idx)
        best = jnp.where(better, vals[k], best)
    return idx, best


def _pick(idx, vals):
    out = vals[0]
    for k in range(1, len(vals)):
        out = jnp.where(idx == float(k), vals[k], out)
    return out


def _mixer_body(has_prev, ts, *refs):
    refs = list(refs)
    x_ref = refs.pop(0)
    if has_prev:
        yprev_ref = refs.pop(0)
        g2p_ref = refs.pop(0)
    (mod_ref, n1g_ref, n2g_ref, win_ref, poolw_ref, pools_ref,
     cdw_ref, cdb_ref, clg_ref, clb_ref, cpw_ref, cpb_ref,
     slg_ref, slb_ref, sgw_ref, sgb_ref, scw_ref, ong_ref, wout_ref, rwt_ref, rb_ref,
     x1_ref, h2_ref, cls_ref,
     ea_ref, eg_ref, ed_ref) = refs

    j = pl.program_id(1)
    gw = GROUP_W
    d = D_MODEL

    x = x_ref[...]
    if has_prev:
        x = x + g2p_ref[0] * yprev_ref[...]

    sh1 = mod_ref[0, 0:1, :]
    sc1 = mod_ref[0, 1:2, :]
    g1 = mod_ref[0, 2:3, :]
    sh2 = mod_ref[0, 3:4, :]
    sc2 = mod_ref[0, 4:5, :]

    r1 = lax.rsqrt(jnp.mean(x * x, axis=-1, keepdims=True) + EPS)
    h = x * r1 * (n1g_ref[...] * (1.0 + sc1)) + sh1
    p = jnp.dot(h.astype(BF16), win_ref[...], preferred_element_type=F32)

    a = p[:, 0:gw]
    _carry_history(ea_ref, a, j, ts, POOL_HALO)

    def a_shift(s, cb):
        return ea_ref[cb, POOL_HALO - s:POOL_HALO - s + ts, :]

    lane = lax.broadcasted_iota(I32, (ts, LANES), 1)
    pos1 = (lax.broadcasted_iota(I32, (ts, LANES), 0) + (j * ts + 1)).astype(F32)
    first = lane < POOL_CG
    a_lo = a[:, 0:LANES]
    s01 = a_lo + a_shift(1, 0)
    s03 = s01 + (a_shift(2, 0) + a_shift(3, 0))
    num_lo = jnp.where(first, s01, s03)
    den_lo = jnp.minimum(pos1, jnp.where(first, float(POOL_WINDOWS[0]), float(POOL_WINDOWS[1])))
    a_hi = a[:, LANES:gw]
    s07 = a_hi
    for s in range(1, 8):
        s07 = s07 + a_shift(s, 1)
    s815 = a_shift(8, 1)
    for s in range(9, 16):
        s815 = s815 + a_shift(s, 1)
    num_hi = jnp.where(first, s07, s07 + s815)
    den_hi = jnp.minimum(pos1, jnp.where(first, float(POOL_WINDOWS[2]), float(POOL_WINDOWS[3])))
    d_pool = jnp.concatenate([num_lo / den_lo - a_lo, num_hi / den_hi - a_hi], axis=1)
    y_a = jnp.dot(d_pool.astype(BF16), poolw_ref[...], preferred_element_type=F32) * pools_ref[...]

    glu = p[:, gw:2 * gw] * jax.nn.sigmoid(p[:, 2 * gw:3 * gw])
    _carry_history(eg_ref, glu, j, ts, CONV_HALO)
    conv_cols = []
    for cb in range(gw // LANES):
        conv_chunks = []
        for r0 in range(0, ts, CONV_ROWS):
            acc = None
            for k in range(CONV_WIDTH):
                off = CONV_HALO - (CONV_WIDTH - 1) + k + r0
                term = eg_ref[cb, off:off + CONV_ROWS, :] * cdw_ref[k:k + 1, cb * LANES:(cb + 1) * LANES]
                acc = term if acc is None else acc + term
            conv_chunks.append(acc)
        conv_cols.append(jnp.concatenate(conv_chunks, axis=0))
    hb = jnp.concatenate(conv_cols, axis=1) + cdb_ref[...]
    hb = jax.nn.silu(_layer_norm_rows(hb, clg_ref[...], clb_ref[...]))
    y_b = jnp.dot(hb.astype(BF16), cpw_ref[...], preferred_element_type=F32) + cpb_ref[...]

    u = p[:, 3 * gw:4 * gw]
    vln = _layer_norm_rows(p[:, 4 * gw:5 * gw], slg_ref[...], slb_ref[...]).astype(BF16)
    rowi = lax.broadcasted_iota(I32, (SG_CHUNK, SG_CHUNK), 0)
    coli = lax.broadcasted_iota(I32, (SG_CHUNK, SG_CHUNK), 1)
    tril = coli <= rowi
    wcat = jnp.concatenate(
        [jnp.where(tril, sgw_ref[hh], 0.0) for hh in range(SG_HEADS)], axis=1).astype(BF16)
    lane_g = lax.broadcasted_iota(I32, (SG_CHUNK, gw), 1) // (gw // SG_HEADS)
    zero_bf = jnp.zeros((SG_CHUNK, gw), BF16)
    yc_chunks = []
    for n in range(ts // SG_CHUNK):
        vch = vln[n * SG_CHUNK:(n + 1) * SG_CHUNK, :]
        vbd = jnp.concatenate([jnp.where(lane_g == hh, vch, zero_bf) for hh in range(SG_HEADS)], axis=0)
        mixed = jnp.dot(wcat, vbd, preferred_element_type=F32) + sgb_ref[...]
        yc_chunks.append(u[n * SG_CHUNK:(n + 1) * SG_CHUNK, :] * mixed)
    y_c = jnp.concatenate(yc_chunks, axis=0)

    cx = p[:, 6 * gw:7 * gw] * p[:, 7 * gw:8 * gw]
    _carry_history(ed_ref, cx, j, ts, SC_HALO)
    cd_cols = []
    for cb in range(gw // LANES):
        ls = slice(cb * LANES, (cb + 1) * LANES)
        cd_cols.append(cx[:, ls] * scw_ref[2:3, ls]
                       + ed_ref[cb, SC_HALO - 1:SC_HALO - 1 + ts, :] * scw_ref[1:2, ls]
                       + ed_ref[cb, SC_HALO - 2:SC_HALO - 2 + ts, :] * scw_ref[0:1, ls])
    y_d = p[:, 5 * gw:6 * gw] * jnp.concatenate(cd_cols, axis=1)

    normed = []
    for gi, yg in enumerate((y_a, y_b, y_c, y_d)):
        rg = lax.rsqrt(jnp.mean(yg * yg, axis=-1, keepdims=True) + EPS)
        normed.append((yg * rg * ong_ref[:, gi * gw:(gi + 1) * gw]).astype(BF16))
    yn = jnp.concatenate(normed, axis=1)
    x1 = x + g1 * jnp.dot(yn, wout_ref[...], preferred_element_type=F32)
    x1_ref[...] = x1

    r2 = lax.rsqrt(jnp.mean(x1 * x1, axis=-1, keepdims=True) + EPS)
    h2 = x1 * r2 * (n2g_ref[...] * (1.0 + sc2)) + sh2
    h2_ref[:, 0:d] = h2
    logits = lax.dot_general(rwt_ref[...], h2.astype(BF16), (((1,), (1,)), ((), ())),
                             preferred_element_type=F32)
    scores = jax.nn.sigmoid(logits)
    sel = scores + rb_ref[...]
    epg = EXPERTS_PER_GROUP
    sel_s = [sel[jj * N_EXPERT_GROUPS:(jj + 1) * N_EXPERT_GROUPS, :] for jj in range(epg)]
    sco_s = [scores[jj * N_EXPERT_GROUPS:(jj + 1) * N_EXPERT_GROUPS, :] for jj in range(epg)]
    top2 = None
    for ja in range(epg):
        for jb in range(ja + 1, epg):
            pair = sel_s[ja] + sel_s[jb]
            top2 = pair if top2 is None else jnp.maximum(top2, pair)
    gidx, _ = _first_argmax([top2[g:g + 1, :] for g in range(N_EXPERT_GROUPS)])
    sel_c = [_pick(gidx, [sel_s[jj][g:g + 1, :] for g in range(N_EXPERT_GROUPS)]) for jj in range(epg)]
    sco_c = [_pick(gidx, [sco_s[jj][g:g + 1, :] for g in range(N_EXPERT_GROUPS)]) for jj in range(epg)]
    i1, _ = _first_argmax(sel_c)
    i2, _ = _first_argmax([jnp.where(i1 == float(jj), -jnp.inf, sel_c[jj]) for jj in range(epg)])
    w1 = _pick(i1, sco_c)
    w2 = _pick(i2, sco_c)
    wsum = w1 + w2
    w1 = w1 / wsum
    w2 = w2 / wsum
    swap = i2 < i1
    lo = jnp.where(swap, i2, i1)
    hi = jnp.where(swap, i1, i2)
    base = jnp.where(lo == 0.0, 0.0, jnp.where(lo == 1.0, 3.0, 5.0))
    cls = gidx * float(PAIRS_PER_GROUP) + base + (hi - lo - 1.0)
    wa = jnp.where(swap, w2, w1)
    wb = jnp.where(swap, w1, w2)
    cls_ref[0] = cls.astype(I32)
    rid = lax.broadcasted_iota(I32, (SUBLANES, ts), 0)
    info = jnp.zeros((SUBLANES, ts), F32)
    for k, row in ((INFO_CLS, cls), (INFO_WA, wa), (INFO_WB, wb)):
        info = jnp.where(rid == k, row, info)
    info = jnp.concatenate([info, jnp.zeros((LANES - SUBLANES, ts), F32)], axis=0)
    h2_ref[:, d:d + LANES] = info.T


def _mixer_layer(x, yprev, g2prev, mod, lw, ts, batch):
    t, d = x.shape
    nj = t // batch // ts
    has_prev = yprev is not None
    tile = lambda bi, j: (bi * nj + j, 0)
    const2 = lambda bi, j: (0, 0)
    const3 = lambda bi, j: (0, 0, 0)
    per_b = lambda bi, j: (bi, 0, 0)

    args = [x]
    in_specs = [pl.BlockSpec((ts, d), tile)]
    if has_prev:
        args += [yprev, g2prev]
        in_specs += [pl.BlockSpec((ts, d), tile), pl.BlockSpec((1, 1, d), per_b)]
    args += [mod]
    in_specs += [pl.BlockSpec((1, N_MOD, d), per_b)]
    for name in ("n1g", "n2g", "w_in", "pool_w", "pool_scale", "conv_dw", "conv_db", "conv_ln_g",
                 "conv_ln_b", "conv_pw", "conv_pb", "sg_ln_g", "sg_ln_b", "sg_w", "sg_b", "sc_w",
                 "out_norm_g", "w_out", "router_wt", "router_b"):
        arr = lw[name]
        args.append(arr)
        in_specs.append(pl.BlockSpec(arr.shape, const3 if arr.ndim == 3 else const2))

    out_shape = (jax.ShapeDtypeStruct((t, d), F32),
                 jax.ShapeDtypeStruct((t, d + LANES), F32),
                 jax.ShapeDtypeStruct((t // ts, 1, ts), I32))
    out_specs = (pl.BlockSpec((ts, d), tile),
                 pl.BlockSpec((ts, d + LANES), tile),
                 pl.BlockSpec((1, 1, ts), lambda bi, j: (bi * nj + j, 0, 0)))
    nslab = GROUP_W // LANES
    scratch = [pltpu.VMEM((nslab, POOL_HALO + ts, LANES), F32),
               pltpu.VMEM((nslab, CONV_HALO + ts, LANES), F32),
               pltpu.VMEM((nslab, SC_HALO + ts, LANES), F32)]
    return pl.pallas_call(
        functools.partial(_mixer_body, has_prev, ts),
        grid=(batch, nj),
        in_specs=in_specs,
        out_specs=out_specs,
        out_shape=out_shape,
        scratch_shapes=scratch,
        compiler_params=pltpu.CompilerParams(
            dimension_semantics=("arbitrary", "arbitrary"), vmem_limit_bytes=VMEM_LIMIT),
        name="mixer_layer",
    )(*args)


def _plan_body(tm, cls_ref, pos_ref, meta_ref):
    cls = cls_ref[...]
    nrow = cls.shape[0]
    ii = lax.broadcasted_iota(I32, (LANES, LANES), 0)
    jj = lax.broadcasted_iota(I32, (LANES, LANES), 1)
    upper = jnp.where(ii <= jj, 1.0, 0.0).astype(BF16)
    ri = lax.broadcasted_iota(I32, (nrow, nrow), 0)
    rj = lax.broadcasted_iota(I32, (nrow, nrow), 1)
    lower = jnp.where(rj < ri, 1.0, 0.0).astype(BF16)
    tile_i = lax.broadcasted_iota(I32, (1, LANES), 1).astype(F32)
    start = jnp.zeros((1, 1), F32)
    cum_tiles = jnp.zeros((1, 1), F32)
    pos = jnp.zeros((nrow, LANES), F32)
    tclass = jnp.zeros((1, LANES), F32)
    for k in range(N_CLASSES):
        hit = cls == k
        oh = jnp.where(hit, 1.0, 0.0)
        ohb = oh.astype(BF16)
        incl = jnp.dot(ohb, upper, preferred_element_type=F32)
        above = jnp.sum(jnp.dot(lower, ohb, preferred_element_type=F32), axis=1, keepdims=True)
        rank = incl - oh + above
        pos = pos + jnp.where(hit, start + rank, 0.0)
        cnt = jnp.sum(jnp.sum(oh, axis=1, keepdims=True), axis=0, keepdims=True)
        n_tiles = jnp.floor((cnt + float(tm - 1)) * (1.0 / tm))
        start = start + n_tiles * float(tm)
        cum_tiles = cum_tiles + n_tiles
        tclass = tclass + jnp.where(tile_i >= cum_tiles, 1.0, 0.0)
    pos_ref[...] = pos.astype(I32)

    valid = tclass < float(N_CLASSES)
    kc = jnp.minimum(tclass, float(N_CLASSES - 1))
    grp = jnp.zeros_like(kc)
    for g in range(1, N_EXPERT_GROUPS):
        grp = grp + jnp.where(kc >= float(g * PAIRS_PER_GROUP), 1.0, 0.0)
    pr = kc - grp * float(PAIRS_PER_GROUP)
    lo = jnp.where(pr >= 3.0, 1.0, 0.0) + jnp.where(pr >= 5.0, 1.0, 0.0)
    base = jnp.where(lo == 0.0, 0.0, jnp.where(lo == 1.0, 3.0, 5.0))
    hi = pr - base + lo + 1.0
    ea = grp * float(EXPERTS_PER_GROUP) + lo
    eb = grp * float(EXPERTS_PER_GROUP) + hi
    rid = lax.broadcasted_iota(I32, (SUBLANES, LANES), 0)
    meta = jnp.where(rid == 0, ea, jnp.where(rid == 1, eb, jnp.where(rid == 2, jnp.where(valid, 1.0, 0.0), 0.0)))
    meta_ref[...] = meta.astype(I32)


def _route_plan(cls2d, tm):
    nrow = cls2d.shape[0]
    return pl.pallas_call(
        functools.partial(_plan_body, tm),
        in_specs=[pl.BlockSpec((nrow, LANES), lambda: (0, 0))],
        out_specs=(pl.BlockSpec((nrow, LANES), lambda: (0, 0)),
                   pl.BlockSpec((SUBLANES, LANES), lambda: (0, 0))),
        out_shape=(jax.ShapeDtypeStruct((nrow, LANES), I32),
                   jax.ShapeDtypeStruct((SUBLANES, LANES), I32)),
        name="route_plan",
    )(cls2d)


def _invert_body(n_tok, n_slot, pos_ref, src_ref):
    def fill(i, c):
        src_ref[i] = -1
        return c

    lax.fori_loop(0, n_slot, fill, 0, unroll=8)

    def put(t, c):
        src_ref[pos_ref[t]] = t
        return c

    lax.fori_loop(0, n_tok, put, 0, unroll=8)


def _route_invert(pos, n_slot):
    n_tok = pos.shape[0]
    return pl.pallas_call(
        functools.partial(_invert_body, n_tok, n_slot),
        in_specs=[pl.BlockSpec(memory_space=pltpu.SMEM)],
        out_specs=pl.BlockSpec(memory_space=pltpu.SMEM),
        out_shape=jax.ShapeDtypeStruct((n_slot,), I32),
        name="route_invert",
    )(pos)


def _moe_body(tm, nt, n_tok, ea_ref, eb_ref, valid_ref, src_ref,
              h_hbm, wgua_ref, wgub_ref, wda_ref, wdb_ref, y_hbm,
              xbuf, ybuf, gsem, ssem):
    i = pl.program_id(0)
    slot = i % 2
    other = 1 - slot
    d = D_MODEL

    def start_gather(tile, s):
        base = tile * tm
        for r in range(tm):
            tok = jnp.maximum(src_ref[base + r], 0)
            pltpu.make_async_copy(h_hbm.at[pl.ds(tok, 1), :], xbuf.at[s, pl.ds(r, 1), :],
                                  gsem.at[s]).start()

    def wait_gather(s):
        pltpu.make_async_copy(h_hbm.at[pl.ds(0, tm), :], xbuf.at[s], gsem.at[s]).wait()

    def start_scatter(tile, s):
        base = tile * tm
        for r in range(tm):
            idx = src_ref[base + r]
            dst = jnp.where(idx < 0, n_tok + s * tm + r, idx)
            pltpu.make_async_copy(ybuf.at[s, pl.ds(r, 1), :], y_hbm.at[pl.ds(dst, 1), :],
                                  ssem.at[s]).start()

    def wait_scatter(s):
        pltpu.make_async_copy(ybuf.at[s], y_hbm.at[pl.ds(0, tm), :], ssem.at[s]).wait()

    @pl.when(i == 0)
    def _():
        start_gather(0, 0)
        ybuf[...] = jnp.zeros_like(ybuf)
        for s in range(2):
            fill = pltpu.make_async_copy(ybuf.at[s], y_hbm.at[pl.ds(n_tok + s * tm, tm), :], ssem.at[s])
            fill.start()
            fill.wait()

    is_valid = valid_ref[i] == 1
    next_valid = valid_ref[i + 1] == 1

    @pl.when(is_valid)
    def _():
        wait_gather(slot)
        start_gather(i + 1, other)
        xt = xbuf[slot]
        xb = xt[:, 0:d].astype(BF16)
        wa = xt[:, d + INFO_WA:d + INFO_WA + 1]
        wb = xt[:, d + INFO_WB:d + INFO_WB + 1]
        ha = jnp.dot(xb, wgua_ref[0], preferred_element_type=F32)
        hb = jnp.dot(xb, wgub_ref[0], preferred_element_type=F32)
        acta = jax.nn.silu(ha[:, :D_EXPERT]) * ha[:, D_EXPERT:] * wa
        actb = jax.nn.silu(hb[:, :D_EXPERT]) * hb[:, D_EXPERT:] * wb
        ybuf[slot] = (jnp.dot(acta.astype(BF16), wda_ref[0], preferred_element_type=F32)
                      + jnp.dot(actb.astype(BF16), wdb_ref[0], preferred_element_type=F32))
        start_scatter(i, slot)

    @pl.when(jnp.logical_and(is_valid, i > 0))
    def _():
        wait_scatter(other)

    @pl.when(jnp.logical_and(is_valid, jnp.logical_not(next_valid)))
    def _():
        wait_gather(other)
        wait_scatter(slot)


def _moe_sparse(h2aug, ea, eb, valid, src, wgu, wd, tm, nt):
    n_tok, width = h2aug.shape
    d = D_MODEL
    wmap_a = lambda i, ea, eb, valid, src: (ea[i], 0, 0)
    wmap_b = lambda i, ea, eb, valid, src: (eb[i], 0, 0)
    grid_spec = pltpu.PrefetchScalarGridSpec(
        num_scalar_prefetch=4,
        grid=(nt,),
        in_specs=[
            pl.BlockSpec(memory_space=pl.ANY),
            pl.BlockSpec((1, d, 2 * D_EXPERT), wmap_a),
            pl.BlockSpec((1, d, 2 * D_EXPERT), wmap_b),
            pl.BlockSpec((1, D_EXPERT, d), wmap_a),
            pl.BlockSpec((1, D_EXPERT, d), wmap_b),
        ],
        out_specs=pl.BlockSpec(memory_space=pl.ANY),
        scratch_shapes=[pltpu.VMEM((2, tm, width), F32),
                        pltpu.VMEM((2, tm, d), F32),
                        pltpu.SemaphoreType.DMA((2,)),
                        pltpu.SemaphoreType.DMA((2,))],
    )
    return pl.pallas_call(
        functools.partial(_moe_body, tm, nt, n_tok),
        grid_spec=grid_spec,
        out_shape=jax.ShapeDtypeStruct((n_tok + 2 * tm, d), F32),
        compiler_params=pltpu.CompilerParams(
            dimension_semantics=("arbitrary",), vmem_limit_bytes=VMEM_LIMIT),
        name="moe_sparse",
    )(ea, eb, valid, src, h2aug, wgu, wgu, wd, wd)


def _final_body(x_ref, y_ref, g2_ref, g_ref, o_ref):
    x = x_ref[...] + g2_ref[0] * y_ref[...]
    r = lax.rsqrt(jnp.mean(x * x, axis=-1, keepdims=True) + EPS)
    o_ref[...] = x * r * g_ref[...]


def _final_norm(x, y, g2, final_g, ts, batch):
    t, d = x.shape
    nj = t // batch // ts
    tile = lambda bi, j: (bi * nj + j, 0)
    return pl.pallas_call(
        _final_body,
        grid=(batch, nj),
        in_specs=[pl.BlockSpec((ts, d), tile), pl.BlockSpec((ts, d), tile),
                  pl.BlockSpec((1, 1, d), lambda bi, j: (bi, 0, 0)),
                  pl.BlockSpec((1, d), lambda bi, j: (0, 0))],
        out_specs=pl.BlockSpec((ts, d), tile),
        out_shape=jax.ShapeDtypeStruct((t, d), F32),
        compiler_params=pltpu.CompilerParams(
            dimension_semantics=("arbitrary", "arbitrary"), vmem_limit_bytes=VMEM_LIMIT),
        name="final_norm",
    )(x, y, g2, final_g.reshape(1, d))


def _block_diag(blocks):
    g, n, m = blocks.shape
    out = jnp.zeros((g * n, g * m), blocks.dtype)
    for i in range(g):
        out = out.at[i * n:(i + 1) * n, i * m:(i + 1) * m].set(blocks[i])
    return out


def _layer_weights(l, norm1_g, norm2_g, w_in, pool_w, pool_scale, conv_dw, conv_db, conv_ln_g,
                   conv_ln_b, conv_pw, conv_pb, sg_ln_g, sg_ln_b, sg_w, sg_b, sc_w, out_norm_g,
                   w_out, router_wt, router_b):
    row = lambda v: v.reshape(1, -1)
    return {
        "n1g": row(norm1_g[l]), "n2g": row(norm2_g[l]),
        "w_in": w_in[l].astype(BF16),
        "pool_w": _block_diag(pool_w[l]).astype(BF16),
        "pool_scale": row(pool_scale[l]),
        "conv_dw": jnp.pad(conv_dw[l], ((0, 1), (0, 0))),
        "conv_db": row(conv_db[l]), "conv_ln_g": row(conv_ln_g[l]), "conv_ln_b": row(conv_ln_b[l]),
        "conv_pw": conv_pw[l].astype(BF16), "conv_pb": row(conv_pb[l]),
        "sg_ln_g": row(sg_ln_g[l]), "sg_ln_b": row(sg_ln_b[l]),
        "sg_w": sg_w[l],
        "sg_b": jnp.repeat(sg_b[l].T, GROUP_W // SG_HEADS, axis=1),
        "sc_w": jnp.pad(sc_w[l], ((0, SUBLANES - SC_WIDTH), (0, 0))),
        "out_norm_g": row(out_norm_g[l]),
        "w_out": w_out[l].astype(BF16),
        "router_wt": router_wt, "router_b": router_b,
    }


def kernel(x, c, norm1_g, norm2_g, w_ada, b_ada, w_in, pool_w, pool_scale, conv_dw, conv_db, conv_ln_g, conv_ln_b, conv_pw, conv_pb, sg_ln_g, sg_ln_b, sg_w, sg_b, sc_w, out_norm_g, w_out, router_w, router_bias, exp_w_gate, exp_w_up, exp_w_down, final_g):
    b, s, d = x.shape
    depth = w_in.shape[0]
    t = b * s
    ts = 512
    tm = 256
    nt = t // tm + N_CLASSES
    assert t % LANES == 0 and s % ts == 0 and t % tm == 0 and nt <= LANES

    mods = _ada_mod(c, w_ada, b_ada)

    perm = lambda v: v.reshape(N_EXPERT_GROUPS, EXPERTS_PER_GROUP, -1).transpose(1, 0, 2).reshape(
        N_EXPERTS, -1)
    router_wt = perm(router_w.T).astype(BF16)
    router_b = perm(router_bias.reshape(N_EXPERTS, 1))

    xcur, yprev, g2prev = x.reshape(t, d), None, None
    for l in range(depth):
        lw = _layer_weights(l, norm1_g, norm2_g, w_in, pool_w, pool_scale, conv_dw, conv_db,
                            conv_ln_g, conv_ln_b, conv_pw, conv_pb, sg_ln_g, sg_ln_b, sg_w, sg_b,
                            sc_w, out_norm_g, w_out, router_wt, router_b)
        x1, h2aug, cls = _mixer_layer(xcur, yprev, g2prev, mods[l], lw, ts, b)
        pos, meta = _route_plan(cls.reshape(t // LANES, LANES), tm)
        src = _route_invert(pos.reshape(t), (nt + 1) * tm)
        wgu = jnp.concatenate([exp_w_gate[l], exp_w_up[l]], axis=-1).astype(BF16)
        wd = exp_w_down[l].astype(BF16)
        y = _moe_sparse(h2aug, meta[0, :nt + 1], meta[1, :nt + 1], meta[2, :nt + 1], src, wgu, wd, tm, nt)
        xcur, yprev, g2prev = x1, y, mods[l][:, 5:6, :]
    return _final_norm(xcur, yprev, g2prev, final_g, ts, b).reshape(b, s, d)
```

```python
import functools

import jax
import jax.numpy as jnp
from jax import lax
from jax.experimental import pallas as pl
from jax.experimental.pallas import tpu as pltpu

D_MODEL = 1024
GROUP_W = 256
POOL_WINDOWS = (2, 4, 8, 16)
POOL_CG = 64
CONV_WIDTH = 31
SG_CHUNK = 128
SG_HEADS = 4
SC_WIDTH = 3
N_EXPERTS = 16
N_EXPERT_GROUPS = 4
EXPERTS_PER_GROUP = 4
PAIRS_PER_GROUP = 6
N_CLASSES = N_EXPERT_GROUPS * PAIRS_PER_GROUP
D_EXPERT = 256
N_MOD = 6
EPS = 1e-6
LN_EPS = 1e-5

LANES = 128
SUBLANES = 8
REC = SUBLANES
CONV_HALO = 32
POOL_HALO = 16
SC_HALO = 8
CONV_ROWS = 32
META_EA, META_EB, META_VALID, META_BLK, META_END = 0, 1, 2, 3, 4
PERMUTE_ROWS = 512
PERMUTE_GROUP = 16
VMEM_LIMIT = 56 * 1024 * 1024

F32 = jnp.float32
BF16 = jnp.bfloat16
I32 = jnp.int32


def _rows_of(ref, c, n):
    return ref[pl.ds(c, n, stride=REC), :]


def _ada_body(c_ref, w_ref, b_ref, o_ref):
    ca = jax.nn.silu(c_ref[...])
    o_ref[0] = jnp.dot(ca, w_ref[0], preferred_element_type=F32,
                       precision=lax.Precision.HIGHEST) + b_ref[0]


def _ada_mod(c, w_ada, b_ada):
    depth, d, n = w_ada.shape
    b = c.shape[0]
    tn = 1024
    cp = jnp.pad(c, ((0, SUBLANES - b), (0, 0)))
    out = pl.pallas_call(
        _ada_body,
        grid=(depth, n // tn),
        in_specs=[
            pl.BlockSpec((SUBLANES, d), lambda l, j: (0, 0)),
            pl.BlockSpec((1, d, tn), lambda l, j: (l, 0, j)),
            pl.BlockSpec((1, 1, tn), lambda l, j: (l, 0, j)),
        ],
        out_specs=pl.BlockSpec((1, SUBLANES, tn), lambda l, j: (l, 0, j)),
        out_shape=jax.ShapeDtypeStruct((depth, SUBLANES, n), F32),
        compiler_params=pltpu.CompilerParams(
            dimension_semantics=("arbitrary", "arbitrary"), vmem_limit_bytes=VMEM_LIMIT),
        name="ada_mod",
    )(cp, w_ada, b_ada.reshape(depth, 1, n))
    return out[:, :b].reshape(depth, b, N_MOD, d)


def _layer_norm_rows(v, g, b):
    mu = jnp.mean(v, axis=-1, keepdims=True)
    vc = v - mu
    var = jnp.mean(vc * vc, axis=-1, keepdims=True)
    return vc * lax.rsqrt(var + LN_EPS) * g + b


def _carry_history(ext_ref, cur, j, ts, halo):
    nslab = ext_ref.shape[0]

    @pl.when(j == 0)
    def _():
        ext_ref[:, 0:halo, :] = jnp.zeros((nslab, halo, LANES), F32)

    @pl.when(j > 0)
    def _():
        ext_ref[:, 0:halo, :] = ext_ref[:, ts:ts + halo, :]

    for cb in range(nslab):
        ext_ref[cb, halo:halo + ts, :] = cur[:, cb * LANES:(cb + 1) * LANES]


def _first_argmax(vals):
    best = vals[0]
    idx = jnp.zeros_like(best)
    for k in range(1, len(vals)):
        better = vals[k] > best
        idx = jnp.where(better, float(k), idx)
        best = jnp.where(better, vals[k], best)
    return idx, best


def _pick(idx, vals):
    out = vals[0]
    for k in range(1, len(vals)):
        out = jnp.where(idx == float(k), vals[k], out)
    return out


def _mixer_body(has_prev, ts, *refs):
    refs = list(refs)
    x_ref = refs.pop(0)
    if has_prev:
        yprev_ref = refs.pop(0)
        g2p_ref = refs.pop(0)
    (mod_ref, n1g_ref, n2g_ref, win_ref, poolw_ref, pools_ref,
     cdw_ref, cdb_ref, clg_ref, clb_ref, cpw_ref, cpb_ref,
     slg_ref, slb_ref, sgw_ref, sgb_ref, scw_ref, ong_ref, wout_ref, rwt_ref, rb_ref,
     x1_ref, rec_ref, cls_ref,
     ea_ref, eg_ref, ed_ref) = refs

    j = pl.program_id(1)
    gw = GROUP_W

    x = x_ref[...]
    if has_prev:
        yprev = jnp.concatenate([_rows_of(yprev_ref, c, ts) for c in range(REC)], axis=1)
        x = x + g2p_ref[0] * yprev

    sh1 = mod_ref[0, 0:1, :]
    sc1 = mod_ref[0, 1:2, :]
    g1 = mod_ref[0, 2:3, :]
    sh2 = mod_ref[0, 3:4, :]
    sc2 = mod_ref[0, 4:5, :]

    r1 = lax.rsqrt(jnp.mean(x * x, axis=-1, keepdims=True) + EPS)
    h = x * r1 * (n1g_ref[...] * (1.0 + sc1)) + sh1
    p = jnp.dot(h.astype(BF16), win_ref[...], preferred_element_type=F32)

    a = p[:, 0:gw]
    _carry_history(ea_ref, a, j, ts, POOL_HALO)

    def a_shift(s, cb):
        return ea_ref[cb, POOL_HALO - s:POOL_HALO - s + ts, :]

    lane = lax.broadcasted_iota(I32, (ts, LANES), 1)
    pos1 = (lax.broadcasted_iota(I32, (ts, LANES), 0) + (j * ts + 1)).astype(F32)
    first = lane < POOL_CG
    a_lo = a[:, 0:LANES]
    s01 = a_lo + a_shift(1, 0)
    s03 = s01 + (a_shift(2, 0) + a_shift(3, 0))
    num_lo = jnp.where(first, s01, s03)
    den_lo = jnp.minimum(pos1, jnp.where(first, float(POOL_WINDOWS[0]), float(POOL_WINDOWS[1])))
    a_hi = a[:, LANES:gw]
    s07 = a_hi
    for s in range(1, 8):
        s07 = s07 + a_shift(s, 1)
    s815 = a_shift(8, 1)
    for s in range(9, 16):
        s815 = s815 + a_shift(s, 1)
    num_hi = jnp.where(first, s07, s07 + s815)
    den_hi = jnp.minimum(pos1, jnp.where(first, float(POOL_WINDOWS[2]), float(POOL_WINDOWS[3])))
    d_pool = jnp.concatenate([num_lo / den_lo - a_lo, num_hi / den_hi - a_hi], axis=1)
    y_a = jnp.dot(d_pool.astype(BF16), poolw_ref[...], preferred_element_type=F32) * pools_ref[...]

    glu = p[:, gw:2 * gw] * jax.nn.sigmoid(p[:, 2 * gw:3 * gw])
    _carry_history(eg_ref, glu, j, ts, CONV_HALO)
    conv_cols = []
    for cb in range(gw // LANES):
        conv_chunks = []
        for r0 in range(0, ts, CONV_ROWS):
            acc = None
            for k in range(CONV_WIDTH):
                off = CONV_HALO - (CONV_WIDTH - 1) + k + r0
                term = eg_ref[cb, off:off + CONV_ROWS, :] * cdw_ref[k:k + 1, cb * LANES:(cb + 1) * LANES]
                acc = term if acc is None else acc + term
            conv_chunks.append(acc)
        conv_cols.append(jnp.concatenate(conv_chunks, axis=0))
    hb = jnp.concatenate(conv_cols, axis=1) + cdb_ref[...]
    hb = jax.nn.silu(_layer_norm_rows(hb, clg_ref[...], clb_ref[...]))
    y_b = jnp.dot(hb.astype(BF16), cpw_ref[...], preferred_element_type=F32) + cpb_ref[...]

    u = p[:, 3 * gw:4 * gw]
    vln = _layer_norm_rows(p[:, 4 * gw:5 * gw], slg_ref[...], slb_ref[...]).astype(BF16)
    rowi = lax.broadcasted_iota(I32, (SG_CHUNK, SG_CHUNK), 0)
    coli = lax.broadcasted_iota(I32, (SG_CHUNK, SG_CHUNK), 1)
    tril = coli <= rowi
    wcat = jnp.concatenate(
        [jnp.where(tril, sgw_ref[hh], 0.0) for hh in range(SG_HEADS)], axis=1).astype(BF16)
    lane_g = lax.broadcasted_iota(I32, (SG_CHUNK, gw), 1) // (gw // SG_HEADS)
    zero_bf = jnp.zeros((SG_CHUNK, gw), BF16)
    yc_chunks = []
    for n in range(ts // SG_CHUNK):
        vch = vln[n * SG_CHUNK:(n + 1) * SG_CHUNK, :]
        vbd = jnp.concatenate([jnp.where(lane_g == hh, vch, zero_bf) for hh in range(SG_HEADS)], axis=0)
        mixed = jnp.dot(wcat, vbd, preferred_element_type=F32) + sgb_ref[...]
        yc_chunks.append(u[n * SG_CHUNK:(n + 1) * SG_CHUNK, :] * mixed)
    y_c = jnp.concatenate(yc_chunks, axis=0)

    cx = p[:, 6 * gw:7 * gw] * p[:, 7 * gw:8 * gw]
    _carry_history(ed_ref, cx, j, ts, SC_HALO)
    cd_cols = []
    for cb in range(gw // LANES):
        ls = slice(cb * LANES, (cb + 1) * LANES)
        cd_cols.append(cx[:, ls] * scw_ref[2:3, ls]
                       + ed_ref[cb, SC_HALO - 1:SC_HALO - 1 + ts, :] * scw_ref[1:2, ls]
                       + ed_ref[cb, SC_HALO - 2:SC_HALO - 2 + ts, :] * scw_ref[0:1, ls])
    y_d = p[:, 5 * gw:6 * gw] * jnp.concatenate(cd_cols, axis=1)

    normed = []
    for gi, yg in enumerate((y_a, y_b, y_c, y_d)):
        rg = lax.rsqrt(jnp.mean(yg * yg, axis=-1, keepdims=True) + EPS)
        normed.append((yg * rg * ong_ref[:, gi * gw:(gi + 1) * gw]).astype(BF16))
    yn = jnp.concatenate(normed, axis=1)
    x1 = x + g1 * jnp.dot(yn, wout_ref[...], preferred_element_type=F32)
    x1_ref[...] = x1

    r2 = lax.rsqrt(jnp.mean(x1 * x1, axis=-1, keepdims=True) + EPS)
    h2 = x1 * r2 * (n2g_ref[...] * (1.0 + sc2)) + sh2
    for c in range(REC):
        rec_ref[pl.ds(c, ts, stride=REC), :] = h2[:, c * LANES:(c + 1) * LANES]
    logits = lax.dot_general(rwt_ref[...], h2.astype(BF16), (((1,), (1,)), ((), ())),
                             preferred_element_type=F32)
    scores = jax.nn.sigmoid(logits)
    sel = scores + rb_ref[...]
    epg = EXPERTS_PER_GROUP
    sel_s = [sel[jj * N_EXPERT_GROUPS:(jj + 1) * N_EXPERT_GROUPS, :] for jj in range(epg)]
    top2 = None
    for ja in range(epg):
        for jb in range(ja + 1, epg):
            pair = sel_s[ja] + sel_s[jb]
            top2 = pair if top2 is None else jnp.maximum(top2, pair)
    gidx, _ = _first_argmax([top2[g:g + 1, :] for g in range(N_EXPERT_GROUPS)])
    sel_c = [_pick(gidx, [sel_s[jj][g:g + 1, :] for g in range(N_EXPERT_GROUPS)]) for jj in range(epg)]
    i1, _ = _first_argmax(sel_c)
    i2, _ = _first_argmax([jnp.where(i1 == float(jj), -jnp.inf, sel_c[jj]) for jj in range(epg)])
    lo = jnp.minimum(i1, i2)
    hi = jnp.maximum(i1, i2)
    base = jnp.where(lo == 0.0, 0.0, jnp.where(lo == 1.0, 3.0, 5.0))
    cls = gidx * float(PAIRS_PER_GROUP) + base + (hi - lo - 1.0)
    cls_ref[0] = cls.astype(I32)


def _mixer_layer(x, yprev, g2prev, mod, lw, ts, batch):
    t, d = x.shape
    nj = t // batch // ts
    has_prev = yprev is not None
    tile = lambda bi, j: (bi * nj + j, 0)
    const2 = lambda bi, j: (0, 0)
    const3 = lambda bi, j: (0, 0, 0)
    per_b = lambda bi, j: (bi, 0, 0)

    args = [x]
    in_specs = [pl.BlockSpec((ts, d), tile)]
    if has_prev:
        args += [yprev, g2prev]
        in_specs += [pl.BlockSpec((ts * REC, LANES), tile), pl.BlockSpec((1, 1, d), per_b)]
    args += [mod]
    in_specs += [pl.BlockSpec((1, N_MOD, d), per_b)]
    for name in ("n1g", "n2g", "w_in", "pool_w", "pool_scale", "conv_dw", "conv_db", "conv_ln_g",
                 "conv_ln_b", "conv_pw", "conv_pb", "sg_ln_g", "sg_ln_b", "sg_w", "sg_b", "sc_w",
                 "out_norm_g", "w_out", "router_wt", "router_b"):
        arr = lw[name]
        args.append(arr)
        in_specs.append(pl.BlockSpec(arr.shape, const3 if arr.ndim == 3 else const2))

    out_shape = (jax.ShapeDtypeStruct((t, d), F32),
                 jax.ShapeDtypeStruct((t * REC, LANES), F32),
                 jax.ShapeDtypeStruct((t // ts, 1, ts), I32))
    out_specs = (pl.BlockSpec((ts, d), tile),
                 pl.BlockSpec((ts * REC, LANES), tile),
                 pl.BlockSpec((1, 1, ts), lambda bi, j: (bi * nj + j, 0, 0)))
    nslab = GROUP_W // LANES
    scratch = [pltpu.VMEM((nslab, POOL_HALO + ts, LANES), F32),
               pltpu.VMEM((nslab, CONV_HALO + ts, LANES), F32),
               pltpu.VMEM((nslab, SC_HALO + ts, LANES), F32)]
    return pl.pallas_call(
        functools.partial(_mixer_body, has_prev, ts),
        grid=(batch, nj),
        in_specs=in_specs,
        out_specs=out_specs,
        out_shape=out_shape,
        scratch_shapes=scratch,
        compiler_params=pltpu.CompilerParams(
            dimension_semantics=("arbitrary", "arbitrary"), vmem_limit_bytes=VMEM_LIMIT),
        name="mixer_layer",
    )(*args)


def _plan_body(tm, cls_ref, pos_ref, meta_ref):
    cls = cls_ref[...]
    nrow = cls.shape[0]
    ii = lax.broadcasted_iota(I32, (LANES, LANES), 0)
    jj = lax.broadcasted_iota(I32, (LANES, LANES), 1)
    upper = jnp.where(ii <= jj, 1.0, 0.0).astype(BF16)
    ri = lax.broadcasted_iota(I32, (nrow, nrow), 0)
    rj = lax.broadcasted_iota(I32, (nrow, nrow), 1)
    lower = jnp.where(rj < ri, 1.0, 0.0).astype(BF16)
    lane_i = lax.broadcasted_iota(I32, (1, LANES), 1)
    tile_i = lane_i.astype(F32)
    start = jnp.zeros((1, 1), F32)
    cum_tiles = jnp.zeros((1, 1), F32)
    pos = jnp.zeros((nrow, LANES), F32)
    tclass = jnp.zeros((1, LANES), F32)
    seg_end = jnp.zeros((1, LANES), F32)
    for k in range(N_CLASSES):
        hit = cls == k
        oh = jnp.where(hit, 1.0, 0.0)
        ohb = oh.astype(BF16)
        incl = jnp.dot(ohb, upper, preferred_element_type=F32)
        above = jnp.sum(jnp.dot(lower, ohb, preferred_element_type=F32), axis=1, keepdims=True)
        rank = incl - oh + above
        pos = pos + jnp.where(hit, start + rank, 0.0)
        cnt = jnp.sum(jnp.sum(oh, axis=1, keepdims=True), axis=0, keepdims=True)
        n_tiles = jnp.floor((cnt + float(tm - 1)) * (1.0 / tm))
        start = start + n_tiles * float(tm)
        cum_tiles = cum_tiles + n_tiles
        tclass = tclass + jnp.where(tile_i >= cum_tiles, 1.0, 0.0)
        seg_end = jnp.where(lane_i == k, start, seg_end)
    pos_ref[...] = pos.astype(I32)

    valid = tclass < float(N_CLASSES)
    kc = jnp.minimum(tclass, float(N_CLASSES - 1))
    grp = jnp.zeros_like(kc)
    for g in range(1, N_EXPERT_GROUPS):
        grp = grp + jnp.where(kc >= float(g * PAIRS_PER_GROUP), 1.0, 0.0)
    pr = kc - grp * float(PAIRS_PER_GROUP)
    lo = jnp.where(pr >= 3.0, 1.0, 0.0) + jnp.where(pr >= 5.0, 1.0, 0.0)
    base = jnp.where(lo == 0.0, 0.0, jnp.where(lo == 1.0, 3.0, 5.0))
    hi = pr - base + lo + 1.0
    rows = {META_EA: grp * float(EXPERTS_PER_GROUP) + lo,
            META_EB: grp * float(EXPERTS_PER_GROUP) + hi,
            META_VALID: jnp.where(valid, 1.0, 0.0),
            META_BLK: jnp.minimum(tile_i, cum_tiles - 1.0),
            META_END: seg_end}
    rid = lax.broadcasted_iota(I32, (SUBLANES, LANES), 0)
    meta = jnp.zeros((SUBLANES, LANES), F32)
    for k, row in rows.items():
        meta = jnp.where(rid == k, row, meta)
    meta_ref[...] = meta.astype(I32)


def _route_plan(cls2d, tm):
    nrow = cls2d.shape[0]
    return pl.pallas_call(
        functools.partial(_plan_body, tm),
        in_specs=[pl.BlockSpec((nrow, LANES), lambda: (0, 0))],
        out_specs=(pl.BlockSpec((nrow, LANES), lambda: (0, 0)),
                   pl.BlockSpec((SUBLANES, LANES), lambda: (0, 0))),
        out_shape=(jax.ShapeDtypeStruct((nrow, LANES), I32),
                   jax.ShapeDtypeStruct((SUBLANES, LANES), I32)),
        name="route_plan",
    )(cls2d)


def _permute_body(dispatch, tm, nt, pos_ref, end_ref, valid_ref, src_hbm, dst_hbm, *scratch):
    sem = scratch[-1]
    g = pl.program_id(0)
    n = PERMUTE_ROWS
    par = g % 2

    if dispatch:
        zbuf, zsem = scratch[0], scratch[1]

        @pl.when(g == 0)
        def _():
            zbuf[...] = jnp.zeros_like(zbuf)

            def zero_tile(first_slot):
                cp = pltpu.make_async_copy(
                    zbuf, dst_hbm.at[pl.ds(pl.multiple_of(first_slot * REC, REC), tm * REC), :], zsem)
                cp.start()
                cp.wait()

            for k in range(N_CLASSES):
                prev_end = end_ref[k - 1] if k > 0 else 0

                @pl.when(end_ref[k] > prev_end)
                def _():
                    zero_tile(end_ref[k] - tm)

            for i in range(nt):
                @pl.when(valid_ref[i] == 0)
                def _():
                    zero_tile(i * tm)

    base = g * n
    for r0 in range(0, n, PERMUTE_GROUP):
        slots = [pos_ref[base + r0 + k] for k in range(PERMUTE_GROUP)]
        for k, slot in enumerate(slots):
            tok = base + r0 + k
            s_row, d_row = (tok, slot) if dispatch else (slot, tok)
            pltpu.make_async_copy(
                src_hbm.at[pl.ds(pl.multiple_of(s_row * REC, REC), REC), :],
                dst_hbm.at[pl.ds(pl.multiple_of(d_row * REC, REC), REC), :],
                sem.at[par]).start(priority=k % 2)

    def wait_step(which):
        pltpu.make_async_copy(src_hbm.at[pl.ds(0, n * REC), :], dst_hbm.at[pl.ds(0, n * REC), :],
                              sem.at[which]).wait()

    @pl.when(g > 0)
    def _():
        wait_step(1 - par)

    @pl.when(g == pl.num_programs(0) - 1)
    def _():
        wait_step(par)


def _row_permute(src, pos, meta, n_dst_rows, dispatch, tm, nt):
    n_tok = pos.shape[0]
    scratch = [pltpu.SemaphoreType.DMA((2,))]
    if dispatch:
        scratch = [pltpu.VMEM((tm * REC, LANES), src.dtype), pltpu.SemaphoreType.DMA(())] + scratch
    grid_spec = pltpu.PrefetchScalarGridSpec(
        num_scalar_prefetch=3,
        grid=(n_tok // PERMUTE_ROWS,),
        in_specs=[pl.BlockSpec(memory_space=pl.ANY)],
        out_specs=pl.BlockSpec(memory_space=pl.ANY),
        scratch_shapes=scratch,
    )
    return pl.pallas_call(
        functools.partial(_permute_body, dispatch, tm, nt),
        grid_spec=grid_spec,
        out_shape=jax.ShapeDtypeStruct((n_dst_rows * REC, LANES), src.dtype),
        compiler_params=pltpu.CompilerParams(
            dimension_semantics=("arbitrary",), vmem_limit_bytes=VMEM_LIMIT),
        name="row_dispatch" if dispatch else "row_combine",
    )(pos, meta[META_END, :N_CLASSES], meta[META_VALID, :nt], src)


def _moe_body(tm, ea_ref, eb_ref, valid_ref, blk_ref, h_ref, rwa_ref, rwb_ref,
              wgua_ref, wgub_ref, wda_ref, wdb_ref, y_ref):
    i = pl.program_id(0)

    @pl.when(valid_ref[i] == 1)
    def _():
        xb = jnp.concatenate([_rows_of(h_ref, c, tm) for c in range(REC)], axis=1).astype(BF16)
        xf = xb.astype(F32)
        sa = jax.nn.sigmoid(jnp.sum(xf * rwa_ref[0].astype(F32), axis=-1, keepdims=True))
        sb = jax.nn.sigmoid(jnp.sum(xf * rwb_ref[0].astype(F32), axis=-1, keepdims=True))
        wa = sa / (sa + sb)
        wb = sb / (sa + sb)
        ha = jnp.dot(xb, wgua_ref[0], preferred_element_type=F32)
        hb = jnp.dot(xb, wgub_ref[0], preferred_element_type=F32)
        acta = jax.nn.silu(ha[:, :D_EXPERT]) * ha[:, D_EXPERT:] * wa
        actb = jax.nn.silu(hb[:, :D_EXPERT]) * hb[:, D_EXPERT:] * wb
        y = (jnp.dot(acta.astype(BF16), wda_ref[0], preferred_element_type=F32)
             + jnp.dot(actb.astype(BF16), wdb_ref[0], preferred_element_type=F32))
        for c in range(REC):
            y_ref[pl.ds(c, tm, stride=REC), :] = y[:, c * LANES:(c + 1) * LANES]

    @pl.when(valid_ref[i] == 0)
    def _():
        y_ref[...] = jnp.zeros_like(y_ref)


def _moe_tiles(hs, meta, router_rows, wgu, wd, tm, nt):
    d = D_MODEL
    wmap_a = lambda i, ea, eb, valid, blk: (ea[i], 0, 0)
    wmap_b = lambda i, ea, eb, valid, blk: (eb[i], 0, 0)
    grid_spec = pltpu.PrefetchScalarGridSpec(
        num_scalar_prefetch=4,
        grid=(nt,),
        in_specs=[
            pl.BlockSpec((tm * REC, LANES), lambda i, ea, eb, valid, blk: (blk[i], 0)),
            pl.BlockSpec((1, 1, d), wmap_a),
            pl.BlockSpec((1, 1, d), wmap_b),
            pl.BlockSpec((1, d, 2 * D_EXPERT), wmap_a),
            pl.BlockSpec((1, d, 2 * D_EXPERT), wmap_b),
            pl.BlockSpec((1, D_EXPERT, d), wmap_a),
            pl.BlockSpec((1, D_EXPERT, d), wmap_b),
        ],
        out_specs=pl.BlockSpec((tm * REC, LANES), lambda i, ea, eb, valid, blk: (i, 0)),
    )
    return pl.pallas_call(
        functools.partial(_moe_body, tm),
        grid_spec=grid_spec,
        out_shape=jax.ShapeDtypeStruct((nt * tm * REC, LANES), F32),
        compiler_params=pltpu.CompilerParams(
            dimension_semantics=("arbitrary",), vmem_limit_bytes=VMEM_LIMIT),
        name="moe_tiles",
    )(meta[META_EA, :nt], meta[META_EB, :nt], meta[META_VALID, :nt], meta[META_BLK, :nt],
      hs, router_rows, router_rows, wgu, wgu, wd, wd)


def _final_body(ts, x_ref, y_ref, g2_ref, g_ref, o_ref):
    y = jnp.concatenate([_rows_of(y_ref, c, ts) for c in range(REC)], axis=1)
    x = x_ref[...] + g2_ref[0] * y
    r = lax.rsqrt(jnp.mean(x * x, axis=-1, keepdims=True) + EPS)
    o_ref[...] = x * r * g_ref[...]


def _final_norm(x, y, g2, final_g, ts, batch):
    t, d = x.shape
    nj = t // batch // ts
    tile = lambda bi, j: (bi * nj + j, 0)
    return pl.pallas_call(
        functools.partial(_final_body, ts),
        grid=(batch, nj),
        in_specs=[pl.BlockSpec((ts, d), tile), pl.BlockSpec((ts * REC, LANES), tile),
                  pl.BlockSpec((1, 1, d), lambda bi, j: (bi, 0, 0)),
                  pl.BlockSpec((1, d), lambda bi, j: (0, 0))],
        out_specs=pl.BlockSpec((ts, d), tile),
        out_shape=jax.ShapeDtypeStruct((t, d), F32),
        compiler_params=pltpu.CompilerParams(
            dimension_semantics=("arbitrary", "arbitrary"), vmem_limit_bytes=VMEM_LIMIT),
        name="final_norm",
    )(x, y, g2, final_g.reshape(1, d))


def _block_diag(blocks):
    g, n, m = blocks.shape
    out = jnp.zeros((g * n, g * m), blocks.dtype)
    for i in range(g):
        out = out.at[i * n:(i + 1) * n, i * m:(i + 1) * m].set(blocks[i])
    return out


def _layer_weights(l, norm1_g, norm2_g, w_in, pool_w, pool_scale, conv_dw, conv_db, conv_ln_g,
                   conv_ln_b, conv_pw, conv_pb, sg_ln_g, sg_ln_b, sg_w, sg_b, sc_w, out_norm_g,
                   w_out, router_wt, router_b):
    row = lambda v: v.reshape(1, -1)
    return {
        "n1g": row(norm1_g[l]), "n2g": row(norm2_g[l]),
        "w_in": w_in[l].astype(BF16),
        "pool_w": _block_diag(pool_w[l]).astype(BF16),
        "pool_scale": row(pool_scale[l]),
        "conv_dw": jnp.pad(conv_dw[l], ((0, 1), (0, 0))),
        "conv_db": row(conv_db[l]), "conv_ln_g": row(conv_ln_g[l]), "conv_ln_b": row(conv_ln_b[l]),
        "conv_pw": conv_pw[l].astype(BF16), "conv_pb": row(conv_pb[l]),
        "sg_ln_g": row(sg_ln_g[l]), "sg_ln_b": row(sg_ln_b[l]),
        "sg_w": sg_w[l],
        "sg_b": jnp.repeat(sg_b[l].T, GROUP_W // SG_HEADS, axis=1),
        "sc_w": jnp.pad(sc_w[l], ((0, SUBLANES - SC_WIDTH), (0, 0))),
        "out_norm_g": row(out_norm_g[l]),
        "w_out": w_out[l].astype(BF16),
        "router_wt": router_wt, "router_b": router_b,
    }


def kernel(x, c, norm1_g, norm2_g, w_ada, b_ada, w_in, pool_w, pool_scale, conv_dw, conv_db, conv_ln_g, conv_ln_b, conv_pw, conv_pb, sg_ln_g, sg_ln_b, sg_w, sg_b, sc_w, out_norm_g, w_out, router_w, router_bias, exp_w_gate, exp_w_up, exp_w_down, final_g):
    b, s, d = x.shape
    depth = w_in.shape[0]
    t = b * s
    ts = 512
    tm = 256
    nt = t // tm + N_CLASSES
    assert d == D_MODEL and t % LANES == 0 and s % ts == 0 and t % tm == 0 and nt <= LANES
    assert t % PERMUTE_ROWS == 0

    mods = _ada_mod(c, w_ada, b_ada)

    perm = lambda v: v.reshape(N_EXPERT_GROUPS, EXPERTS_PER_GROUP, -1).transpose(1, 0, 2).reshape(
        N_EXPERTS, -1)
    router_wt = perm(router_w.T).astype(BF16)
    router_b = perm(router_bias.reshape(N_EXPERTS, 1))
    router_rows = router_w.T.astype(BF16).reshape(N_EXPERTS, 1, d)

    xcur, yprev, g2prev = x.reshape(t, d), None, None
    for l in range(depth):
        lw = _layer_weights(l, norm1_g, norm2_g, w_in, pool_w, pool_scale, conv_dw, conv_db,
                            conv_ln_g, conv_ln_b, conv_pw, conv_pb, sg_ln_g, sg_ln_b, sg_w, sg_b,
                            sc_w, out_norm_g, w_out, router_wt, router_b)
        x1, rec, cls = _mixer_layer(xcur, yprev, g2prev, mods[l], lw, ts, b)
        pos2d, meta = _route_plan(cls.reshape(t // LANES, LANES), tm)
        pos = pos2d.reshape(t)
        hs = _row_permute(rec, pos, meta, nt * tm, True, tm, nt)
        wgu = jnp.concatenate([exp_w_gate[l], exp_w_up[l]], axis=-1).astype(BF16)
        wd = exp_w_down[l].astype(BF16)
        ys = _moe_tiles(hs, meta, router_rows, wgu, wd, tm, nt)
        y = _row_permute(ys, pos, meta, t, False, tm, nt)
        xcur, yprev, g2prev = x1, y, mods[l][:, 5:6, :]
    return _final_norm(xcur, yprev, g2prev, final_g, ts, b).reshape(b, s, d)
```

```python
import functools

import jax
import jax.numpy as jnp
from jax import lax
from jax.experimental import pallas as pl
from jax.experimental.pallas import tpu as pltpu

D_MODEL = 1024
GROUP_W = 256
POOL_WINDOWS = (2, 4, 8, 16)
POOL_CG = 64
CONV_WIDTH = 31
SG_CHUNK = 128
SG_HEADS = 4
SC_WIDTH = 3
N_EXPERTS = 16
N_EXPERT_GROUPS = 4
EXPERTS_PER_GROUP = 4
PAIRS_PER_GROUP = 6
N_CLASSES = N_EXPERT_GROUPS * PAIRS_PER_GROUP
D_EXPERT = 256
N_MOD = 6
EPS = 1e-6
LN_EPS = 1e-5

LANES = 128
SUBLANES = 8
REC = SUBLANES
CONV_HALO = 32
POOL_HALO = 16
SC_HALO = 8
CONV_ROWS = 32
META_EA, META_EB, META_VALID, META_BLK, META_END = 0, 1, 2, 3, 4
PERMUTE_ROWS = 512
PERMUTE_GROUP = 16
VMEM_LIMIT = 56 * 1024 * 1024

F32 = jnp.float32
BF16 = jnp.bfloat16
I32 = jnp.int32


def _rows_of(ref, c, n):
    return ref[pl.ds(c, n, stride=REC), :]


def _ada_body(c_ref, w_ref, b_ref, o_ref):
    ca = jax.nn.silu(c_ref[...])
    o_ref[0] = jnp.dot(ca, w_ref[0], preferred_element_type=F32,
                       precision=lax.Precision.HIGHEST) + b_ref[0]


def _ada_mod(c, w_ada, b_ada):
    depth, d, n = w_ada.shape
    b = c.shape[0]
    tn = 1024
    cp = jnp.pad(c, ((0, SUBLANES - b), (0, 0)))
    out = pl.pallas_call(
        _ada_body,
        grid=(depth, n // tn),
        in_specs=[
            pl.BlockSpec((SUBLANES, d), lambda l, j: (0, 0)),
            pl.BlockSpec((1, d, tn), lambda l, j: (l, 0, j)),
            pl.BlockSpec((1, 1, tn), lambda l, j: (l, 0, j)),
        ],
        out_specs=pl.BlockSpec((1, SUBLANES, tn), lambda l, j: (l, 0, j)),
        out_shape=jax.ShapeDtypeStruct((depth, SUBLANES, n), F32),
        compiler_params=pltpu.CompilerParams(
            dimension_semantics=("arbitrary", "arbitrary"), vmem_limit_bytes=VMEM_LIMIT),
        name="ada_mod",
    )(cp, w_ada, b_ada.reshape(depth, 1, n))
    return out[:, :b].reshape(depth, b, N_MOD, d)


def _layer_norm_rows(v, g, b):
    mu = jnp.mean(v, axis=-1, keepdims=True)
    vc = v - mu
    var = jnp.mean(vc * vc, axis=-1, keepdims=True)
    return vc * lax.rsqrt(var + LN_EPS) * g + b


def _carry_history(ext_ref, cur, j, ts, halo):
    nslab = ext_ref.shape[0]

    @pl.when(j == 0)
    def _():
        ext_ref[:, 0:halo, :] = jnp.zeros((nslab, halo, LANES), F32)

    @pl.when(j > 0)
    def _():
        ext_ref[:, 0:halo, :] = ext_ref[:, ts:ts + halo, :]

    for cb in range(nslab):
        ext_ref[cb, halo:halo + ts, :] = cur[:, cb * LANES:(cb + 1) * LANES]


def _first_argmax(vals):
    best = vals[0]
    idx = jnp.zeros_like(best)
    for k in range(1, len(vals)):
        better = vals[k] > best
        idx = jnp.where(better, float(k), idx)
        best = jnp.where(better, vals[k], best)
    return idx, best


def _pick(idx, vals):
    out = vals[0]
    for k in range(1, len(vals)):
        out = jnp.where(idx == float(k), vals[k], out)
    return out


def _mixer_body(has_prev, ts, *refs):
    refs = list(refs)
    x_ref = refs.pop(0)
    if has_prev:
        yprev_ref = refs.pop(0)
        g2p_ref = refs.pop(0)
    (mod_ref, n1g_ref, n2g_ref, win_ref, poolw_ref, pools_ref,
     cdw_ref, cdb_ref, clg_ref, clb_ref, cpw_ref, cpb_ref,
     slg_ref, slb_ref, sgw_ref, sgb_ref, scw_ref, ong_ref, wout_ref, rwt_ref, rb_ref,
     x1_ref, rec_ref, cls_ref,
     ea_ref, eg_ref, ed_ref) = refs

    j = pl.program_id(1)
    gw = GROUP_W

    x = x_ref[...]
    if has_prev:
        yprev = jnp.concatenate([_rows_of(yprev_ref, c, ts) for c in range(REC)], axis=1)
        x = x + g2p_ref[0] * yprev

    sh1 = mod_ref[0, 0:1, :]
    sc1 = mod_ref[0, 1:2, :]
    g1 = mod_ref[0, 2:3, :]
    sh2 = mod_ref[0, 3:4, :]
    sc2 = mod_ref[0, 4:5, :]

    r1 = lax.rsqrt(jnp.mean(x * x, axis=-1, keepdims=True) + EPS)
    h = x * r1 * (n1g_ref[...] * (1.0 + sc1)) + sh1
    p = jnp.dot(h.astype(BF16), win_ref[...], preferred_element_type=F32)

    a = p[:, 0:gw]
    _carry_history(ea_ref, a, j, ts, POOL_HALO)

    def a_shift(s, cb):
        return ea_ref[cb, POOL_HALO - s:POOL_HALO - s + ts, :]

    lane = lax.broadcasted_iota(I32, (ts, LANES), 1)
    pos1 = (lax.broadcasted_iota(I32, (ts, LANES), 0) + (j * ts + 1)).astype(F32)
    first = lane < POOL_CG
    a_lo = a[:, 0:LANES]
    s01 = a_lo + a_shift(1, 0)
    s03 = s01 + (a_shift(2, 0) + a_shift(3, 0))
    num_lo = jnp.where(first, s01, s03)
    den_lo = jnp.minimum(pos1, jnp.where(first, float(POOL_WINDOWS[0]), float(POOL_WINDOWS[1])))
    a_hi = a[:, LANES:gw]
    s07 = a_hi
    for s in range(1, 8):
        s07 = s07 + a_shift(s, 1)
    s815 = a_shift(8, 1)
    for s in range(9, 16):
        s815 = s815 + a_shift(s, 1)
    num_hi = jnp.where(first, s07, s07 + s815)
    den_hi = jnp.minimum(pos1, jnp.where(first, float(POOL_WINDOWS[2]), float(POOL_WINDOWS[3])))
    d_pool = jnp.concatenate([num_lo / den_lo - a_lo, num_hi / den_hi - a_hi], axis=1)
    y_a = jnp.dot(d_pool.astype(BF16), poolw_ref[...], preferred_element_type=F32) * pools_ref[...]

    glu = p[:, gw:2 * gw] * jax.nn.sigmoid(p[:, 2 * gw:3 * gw])
    _carry_history(eg_ref, glu, j, ts, CONV_HALO)
    conv_cols = []
    for cb in range(gw // LANES):
        conv_chunks = []
        for r0 in range(0, ts, CONV_ROWS):
            acc = None
            for k in range(CONV_WIDTH):
                off = CONV_HALO - (CONV_WIDTH - 1) + k + r0
                term = eg_ref[cb, off:off + CONV_ROWS, :] * cdw_ref[k:k + 1, cb * LANES:(cb + 1) * LANES]
                acc = term if acc is None else acc + term
            conv_chunks.append(acc)
        conv_cols.append(jnp.concatenate(conv_chunks, axis=0))
    hb = jnp.concatenate(conv_cols, axis=1) + cdb_ref[...]
    hb = jax.nn.silu(_layer_norm_rows(hb, clg_ref[...], clb_ref[...]))
    y_b = jnp.dot(hb.astype(BF16), cpw_ref[...], preferred_element_type=F32) + cpb_ref[...]

    u = p[:, 3 * gw:4 * gw]
    vln = _layer_norm_rows(p[:, 4 * gw:5 * gw], slg_ref[...], slb_ref[...]).astype(BF16)
    rowi = lax.broadcasted_iota(I32, (SG_CHUNK, SG_CHUNK), 0)
    coli = lax.broadcasted_iota(I32, (SG_CHUNK, SG_CHUNK), 1)
    tril = coli <= rowi
    wcat = jnp.concatenate(
        [jnp.where(tril, sgw_ref[hh], 0.0) for hh in range(SG_HEADS)], axis=1).astype(BF16)
    lane_g = lax.broadcasted_iota(I32, (SG_CHUNK, gw), 1) // (gw // SG_HEADS)
    zero_bf = jnp.zeros((SG_CHUNK, gw), BF16)
    yc_chunks = []
    for n in range(ts // SG_CHUNK):
        vch = vln[n * SG_CHUNK:(n + 1) * SG_CHUNK, :]
        vbd = jnp.concatenate([jnp.where(lane_g == hh, vch, zero_bf) for hh in range(SG_HEADS)], axis=0)
        mixed = jnp.dot(wcat, vbd, preferred_element_type=F32) + sgb_ref[...]
        yc_chunks.append(u[n * SG_CHUNK:(n + 1) * SG_CHUNK, :] * mixed)
    y_c = jnp.concatenate(yc_chunks, axis=0)

    cx = p[:, 6 * gw:7 * gw] * p[:, 7 * gw:8 * gw]
    _carry_history(ed_ref, cx, j, ts, SC_HALO)
    cd_cols = []
    for cb in range(gw // LANES):
        ls = slice(cb * LANES, (cb + 1) * LANES)
        cd_cols.append(cx[:, ls] * scw_ref[2:3, ls]
                       + ed_ref[cb, SC_HALO - 1:SC_HALO - 1 + ts, :] * scw_ref[1:2, ls]
                       + ed_ref[cb, SC_HALO - 2:SC_HALO - 2 + ts, :] * scw_ref[0:1, ls])
    y_d = p[:, 5 * gw:6 * gw] * jnp.concatenate(cd_cols, axis=1)

    normed = []
    for gi, yg in enumerate((y_a, y_b, y_c, y_d)):
        rg = lax.rsqrt(jnp.mean(yg * yg, axis=-1, keepdims=True) + EPS)
        normed.append((yg * rg * ong_ref[:, gi * gw:(gi + 1) * gw]).astype(BF16))
    yn = jnp.concatenate(normed, axis=1)
    x1 = x + g1 * jnp.dot(yn, wout_ref[...], preferred_element_type=F32)
    x1_ref[...] = x1

    r2 = lax.rsqrt(jnp.mean(x1 * x1, axis=-1, keepdims=True) + EPS)
    h2 = x1 * r2 * (n2g_ref[...] * (1.0 + sc2)) + sh2
    for c in range(REC):
        rec_ref[pl.ds(c, ts, stride=REC), :] = h2[:, c * LANES:(c + 1) * LANES]
    logits = lax.dot_general(rwt_ref[...], h2.astype(BF16), (((1,), (1,)), ((), ())),
                             preferred_element_type=F32)
    scores = jax.nn.sigmoid(logits)
    sel = scores + rb_ref[...]
    epg = EXPERTS_PER_GROUP
    sel_s = [sel[jj * N_EXPERT_GROUPS:(jj + 1) * N_EXPERT_GROUPS, :] for jj in range(epg)]
    top2 = None
    for ja in range(epg):
        for jb in range(ja + 1, epg):
            pair = sel_s[ja] + sel_s[jb]
            top2 = pair if top2 is None else jnp.maximum(top2, pair)
    gidx, _ = _first_argmax([top2[g:g + 1, :] for g in range(N_EXPERT_GROUPS)])
    sel_c = [_pick(gidx, [sel_s[jj][g:g + 1, :] for g in range(N_EXPERT_GROUPS)]) for jj in range(epg)]
    i1, _ = _first_argmax(sel_c)
    i2, _ = _first_argmax([jnp.where(i1 == float(jj), -jnp.inf, sel_c[jj]) for jj in range(epg)])
    lo = jnp.minimum(i1, i2)
    hi = jnp.maximum(i1, i2)
    base = jnp.where(lo == 0.0, 0.0, jnp.where(lo == 1.0, 3.0, 5.0))
    cls = gidx * float(PAIRS_PER_GROUP) + base + (hi - lo - 1.0)
    cls_ref[0] = cls.astype(I32)


def _mixer_layer(x, yprev, g2prev, mod, lw, ts, batch):
    t, d = x.shape
    nj = t // batch // ts
    has_prev = yprev is not None
    tile = lambda bi, j: (bi * nj + j, 0)
    const2 = lambda bi, j: (0, 0)
    const3 = lambda bi, j: (0, 0, 0)
    per_b = lambda bi, j: (bi, 0, 0)

    args = [x]
    in_specs = [pl.BlockSpec((ts, d), tile)]
    if has_prev:
        args += [yprev, g2prev]
        in_specs += [pl.BlockSpec((ts * REC, LANES), tile), pl.BlockSpec((1, 1, d), per_b)]
    args += [mod]
    in_specs += [pl.BlockSpec((1, N_MOD, d), per_b)]
    for name in ("n1g", "n2g", "w_in", "pool_w", "pool_scale", "conv_dw", "conv_db", "conv_ln_g",
                 "conv_ln_b", "conv_pw", "conv_pb", "sg_ln_g", "sg_ln_b", "sg_w", "sg_b", "sc_w",
                 "out_norm_g", "w_out", "router_wt", "router_b"):
        arr = lw[name]
        args.append(arr)
        in_specs.append(pl.BlockSpec(arr.shape, const3 if arr.ndim == 3 else const2))

    out_shape = (jax.ShapeDtypeStruct((t, d), F32),
                 jax.ShapeDtypeStruct((t * REC, LANES), F32),
                 jax.ShapeDtypeStruct((t // ts, 1, ts), I32))
    out_specs = (pl.BlockSpec((ts, d), tile),
                 pl.BlockSpec((ts * REC, LANES), tile),
                 pl.BlockSpec((1, 1, ts), lambda bi, j: (bi * nj + j, 0, 0)))
    nslab = GROUP_W // LANES
    scratch = [pltpu.VMEM((nslab, POOL_HALO + ts, LANES), F32),
               pltpu.VMEM((nslab, CONV_HALO + ts, LANES), F32),
               pltpu.VMEM((nslab, SC_HALO + ts, LANES), F32)]
    return pl.pallas_call(
        functools.partial(_mixer_body, has_prev, ts),
        grid=(batch, nj),
        in_specs=in_specs,
        out_specs=out_specs,
        out_shape=out_shape,
        scratch_shapes=scratch,
        compiler_params=pltpu.CompilerParams(
            dimension_semantics=("arbitrary", "arbitrary"), vmem_limit_bytes=VMEM_LIMIT),
        name="mixer_layer",
    )(*args)


def _plan_body(tm, cls_ref, pos_ref, meta_ref):
    cls = cls_ref[...]
    nrow = cls.shape[0]
    ii = lax.broadcasted_iota(I32, (LANES, LANES), 0)
    jj = lax.broadcasted_iota(I32, (LANES, LANES), 1)
    upper = jnp.where(ii <= jj, 1.0, 0.0).astype(BF16)
    ri = lax.broadcasted_iota(I32, (nrow, nrow), 0)
    rj = lax.broadcasted_iota(I32, (nrow, nrow), 1)
    lower = jnp.where(rj < ri, 1.0, 0.0).astype(BF16)
    lane_i = lax.broadcasted_iota(I32, (1, LANES), 1)
    tile_i = lane_i.astype(F32)
    start = jnp.zeros((1, 1), F32)
    cum_tiles = jnp.zeros((1, 1), F32)
    pos = jnp.zeros((nrow, LANES), F32)
    tclass = jnp.zeros((1, LANES), F32)
    seg_end = jnp.zeros((1, LANES), F32)
    for k in range(N_CLASSES):
        hit = cls == k
        oh = jnp.where(hit, 1.0, 0.0)
        ohb = oh.astype(BF16)
        incl = jnp.dot(ohb, upper, preferred_element_type=F32)
        above = jnp.sum(jnp.dot(lower, ohb, preferred_element_type=F32), axis=1, keepdims=True)
        rank = incl - oh + above
        pos = pos + jnp.where(hit, start + rank, 0.0)
        cnt = jnp.sum(jnp.sum(oh, axis=1, keepdims=True), axis=0, keepdims=True)
        n_tiles = jnp.floor((cnt + float(tm - 1)) * (1.0 / tm))
        start = start + n_tiles * float(tm)
        cum_tiles = cum_tiles + n_tiles
        tclass = tclass + jnp.where(tile_i >= cum_tiles, 1.0, 0.0)
        seg_end = jnp.where(lane_i == k, start, seg_end)
    pos_ref[...] = pos.astype(I32)

    valid = tclass < float(N_CLASSES)
    kc = jnp.minimum(tclass, float(N_CLASSES - 1))
    grp = jnp.zeros_like(kc)
    for g in range(1, N_EXPERT_GROUPS):
        grp = grp + jnp.where(kc >= float(g * PAIRS_PER_GROUP), 1.0, 0.0)
    pr = kc - grp * float(PAIRS_PER_GROUP)
    lo = jnp.where(pr >= 3.0, 1.0, 0.0) + jnp.where(pr >= 5.0, 1.0, 0.0)
    base = jnp.where(lo == 0.0, 0.0, jnp.where(lo == 1.0, 3.0, 5.0))
    hi = pr - base + lo + 1.0
    rows = {META_EA: grp * float(EXPERTS_PER_GROUP) + lo,
            META_EB: grp * float(EXPERTS_PER_GROUP) + hi,
            META_VALID: jnp.where(valid, 1.0, 0.0),
            META_BLK: jnp.minimum(tile_i, cum_tiles - 1.0),
            META_END: seg_end}
    rid = lax.broadcasted_iota(I32, (SUBLANES, LANES), 0)
    meta = jnp.zeros((SUBLANES, LANES), F32)
    for k, row in rows.items():
        meta = jnp.where(rid == k, row, meta)
    meta_ref[...] = meta.astype(I32)


def _route_plan(cls2d, tm):
    nrow = cls2d.shape[0]
    return pl.pallas_call(
        functools.partial(_plan_body, tm),
        in_specs=[pl.BlockSpec((nrow, LANES), lambda: (0, 0))],
        out_specs=(pl.BlockSpec((nrow, LANES), lambda: (0, 0)),
                   pl.BlockSpec((SUBLANES, LANES), lambda: (0, 0))),
        out_shape=(jax.ShapeDtypeStruct((nrow, LANES), I32),
                   jax.ShapeDtypeStruct((SUBLANES, LANES), I32)),
        name="route_plan",
    )(cls2d)


def _permute_body(dispatch, tm, nt, pos_ref, end_ref, valid_ref, src_ref, dst_ref, *scratch):
    sem = scratch[-1]
    g = pl.program_id(0)
    n = PERMUTE_ROWS

    if dispatch:
        zbuf, zsem = scratch[0], scratch[1]

        @pl.when(g == 0)
        def _():
            zbuf[...] = jnp.zeros_like(zbuf)

            def zero_tile(first_slot):
                cp = pltpu.make_async_copy(
                    zbuf, dst_ref.at[pl.ds(pl.multiple_of(first_slot * REC, REC), tm * REC), :], zsem)
                cp.start()
                cp.wait()

            for k in range(N_CLASSES):
                prev_end = end_ref[k - 1] if k > 0 else 0

                @pl.when(end_ref[k] > prev_end)
                def _():
                    zero_tile(end_ref[k] - tm)

            for i in range(nt):
                @pl.when(valid_ref[i] == 0)
                def _():
                    zero_tile(i * tm)

    base = g * n
    for r0 in range(0, n, PERMUTE_GROUP):
        slots = [pos_ref[base + r0 + k] for k in range(PERMUTE_GROUP)]
        for k, slot in enumerate(slots):
            here = pl.ds((r0 + k) * REC, REC)
            there = pl.ds(pl.multiple_of(slot * REC, REC), REC)
            if dispatch:
                cp = pltpu.make_async_copy(src_ref.at[here, :], dst_ref.at[there, :], sem)
            else:
                cp = pltpu.make_async_copy(src_ref.at[there, :], dst_ref.at[here, :], sem)
            cp.start(priority=k % 2)

    if dispatch:
        pltpu.make_async_copy(src_ref, dst_ref.at[pl.ds(0, n * REC), :], sem).wait()
    else:
        pltpu.make_async_copy(src_ref.at[pl.ds(0, n * REC), :], dst_ref, sem).wait()


def _row_permute(src, pos, meta, n_dst_rows, dispatch, tm, nt):
    n_tok = pos.shape[0]
    scratch = [pltpu.SemaphoreType.DMA(())]
    if dispatch:
        scratch = [pltpu.VMEM((tm * REC, LANES), src.dtype), pltpu.SemaphoreType.DMA(())] + scratch
    block = pl.BlockSpec((PERMUTE_ROWS * REC, LANES), lambda g, pos, end, valid: (g, 0))
    whole = pl.BlockSpec(memory_space=pl.ANY)
    grid_spec = pltpu.PrefetchScalarGridSpec(
        num_scalar_prefetch=3,
        grid=(n_tok // PERMUTE_ROWS,),
        in_specs=[block if dispatch else whole],
        out_specs=whole if dispatch else block,
        scratch_shapes=scratch,
    )
    return pl.pallas_call(
        functools.partial(_permute_body, dispatch, tm, nt),
        grid_spec=grid_spec,
        out_shape=jax.ShapeDtypeStruct((n_dst_rows * REC, LANES), src.dtype),
        compiler_params=pltpu.CompilerParams(
            dimension_semantics=("arbitrary",), vmem_limit_bytes=VMEM_LIMIT),
        name="row_dispatch" if dispatch else "row_combine",
    )(pos, meta[META_END, :N_CLASSES], meta[META_VALID, :nt], src)


def _moe_body(tm, ea_ref, eb_ref, valid_ref, blk_ref, h_ref, rwa_ref, rwb_ref,
              wgua_ref, wgub_ref, wda_ref, wdb_ref, y_ref):
    i = pl.program_id(0)

    @pl.when(valid_ref[i] == 1)
    def _():
        xb = jnp.concatenate([_rows_of(h_ref, c, tm) for c in range(REC)], axis=1).astype(BF16)
        xf = xb.astype(F32)
        sa = jax.nn.sigmoid(jnp.sum(xf * rwa_ref[0].astype(F32), axis=-1, keepdims=True))
        sb = jax.nn.sigmoid(jnp.sum(xf * rwb_ref[0].astype(F32), axis=-1, keepdims=True))
        wa = sa / (sa + sb)
        wb = sb / (sa + sb)
        ha = jnp.dot(xb, wgua_ref[0], preferred_element_type=F32)
        hb = jnp.dot(xb, wgub_ref[0], preferred_element_type=F32)
        acta = jax.nn.silu(ha[:, :D_EXPERT]) * ha[:, D_EXPERT:] * wa
        actb = jax.nn.silu(hb[:, :D_EXPERT]) * hb[:, D_EXPERT:] * wb
        y = (jnp.dot(acta.astype(BF16), wda_ref[0], preferred_element_type=F32)
             + jnp.dot(actb.astype(BF16), wdb_ref[0], preferred_element_type=F32))
        for c in range(REC):
            y_ref[pl.ds(c, tm, stride=REC), :] = y[:, c * LANES:(c + 1) * LANES]

    @pl.when(valid_ref[i] == 0)
    def _():
        y_ref[...] = jnp.zeros_like(y_ref)


def _moe_tiles(hs, meta, router_rows, wgu, wd, tm, nt):
    d = D_MODEL
    wmap_a = lambda i, ea, eb, valid, blk: (ea[i], 0, 0)
    wmap_b = lambda i, ea, eb, valid, blk: (eb[i], 0, 0)
    grid_spec = pltpu.PrefetchScalarGridSpec(
        num_scalar_prefetch=4,
        grid=(nt,),
        in_specs=[
            pl.BlockSpec((tm * REC, LANES), lambda i, ea, eb, valid, blk: (blk[i], 0)),
            pl.BlockSpec((1, 1, d), wmap_a),
            pl.BlockSpec((1, 1, d), wmap_b),
            pl.BlockSpec((1, d, 2 * D_EXPERT), wmap_a),
            pl.BlockSpec((1, d, 2 * D_EXPERT), wmap_b),
            pl.BlockSpec((1, D_EXPERT, d), wmap_a),
            pl.BlockSpec((1, D_EXPERT, d), wmap_b),
        ],
        out_specs=pl.BlockSpec((tm * REC, LANES), lambda i, ea, eb, valid, blk: (i, 0)),
    )
    return pl.pallas_call(
        functools.partial(_moe_body, tm),
        grid_spec=grid_spec,
        out_shape=jax.ShapeDtypeStruct((nt * tm * REC, LANES), F32),
        compiler_params=pltpu.CompilerParams(
            dimension_semantics=("arbitrary",), vmem_limit_bytes=VMEM_LIMIT),
        name="moe_tiles",
    )(meta[META_EA, :nt], meta[META_EB, :nt], meta[META_VALID, :nt], meta[META_BLK, :nt],
      hs, router_rows, router_rows, wgu, wgu, wd, wd)


def _final_body(ts, x_ref, y_ref, g2_ref, g_ref, o_ref):
    y = jnp.concatenate([_rows_of(y_ref, c, ts) for c in range(REC)], axis=1)
    x = x_ref[...] + g2_ref[0] * y
    r = lax.rsqrt(jnp.mean(x * x, axis=-1, keepdims=True) + EPS)
    o_ref[...] = x * r * g_ref[...]


def _final_norm(x, y, g2, final_g, ts, batch):
    t, d = x.shape
    nj = t // batch // ts
    tile = lambda bi, j: (bi * nj + j, 0)
    return pl.pallas_call(
        functools.partial(_final_body, ts),
        grid=(batch, nj),
        in_specs=[pl.BlockSpec((ts, d), tile), pl.BlockSpec((ts * REC, LANES), tile),
                  pl.BlockSpec((1, 1, d), lambda bi, j: (bi, 0, 0)),
                  pl.BlockSpec((1, d), lambda bi, j: (0, 0))],
        out_specs=pl.BlockSpec((ts, d), tile),
        out_shape=jax.ShapeDtypeStruct((t, d), F32),
        compiler_params=pltpu.CompilerParams(
            dimension_semantics=("arbitrary", "arbitrary"), vmem_limit_bytes=VMEM_LIMIT),
        name="final_norm",
    )(x, y, g2, final_g.reshape(1, d))


def _block_diag(blocks):
    g, n, m = blocks.shape
    out = jnp.zeros((g * n, g * m), blocks.dtype)
    for i in range(g):
        out = out.at[i * n:(i + 1) * n, i * m:(i + 1) * m].set(blocks[i])
    return out


def _layer_weights(l, norm1_g, norm2_g, w_in, pool_w, pool_scale, conv_dw, conv_db, conv_ln_g,
                   conv_ln_b, conv_pw, conv_pb, sg_ln_g, sg_ln_b, sg_w, sg_b, sc_w, out_norm_g,
                   w_out, router_wt, router_b):
    row = lambda v: v.reshape(1, -1)
    return {
        "n1g": row(norm1_g[l]), "n2g": row(norm2_g[l]),
        "w_in": w_in[l].astype(BF16),
        "pool_w": _block_diag(pool_w[l]).astype(BF16),
        "pool_scale": row(pool_scale[l]),
        "conv_dw": jnp.pad(conv_dw[l], ((0, 1), (0, 0))),
        "conv_db": row(conv_db[l]), "conv_ln_g": row(conv_ln_g[l]), "conv_ln_b": row(conv_ln_b[l]),
        "conv_pw": conv_pw[l].astype(BF16), "conv_pb": row(conv_pb[l]),
        "sg_ln_g": row(sg_ln_g[l]), "sg_ln_b": row(sg_ln_b[l]),
        "sg_w": sg_w[l],
        "sg_b": jnp.repeat(sg_b[l].T, GROUP_W // SG_HEADS, axis=1),
        "sc_w": jnp.pad(sc_w[l], ((0, SUBLANES - SC_WIDTH), (0, 0))),
        "out_norm_g": row(out_norm_g[l]),
        "w_out": w_out[l].astype(BF16),
        "router_wt": router_wt, "router_b": router_b,
    }


def kernel(x, c, norm1_g, norm2_g, w_ada, b_ada, w_in, pool_w, pool_scale, conv_dw, conv_db, conv_ln_g, conv_ln_b, conv_pw, conv_pb, sg_ln_g, sg_ln_b, sg_w, sg_b, sc_w, out_norm_g, w_out, router_w, router_bias, exp_w_gate, exp_w_up, exp_w_down, final_g):
    b, s, d = x.shape
    depth = w_in.shape[0]
    t = b * s
    ts = 512
    tm = 256
    nt = t // tm + N_CLASSES
    assert d == D_MODEL and t % LANES == 0 and s % ts == 0 and t % tm == 0 and nt <= LANES
    assert t % PERMUTE_ROWS == 0

    mods = _ada_mod(c, w_ada, b_ada)

    perm = lambda v: v.reshape(N_EXPERT_GROUPS, EXPERTS_PER_GROUP, -1).transpose(1, 0, 2).reshape(
        N_EXPERTS, -1)
    router_wt = perm(router_w.T).astype(BF16)
    router_b = perm(router_bias.reshape(N_EXPERTS, 1))
    router_rows = router_w.T.astype(BF16).reshape(N_EXPERTS, 1, d)

    xcur, yprev, g2prev = x.reshape(t, d), None, None
    for l in range(depth):
        lw = _layer_weights(l, norm1_g, norm2_g, w_in, pool_w, pool_scale, conv_dw, conv_db,
                            conv_ln_g, conv_ln_b, conv_pw, conv_pb, sg_ln_g, sg_ln_b, sg_w, sg_b,
                            sc_w, out_norm_g, w_out, router_wt, router_b)
        x1, rec, cls = _mixer_layer(xcur, yprev, g2prev, mods[l], lw, ts, b)
        pos2d, meta = _route_plan(cls.reshape(t // LANES, LANES), tm)
        pos = pos2d.reshape(t)
        hs = _row_permute(rec, pos, meta, nt * tm, True, tm, nt)
        wgu = jnp.concatenate([exp_w_gate[l], exp_w_up[l]], axis=-1).astype(BF16)
        wd = exp_w_down[l].astype(BF16)
        ys = _moe_tiles(hs, meta, router_rows, wgu, wd, tm, nt)
        y = _row_permute(ys, pos, meta, t, False, tm, nt)
        xcur, yprev, g2prev = x1, y, mods[l][:, 5:6, :]
    return _final_norm(xcur, yprev, g2prev, final_g, ts, b).reshape(b, s, d)
```

```python
import functools

import jax
import jax.numpy as jnp
from jax import lax
from jax.experimental import pallas as pl
from jax.experimental.pallas import tpu as pltpu

D_MODEL = 1024
GROUP_W = 256
POOL_WINDOWS = (2, 4, 8, 16)
POOL_CG = 64
CONV_WIDTH = 31
SG_CHUNK = 128
SG_HEADS = 4
SC_WIDTH = 3
N_EXPERTS = 16
N_EXPERT_GROUPS = 4
EXPERTS_PER_GROUP = 4
PAIRS_PER_GROUP = 6
N_CLASSES = N_EXPERT_GROUPS * PAIRS_PER_GROUP
D_EXPERT = 256
N_MOD = 6
EPS = 1e-6
LN_EPS = 1e-5

LANES = 128
SUBLANES = 8
REC = SUBLANES
CONV_HALO = 32
POOL_HALO = 16
SC_HALO = 8
CONV_ROWS = 32
META_EA, META_EB, META_VALID, META_BLK, META_END = 0, 1, 2, 3, 4
PERMUTE_ROWS = 512
PERMUTE_GROUP = 16
VMEM_LIMIT = 56 * 1024 * 1024

F32 = jnp.float32
BF16 = jnp.bfloat16
I32 = jnp.int32


def _rows_of(ref, c, n, first=0):
    return ref[pl.ds(first * REC + c, n, stride=REC), :]


def _ada_body(c_ref, w_ref, b_ref, o_ref):
    ca = jax.nn.silu(c_ref[...])
    o_ref[0] = jnp.dot(ca, w_ref[0], preferred_element_type=F32,
                       precision=lax.Precision.HIGHEST) + b_ref[0]


def _ada_mod(c, w_ada, b_ada):
    depth, d, n = w_ada.shape
    b = c.shape[0]
    tn = 1024
    cp = jnp.pad(c, ((0, SUBLANES - b), (0, 0)))
    out = pl.pallas_call(
        _ada_body,
        grid=(depth, n // tn),
        in_specs=[
            pl.BlockSpec((SUBLANES, d), lambda l, j: (0, 0)),
            pl.BlockSpec((1, d, tn), lambda l, j: (l, 0, j)),
            pl.BlockSpec((1, 1, tn), lambda l, j: (l, 0, j)),
        ],
        out_specs=pl.BlockSpec((1, SUBLANES, tn), lambda l, j: (l, 0, j)),
        out_shape=jax.ShapeDtypeStruct((depth, SUBLANES, n), F32),
        compiler_params=pltpu.CompilerParams(
            dimension_semantics=("arbitrary", "arbitrary"), vmem_limit_bytes=VMEM_LIMIT),
        name="ada_mod",
    )(cp, w_ada, b_ada.reshape(depth, 1, n))
    return out[:, :b].reshape(depth, b, N_MOD, d)


def _layer_norm_rows(v, g, b):
    mu = jnp.mean(v, axis=-1, keepdims=True)
    vc = v - mu
    var = jnp.mean(vc * vc, axis=-1, keepdims=True)
    return vc * lax.rsqrt(var + LN_EPS) * g + b


def _carry_tail(ext_ref, j, ts, halo):
    nslab = ext_ref.shape[0]

    @pl.when(j == 0)
    def _():
        ext_ref[:, 0:halo, :] = jnp.zeros((nslab, halo, LANES), F32)

    @pl.when(j > 0)
    def _():
        ext_ref[:, 0:halo, :] = ext_ref[:, ts:ts + halo, :]


def _append_rows(ext_ref, cur, first):
    for cb in range(ext_ref.shape[0]):
        ext_ref[cb, first:first + cur.shape[0], :] = cur[:, cb * LANES:(cb + 1) * LANES]


def _first_argmax(vals):
    best = vals[0]
    idx = jnp.zeros_like(best)
    for k in range(1, len(vals)):
        better = vals[k] > best
        idx = jnp.where(better, float(k), idx)
        best = jnp.where(better, vals[k], best)
    return idx, best


def _pick(idx, vals):
    out = vals[0]
    for k in range(1, len(vals)):
        out = jnp.where(idx == float(k), vals[k], out)
    return out


def _mixer_body(has_prev, ts, sb, *refs):
    refs = list(refs)
    x_ref = refs.pop(0)
    if has_prev:
        yprev_ref = refs.pop(0)
        modp_ref = refs.pop(0)
    (mod_ref, n1g_ref, n2g_ref, win_ref, poolw_ref, pools_ref,
     cdw_ref, cdb_ref, clg_ref, clb_ref, cpw_ref, cpb_ref,
     slg_ref, slb_ref, sgw_ref, sgb_ref, scw_ref, ong_ref, wout_ref, rwt_ref, rb_ref,
     x1_ref, rec_ref, cls_ref,
     ea_ref, eg_ref, ed_ref) = refs

    j = pl.program_id(1)
    gw = GROUP_W
    nslab = gw // LANES

    sh1 = mod_ref[0, 0, 0:1, :]
    sc1 = mod_ref[0, 0, 1:2, :]
    g1 = mod_ref[0, 0, 2:3, :]
    sh2 = mod_ref[0, 0, 3:4, :]
    sc2 = mod_ref[0, 0, 4:5, :]
    gain1 = n1g_ref[0] * (1.0 + sc1)
    gain2 = n2g_ref[0] * (1.0 + sc2)

    _carry_tail(ea_ref, j, ts, POOL_HALO)
    _carry_tail(eg_ref, j, ts, CONV_HALO)
    _carry_tail(ed_ref, j, ts, SC_HALO)

    rowi = lax.broadcasted_iota(I32, (SG_CHUNK, SG_CHUNK), 0)
    coli = lax.broadcasted_iota(I32, (SG_CHUNK, SG_CHUNK), 1)
    tril = coli <= rowi
    wcat = jnp.concatenate(
        [jnp.where(tril, sgw_ref[0, hh], 0.0) for hh in range(SG_HEADS)], axis=1).astype(BF16)
    lane_g = lax.broadcasted_iota(I32, (SG_CHUNK, gw), 1) // (gw // SG_HEADS)
    zero_bf = jnp.zeros((SG_CHUNK, gw), BF16)
    lane = lax.broadcasted_iota(I32, (sb, LANES), 1)
    first = lane < POOL_CG

    for r0 in range(0, ts, sb):
        x = x_ref[r0:r0 + sb, :]
        if has_prev:
            yprev = jnp.concatenate([_rows_of(yprev_ref, c, sb, r0) for c in range(REC)], axis=1)
            x = x + modp_ref[0, 0, 5:6, :] * yprev

        r1 = lax.rsqrt(jnp.mean(x * x, axis=-1, keepdims=True) + EPS)
        h = x * r1 * gain1 + sh1
        p = jnp.dot(h.astype(BF16), win_ref[0], preferred_element_type=F32)

        a = p[:, 0:gw]
        _append_rows(ea_ref, a, POOL_HALO + r0)

        def a_shift(s, cb):
            return ea_ref[cb, POOL_HALO + r0 - s:POOL_HALO + r0 - s + sb, :]

        pos1 = (lax.broadcasted_iota(I32, (sb, LANES), 0) + (j * ts + r0 + 1)).astype(F32)
        a_lo = a[:, 0:LANES]
        s01 = a_lo + a_shift(1, 0)
        s03 = s01 + (a_shift(2, 0) + a_shift(3, 0))
        num_lo = jnp.where(first, s01, s03)
        den_lo = jnp.minimum(pos1, jnp.where(first, float(POOL_WINDOWS[0]), float(POOL_WINDOWS[1])))
        a_hi = a[:, LANES:gw]
        s07 = a_hi
        for s in range(1, 8):
            s07 = s07 + a_shift(s, 1)
        s815 = a_shift(8, 1)
        for s in range(9, 16):
            s815 = s815 + a_shift(s, 1)
        num_hi = jnp.where(first, s07, s07 + s815)
        den_hi = jnp.minimum(pos1, jnp.where(first, float(POOL_WINDOWS[2]), float(POOL_WINDOWS[3])))
        d_pool = jnp.concatenate([num_lo / den_lo - a_lo, num_hi / den_hi - a_hi], axis=1)
        y_a = jnp.dot(d_pool.astype(BF16), poolw_ref[0], preferred_element_type=F32) * pools_ref[0]

        glu = p[:, gw:2 * gw] * jax.nn.sigmoid(p[:, 2 * gw:3 * gw])
        _append_rows(eg_ref, glu, CONV_HALO + r0)
        conv_cols = []
        for cb in range(nslab):
            conv_chunks = []
            for rr in range(r0, r0 + sb, CONV_ROWS):
                acc = None
                for k in range(CONV_WIDTH):
                    off = CONV_HALO - (CONV_WIDTH - 1) + k + rr
                    term = (eg_ref[cb, off:off + CONV_ROWS, :]
                            * cdw_ref[0, k:k + 1, cb * LANES:(cb + 1) * LANES])
                    acc = term if acc is None else acc + term
                conv_chunks.append(acc)
            conv_cols.append(jnp.concatenate(conv_chunks, axis=0))
        hb = jnp.concatenate(conv_cols, axis=1) + cdb_ref[0]
        hb = jax.nn.silu(_layer_norm_rows(hb, clg_ref[0], clb_ref[0]))
        y_b = jnp.dot(hb.astype(BF16), cpw_ref[0], preferred_element_type=F32) + cpb_ref[0]

        u = p[:, 3 * gw:4 * gw]
        vln = _layer_norm_rows(p[:, 4 * gw:5 * gw], slg_ref[0], slb_ref[0]).astype(BF16)
        yc_chunks = []
        for n in range(sb // SG_CHUNK):
            vch = vln[n * SG_CHUNK:(n + 1) * SG_CHUNK, :]
            vbd = jnp.concatenate([jnp.where(lane_g == hh, vch, zero_bf) for hh in range(SG_HEADS)],
                                  axis=0)
            mixed = jnp.dot(wcat, vbd, preferred_element_type=F32) + sgb_ref[0]
            yc_chunks.append(u[n * SG_CHUNK:(n + 1) * SG_CHUNK, :] * mixed)
        y_c = jnp.concatenate(yc_chunks, axis=0)

        cx = p[:, 6 * gw:7 * gw] * p[:, 7 * gw:8 * gw]
        _append_rows(ed_ref, cx, SC_HALO + r0)
        cd_cols = []
        for cb in range(nslab):
            ls = slice(cb * LANES, (cb + 1) * LANES)
            at = SC_HALO + r0
            cd_cols.append(cx[:, ls] * scw_ref[0, 2:3, ls]
                           + ed_ref[cb, at - 1:at - 1 + sb, :] * scw_ref[0, 1:2, ls]
                           + ed_ref[cb, at - 2:at - 2 + sb, :] * scw_ref[0, 0:1, ls])
        y_d = p[:, 5 * gw:6 * gw] * jnp.concatenate(cd_cols, axis=1)

        normed = []
        for gi, yg in enumerate((y_a, y_b, y_c, y_d)):
            rg = lax.rsqrt(jnp.mean(yg * yg, axis=-1, keepdims=True) + EPS)
            normed.append((yg * rg * ong_ref[0, :, gi * gw:(gi + 1) * gw]).astype(BF16))
        yn = jnp.concatenate(normed, axis=1)
        x1 = x + g1 * jnp.dot(yn, wout_ref[0], preferred_element_type=F32)
        x1_ref[r0:r0 + sb, :] = x1

        r2 = lax.rsqrt(jnp.mean(x1 * x1, axis=-1, keepdims=True) + EPS)
        h2 = x1 * r2 * gain2 + sh2
        for c in range(REC):
            rec_ref[pl.ds(r0 * REC + c, sb, stride=REC), :] = h2[:, c * LANES:(c + 1) * LANES]
        logits = lax.dot_general(rwt_ref[...], h2.astype(BF16), (((1,), (1,)), ((), ())),
                                 preferred_element_type=F32)
        sel = jax.nn.sigmoid(logits) + rb_ref[...]
        epg = EXPERTS_PER_GROUP
        sel_s = [sel[jj * N_EXPERT_GROUPS:(jj + 1) * N_EXPERT_GROUPS, :] for jj in range(epg)]
        top2 = None
        for ja in range(epg):
            for jb in range(ja + 1, epg):
                pair = sel_s[ja] + sel_s[jb]
                top2 = pair if top2 is None else jnp.maximum(top2, pair)
        gidx, _ = _first_argmax([top2[g:g + 1, :] for g in range(N_EXPERT_GROUPS)])
        sel_c = [_pick(gidx, [sel_s[jj][g:g + 1, :] for g in range(N_EXPERT_GROUPS)])
                 for jj in range(epg)]
        i1, _ = _first_argmax(sel_c)
        i2, _ = _first_argmax([jnp.where(i1 == float(jj), -jnp.inf, sel_c[jj]) for jj in range(epg)])
        lo = jnp.minimum(i1, i2)
        hi = jnp.maximum(i1, i2)
        base = jnp.where(lo == 0.0, 0.0, jnp.where(lo == 1.0, 3.0, 5.0))
        cls = gidx * float(PAIRS_PER_GROUP) + base + (hi - lo - 1.0)
        cls_ref[0, :, r0:r0 + sb] = cls.astype(I32)


def _mixer_layer(l, x, yprev, mods, lw, ts, sb, batch):
    t, d = x.shape
    nj = t // batch // ts
    has_prev = yprev is not None
    tile = lambda bi, j: (bi * nj + j, 0)

    args = [x]
    in_specs = [pl.BlockSpec((ts, d), tile)]
    if has_prev:
        args += [yprev, mods]
        in_specs += [pl.BlockSpec((ts * REC, LANES), tile),
                     pl.BlockSpec((1, 1, N_MOD, d), lambda bi, j: (l - 1, bi, 0, 0))]
    args += [mods]
    in_specs += [pl.BlockSpec((1, 1, N_MOD, d), lambda bi, j: (l, bi, 0, 0))]
    for name in ("n1g", "n2g", "w_in", "pool_w", "pool_scale", "conv_dw", "conv_db", "conv_ln_g",
                 "conv_ln_b", "conv_pw", "conv_pb", "sg_ln_g", "sg_ln_b", "sg_w", "sg_b", "sc_w",
                 "out_norm_g", "w_out"):
        arr = lw[name]
        args.append(arr)
        in_specs.append(pl.BlockSpec((1,) + arr.shape[1:],
                                     lambda bi, j, nd=arr.ndim: (l,) + (0,) * (nd - 1)))
    for name in ("router_wt", "router_b"):
        arr = lw[name]
        args.append(arr)
        in_specs.append(pl.BlockSpec(arr.shape, lambda bi, j: (0, 0)))

    out_shape = (jax.ShapeDtypeStruct((t, d), F32),
                 jax.ShapeDtypeStruct((t * REC, LANES), F32),
                 jax.ShapeDtypeStruct((t // ts, 1, ts), I32))
    out_specs = (pl.BlockSpec((ts, d), tile),
                 pl.BlockSpec((ts * REC, LANES), tile),
                 pl.BlockSpec((1, 1, ts), lambda bi, j: (bi * nj + j, 0, 0)))
    nslab = GROUP_W // LANES
    scratch = [pltpu.VMEM((nslab, POOL_HALO + ts, LANES), F32),
               pltpu.VMEM((nslab, CONV_HALO + ts, LANES), F32),
               pltpu.VMEM((nslab, SC_HALO + ts, LANES), F32)]
    return pl.pallas_call(
        functools.partial(_mixer_body, has_prev, ts, sb),
        grid=(batch, nj),
        in_specs=in_specs,
        out_specs=out_specs,
        out_shape=out_shape,
        scratch_shapes=scratch,
        compiler_params=pltpu.CompilerParams(
            dimension_semantics=("arbitrary", "arbitrary"), vmem_limit_bytes=VMEM_LIMIT),
        name="mixer_layer",
    )(*args)


def _plan_body(tm, cls_ref, pos_ref, meta_ref):
    cls = cls_ref[...]
    nrow = cls.shape[0]
    ii = lax.broadcasted_iota(I32, (LANES, LANES), 0)
    jj = lax.broadcasted_iota(I32, (LANES, LANES), 1)
    upper = jnp.where(ii <= jj, 1.0, 0.0).astype(BF16)
    ri = lax.broadcasted_iota(I32, (nrow, nrow), 0)
    rj = lax.broadcasted_iota(I32, (nrow, nrow), 1)
    lower = jnp.where(rj < ri, 1.0, 0.0).astype(BF16)
    lane_i = lax.broadcasted_iota(I32, (1, LANES), 1)
    tile_i = lane_i.astype(F32)
    start = jnp.zeros((1, 1), F32)
    cum_tiles = jnp.zeros((1, 1), F32)
    pos = jnp.zeros((nrow, LANES), F32)
    tclass = jnp.zeros((1, LANES), F32)
    seg_end = jnp.zeros((1, LANES), F32)
    for k in range(N_CLASSES):
        hit = cls == k
        oh = jnp.where(hit, 1.0, 0.0)
        ohb = oh.astype(BF16)
        incl = jnp.dot(ohb, upper, preferred_element_type=F32)
        above = jnp.sum(jnp.dot(lower, ohb, preferred_element_type=F32), axis=1, keepdims=True)
        rank = incl - oh + above
        pos = pos + jnp.where(hit, start + rank, 0.0)
        cnt = jnp.sum(jnp.sum(oh, axis=1, keepdims=True), axis=0, keepdims=True)
        n_tiles = jnp.floor((cnt + float(tm - 1)) * (1.0 / tm))
        start = start + n_tiles * float(tm)
        cum_tiles = cum_tiles + n_tiles
        tclass = tclass + jnp.where(tile_i >= cum_tiles, 1.0, 0.0)
        seg_end = jnp.where(lane_i == k, start, seg_end)
    pos_ref[...] = pos.astype(I32)

    valid = tclass < float(N_CLASSES)
    kc = jnp.minimum(tclass, float(N_CLASSES - 1))
    grp = jnp.zeros_like(kc)
    for g in range(1, N_EXPERT_GROUPS):
        grp = grp + jnp.where(kc >= float(g * PAIRS_PER_GROUP), 1.0, 0.0)
    pr = kc - grp * float(PAIRS_PER_GROUP)
    lo = jnp.where(pr >= 3.0, 1.0, 0.0) + jnp.where(pr >= 5.0, 1.0, 0.0)
    base = jnp.where(lo == 0.0, 0.0, jnp.where(lo == 1.0, 3.0, 5.0))
    hi = pr - base + lo + 1.0
    rows = {META_EA: grp * float(EXPERTS_PER_GROUP) + lo,
            META_EB: grp * float(EXPERTS_PER_GROUP) + hi,
            META_VALID: jnp.where(valid, 1.0, 0.0),
            META_BLK: jnp.minimum(tile_i, cum_tiles - 1.0),
            META_END: seg_end}
    rid = lax.broadcasted_iota(I32, (SUBLANES, LANES), 0)
    meta = jnp.zeros((SUBLANES, LANES), F32)
    for k, row in rows.items():
        meta = jnp.where(rid == k, row, meta)
    meta_ref[...] = meta.astype(I32)


def _route_plan(cls2d, tm):
    nrow = cls2d.shape[0]
    return pl.pallas_call(
        functools.partial(_plan_body, tm),
        in_specs=[pl.BlockSpec((nrow, LANES), lambda: (0, 0))],
        out_specs=(pl.BlockSpec((nrow, LANES), lambda: (0, 0)),
                   pl.BlockSpec((SUBLANES, LANES), lambda: (0, 0))),
        out_shape=(jax.ShapeDtypeStruct((nrow, LANES), I32),
                   jax.ShapeDtypeStruct((SUBLANES, LANES), I32)),
        name="route_plan",
    )(cls2d)


def _permute_body(dispatch, tm, nt, pos_ref, end_ref, valid_ref, src_ref, dst_ref, *scratch):
    sem = scratch[-1]
    g = pl.program_id(0)
    n = PERMUTE_ROWS

    if dispatch:
        zbuf, zsem = scratch[0], scratch[1]

        @pl.when(g == 0)
        def _():
            zbuf[...] = jnp.zeros_like(zbuf)

            def zero_tile(first_slot):
                cp = pltpu.make_async_copy(
                    zbuf, dst_ref.at[pl.ds(pl.multiple_of(first_slot * REC, REC), tm * REC), :], zsem)
                cp.start()
                cp.wait()

            for k in range(N_CLASSES):
                prev_end = end_ref[k - 1] if k > 0 else 0

                @pl.when(end_ref[k] > prev_end)
                def _():
                    zero_tile(end_ref[k] - tm)

            for i in range(nt):
                @pl.when(valid_ref[i] == 0)
                def _():
                    zero_tile(i * tm)

    base = g * n
    for r0 in range(0, n, PERMUTE_GROUP):
        slots = [pos_ref[base + r0 + k] for k in range(PERMUTE_GROUP)]
        for k, slot in enumerate(slots):
            here = pl.ds((r0 + k) * REC, REC)
            there = pl.ds(pl.multiple_of(slot * REC, REC), REC)
            if dispatch:
                cp = pltpu.make_async_copy(src_ref.at[here, :], dst_ref.at[there, :], sem)
            else:
                cp = pltpu.make_async_copy(src_ref.at[there, :], dst_ref.at[here, :], sem)
            cp.start(priority=k % 2)

    if dispatch:
        pltpu.make_async_copy(src_ref, dst_ref.at[pl.ds(0, n * REC), :], sem).wait()
    else:
        pltpu.make_async_copy(src_ref.at[pl.ds(0, n * REC), :], dst_ref, sem).wait()


def _row_permute(src, pos, meta, n_dst_rows, dispatch, tm, nt):
    n_tok = pos.shape[0]
    scratch = [pltpu.SemaphoreType.DMA(())]
    if dispatch:
        scratch = [pltpu.VMEM((tm * REC, LANES), src.dtype), pltpu.SemaphoreType.DMA(())] + scratch
    block = pl.BlockSpec((PERMUTE_ROWS * REC, LANES), lambda g, pos, end, valid: (g, 0))
    whole = pl.BlockSpec(memory_space=pl.ANY)
    grid_spec = pltpu.PrefetchScalarGridSpec(
        num_scalar_prefetch=3,
        grid=(n_tok // PERMUTE_ROWS,),
        in_specs=[block if dispatch else whole],
        out_specs=whole if dispatch else block,
        scratch_shapes=scratch,
    )
    return pl.pallas_call(
        functools.partial(_permute_body, dispatch, tm, nt),
        grid_spec=grid_spec,
        out_shape=jax.ShapeDtypeStruct((n_dst_rows * REC, LANES), src.dtype),
        compiler_params=pltpu.CompilerParams(
            dimension_semantics=("arbitrary",), vmem_limit_bytes=VMEM_LIMIT),
        name="row_dispatch" if dispatch else "row_combine",
    )(pos, meta[META_END, :N_CLASSES], meta[META_VALID, :nt], src)


def _moe_body(tm, ea_ref, eb_ref, valid_ref, blk_ref, h_ref, rwa_ref, rwb_ref,
              wga_ref, wgb_ref, wua_ref, wub_ref, wda_ref, wdb_ref, y_ref):
    i = pl.program_id(0)

    @pl.when(valid_ref[i] == 1)
    def _():
        xb = jnp.concatenate([_rows_of(h_ref, c, tm) for c in range(REC)], axis=1).astype(BF16)
        xf = xb.astype(F32)
        sa = jax.nn.sigmoid(jnp.sum(xf * rwa_ref[0].astype(F32), axis=-1, keepdims=True))
        sb = jax.nn.sigmoid(jnp.sum(xf * rwb_ref[0].astype(F32), axis=-1, keepdims=True))
        wa = sa / (sa + sb)
        wb = sb / (sa + sb)

        def expert(wg_ref, wu_ref, wd_ref, w):
            hg = jnp.dot(xb, wg_ref[0, 0], preferred_element_type=F32)
            hu = jnp.dot(xb, wu_ref[0, 0], preferred_element_type=F32)
            act = jax.nn.silu(hg) * hu * w
            return jnp.dot(act.astype(BF16), wd_ref[0, 0], preferred_element_type=F32)

        y = expert(wga_ref, wua_ref, wda_ref, wa) + expert(wgb_ref, wub_ref, wdb_ref, wb)
        for c in range(REC):
            y_ref[pl.ds(c, tm, stride=REC), :] = y[:, c * LANES:(c + 1) * LANES]

    @pl.when(valid_ref[i] == 0)
    def _():
        y_ref[...] = jnp.zeros_like(y_ref)


def _moe_tiles(l, hs, meta, router_rows, wg, wu, wd, tm, nt):
    d = D_MODEL
    rmap_a = lambda i, ea, eb, valid, blk: (ea[i], 0, 0)
    rmap_b = lambda i, ea, eb, valid, blk: (eb[i], 0, 0)
    wmap_a = lambda i, ea, eb, valid, blk: (l, ea[i], 0, 0)
    wmap_b = lambda i, ea, eb, valid, blk: (l, eb[i], 0, 0)
    grid_spec = pltpu.PrefetchScalarGridSpec(
        num_scalar_prefetch=4,
        grid=(nt,),
        in_specs=[
            pl.BlockSpec((tm * REC, LANES), lambda i, ea, eb, valid, blk: (blk[i], 0)),
            pl.BlockSpec((1, 1, d), rmap_a),
            pl.BlockSpec((1, 1, d), rmap_b),
            pl.BlockSpec((1, 1, d, D_EXPERT), wmap_a),
            pl.BlockSpec((1, 1, d, D_EXPERT), wmap_b),
            pl.BlockSpec((1, 1, d, D_EXPERT), wmap_a),
            pl.BlockSpec((1, 1, d, D_EXPERT), wmap_b),
            pl.BlockSpec((1, 1, D_EXPERT, d), wmap_a),
            pl.BlockSpec((1, 1, D_EXPERT, d), wmap_b),
        ],
        out_specs=pl.BlockSpec((tm * REC, LANES), lambda i, ea, eb, valid, blk: (i, 0)),
    )
    return pl.pallas_call(
        functools.partial(_moe_body, tm),
        grid_spec=grid_spec,
        out_shape=jax.ShapeDtypeStruct((nt * tm * REC, LANES), F32),
        compiler_params=pltpu.CompilerParams(
            dimension_semantics=("arbitrary",), vmem_limit_bytes=VMEM_LIMIT),
        name="moe_tiles",
    )(meta[META_EA, :nt], meta[META_EB, :nt], meta[META_VALID, :nt], meta[META_BLK, :nt],
      hs, router_rows, router_rows, wg, wg, wu, wu, wd, wd)


def _final_body(ts, x_ref, y_ref, mod_ref, g_ref, o_ref):
    y = jnp.concatenate([_rows_of(y_ref, c, ts) for c in range(REC)], axis=1)
    x = x_ref[...] + mod_ref[0, 0, 5:6, :] * y
    r = lax.rsqrt(jnp.mean(x * x, axis=-1, keepdims=True) + EPS)
    o_ref[...] = x * r * g_ref[...]


def _final_norm(x, y, mods, final_g, ts, batch):
    t, d = x.shape
    nj = t // batch // ts
    last = mods.shape[0] - 1
    tile = lambda bi, j: (bi * nj + j, 0)
    return pl.pallas_call(
        functools.partial(_final_body, ts),
        grid=(batch, nj),
        in_specs=[pl.BlockSpec((ts, d), tile), pl.BlockSpec((ts * REC, LANES), tile),
                  pl.BlockSpec((1, 1, N_MOD, d), lambda bi, j: (last, bi, 0, 0)),
                  pl.BlockSpec((1, d), lambda bi, j: (0, 0))],
        out_specs=pl.BlockSpec((ts, d), tile),
        out_shape=jax.ShapeDtypeStruct((t, d), F32),
        compiler_params=pltpu.CompilerParams(
            dimension_semantics=("arbitrary", "arbitrary"), vmem_limit_bytes=VMEM_LIMIT),
        name="final_norm",
    )(x, y, mods, final_g.reshape(1, d))


def _stacked_weights(norm1_g, norm2_g, w_in, pool_w, pool_scale, conv_dw, conv_db, conv_ln_g,
                     conv_ln_b, conv_pw, conv_pb, sg_ln_g, sg_ln_b, sg_w, sg_b, sc_w, out_norm_g, w_out):
    depth = w_in.shape[0]
    row = lambda v: v.reshape(depth, 1, -1)
    pool_bd = jnp.zeros((depth, GROUP_W, GROUP_W), pool_w.dtype)
    for i in range(len(POOL_WINDOWS)):
        pool_bd = pool_bd.at[:, i * POOL_CG:(i + 1) * POOL_CG, i * POOL_CG:(i + 1) * POOL_CG].set(
            pool_w[:, i])
    return {
        "n1g": row(norm1_g), "n2g": row(norm2_g),
        "w_in": w_in.astype(BF16),
        "pool_w": pool_bd.astype(BF16),
        "pool_scale": row(pool_scale),
        "conv_dw": jnp.pad(conv_dw, ((0, 0), (0, 1), (0, 0))),
        "conv_db": row(conv_db), "conv_ln_g": row(conv_ln_g), "conv_ln_b": row(conv_ln_b),
        "conv_pw": conv_pw.astype(BF16), "conv_pb": row(conv_pb),
        "sg_ln_g": row(sg_ln_g), "sg_ln_b": row(sg_ln_b),
        "sg_w": sg_w,
        "sg_b": jnp.repeat(jnp.swapaxes(sg_b, 1, 2), GROUP_W // SG_HEADS, axis=2),
        "sc_w": jnp.pad(sc_w, ((0, 0), (0, SUBLANES - SC_WIDTH), (0, 0))),
        "out_norm_g": row(out_norm_g),
        "w_out": w_out.astype(BF16),
    }


def kernel(x, c, norm1_g, norm2_g, w_ada, b_ada, w_in, pool_w, pool_scale, conv_dw, conv_db, conv_ln_g, conv_ln_b, conv_pw, conv_pb, sg_ln_g, sg_ln_b, sg_w, sg_b, sc_w, out_norm_g, w_out, router_w, router_bias, exp_w_gate, exp_w_up, exp_w_down, final_g):
    b, s, d = x.shape
    depth = w_in.shape[0]
    t = b * s
    ts = 512
    sb = 256
    tm = 256
    nt = t // tm + N_CLASSES
    assert d == D_MODEL and t % LANES == 0 and s % ts == 0 and t % tm == 0 and nt <= LANES
    assert t % PERMUTE_ROWS == 0 and ts % sb == 0 and sb % SG_CHUNK == 0

    mods = _ada_mod(c, w_ada, b_ada)

    perm = lambda v: v.reshape(N_EXPERT_GROUPS, EXPERTS_PER_GROUP, -1).transpose(1, 0, 2).reshape(
        N_EXPERTS, -1)
    lw = _stacked_weights(norm1_g, norm2_g, w_in, pool_w, pool_scale, conv_dw, conv_db, conv_ln_g,
                          conv_ln_b, conv_pw, conv_pb, sg_ln_g, sg_ln_b, sg_w, sg_b, sc_w, out_norm_g,
                          w_out)
    lw["router_wt"] = perm(router_w.T).astype(BF16)
    lw["router_b"] = perm(router_bias.reshape(N_EXPERTS, 1))
    router_rows = router_w.T.astype(BF16).reshape(N_EXPERTS, 1, d)
    wg = exp_w_gate.astype(BF16)
    wu = exp_w_up.astype(BF16)
    wd = exp_w_down.astype(BF16)

    xcur, yprev = x.reshape(t, d), None
    for l in range(depth):
        x1, rec, cls = _mixer_layer(l, xcur, yprev, mods, lw, ts, sb, b)
        pos2d, meta = _route_plan(cls.reshape(t // LANES, LANES), tm)
        pos = pos2d.reshape(t)
        hs = _row_permute(rec, pos, meta, nt * tm, True, tm, nt)
        ys = _moe_tiles(l, hs, meta, router_rows, wg, wu, wd, tm, nt)
        yprev = _row_permute(ys, pos, meta, t, False, tm, nt)
        xcur = x1
    return _final_norm(xcur, yprev, mods, final_g, ts, b).reshape(b, s, d)
```

```python
import functools

import jax
import jax.numpy as jnp
from jax import lax
from jax.experimental import pallas as pl
from jax.experimental.pallas import tpu as pltpu

D_MODEL = 1024
GROUP_W = 256
POOL_WINDOWS = (2, 4, 8, 16)
POOL_CG = 64
CONV_WIDTH = 31
SG_CHUNK = 128
SG_HEADS = 4
SC_WIDTH = 3
N_EXPERTS = 16
N_EXPERT_GROUPS = 4
EXPERTS_PER_GROUP = 4
PAIRS_PER_GROUP = 6
N_CLASSES = N_EXPERT_GROUPS * PAIRS_PER_GROUP
D_EXPERT = 256
N_MOD = 6
EPS = 1e-6
LN_EPS = 1e-5

LANES = 128
SUBLANES = 8
REC = SUBLANES
CONV_HALO = 32
POOL_HALO = 16
SC_HALO = 8
CONV_ROWS = 32
META_EA, META_EB, META_VALID, META_BLK, META_END = 0, 1, 2, 3, 4
PERMUTE_ROWS = 512
PERMUTE_GROUP = 16
VMEM_LIMIT = 56 * 1024 * 1024

F32 = jnp.float32
BF16 = jnp.bfloat16
I32 = jnp.int32


def _rows_of(ref, c, n, first=0):
    return ref[pl.ds(first * REC + c, n, stride=REC), :]


def _ada_body(c_ref, w_ref, b_ref, o_ref):
    ca = jax.nn.silu(c_ref[...])
    o_ref[0] = jnp.dot(ca, w_ref[0], preferred_element_type=F32,
                       precision=lax.Precision.HIGHEST) + b_ref[0]


def _ada_mod(c, w_ada, b_ada):
    depth, d, n = w_ada.shape
    b = c.shape[0]
    tn = 1024
    cp = jnp.pad(c, ((0, SUBLANES - b), (0, 0)))
    out = pl.pallas_call(
        _ada_body,
        grid=(depth, n // tn),
        in_specs=[
            pl.BlockSpec((SUBLANES, d), lambda l, j: (0, 0)),
            pl.BlockSpec((1, d, tn), lambda l, j: (l, 0, j)),
            pl.BlockSpec((1, 1, tn), lambda l, j: (l, 0, j)),
        ],
        out_specs=pl.BlockSpec((1, SUBLANES, tn), lambda l, j: (l, 0, j)),
        out_shape=jax.ShapeDtypeStruct((depth, SUBLANES, n), F32),
        compiler_params=pltpu.CompilerParams(
            dimension_semantics=("arbitrary", "arbitrary"), vmem_limit_bytes=VMEM_LIMIT),
        name="ada_mod",
    )(cp, w_ada, b_ada.reshape(depth, 1, n))
    return out[:, :b].reshape(depth, b, N_MOD, d)


def _layer_norm_rows(v, g, b):
    mu = jnp.mean(v, axis=-1, keepdims=True)
    vc = v - mu
    var = jnp.mean(vc * vc, axis=-1, keepdims=True)
    return vc * lax.rsqrt(var + LN_EPS) * g + b


def _carry_tail(ext_ref, j, ts, halo):
    nslab = ext_ref.shape[0]

    @pl.when(j == 0)
    def _():
        ext_ref[:, 0:halo, :] = jnp.zeros((nslab, halo, LANES), F32)

    @pl.when(j > 0)
    def _():
        ext_ref[:, 0:halo, :] = ext_ref[:, ts:ts + halo, :]


def _append_rows(ext_ref, cur, first):
    for cb in range(ext_ref.shape[0]):
        ext_ref[cb, first:first + cur.shape[0], :] = cur[:, cb * LANES:(cb + 1) * LANES]


def _first_argmax(vals):
    best = vals[0]
    idx = jnp.zeros_like(best)
    for k in range(1, len(vals)):
        better = vals[k] > best
        idx = jnp.where(better, float(k), idx)
        best = jnp.where(better, vals[k], best)
    return idx, best


def _pick(idx, vals):
    out = vals[0]
    for k in range(1, len(vals)):
        out = jnp.where(idx == float(k), vals[k], out)
    return out


def _gather_moe_rows(pos_ref, ys_hbm, ybuf, gsem, ts):
    g = pl.program_id(0) * pl.num_programs(1) + pl.program_id(1)
    n_steps = pl.num_programs(0) * pl.num_programs(1)

    def start(step, buf):
        base = step * ts
        for r0 in range(0, ts, PERMUTE_GROUP):
            slots = [pos_ref[base + r0 + k] for k in range(PERMUTE_GROUP)]
            for k, slot in enumerate(slots):
                pltpu.make_async_copy(
                    ys_hbm.at[pl.ds(pl.multiple_of(slot * REC, REC), REC), :],
                    ybuf.at[buf, pl.ds((r0 + k) * REC, REC), :],
                    gsem.at[buf]).start(priority=k % 2)

    @pl.when(g == 0)
    def _():
        start(0, 0)

    @pl.when(g + 1 < n_steps)
    def _():
        start(g + 1, (g + 1) % 2)

    cur = g % 2
    pltpu.make_async_copy(ys_hbm.at[pl.ds(0, ts * REC), :], ybuf.at[cur], gsem.at[cur]).wait()
    return ybuf.at[cur]


def _mixer_body(has_prev, ts, sb, *refs):
    refs = list(refs)
    if has_prev:
        pos_ref = refs.pop(0)
    x_ref = refs.pop(0)
    if has_prev:
        ys_hbm = refs.pop(0)
        modp_ref = refs.pop(0)
        gsem = refs.pop()
        ybuf = refs.pop()
    (mod_ref, n1g_ref, n2g_ref, win_ref, poolw_ref, pools_ref,
     cdw_ref, cdb_ref, clg_ref, clb_ref, cpw_ref, cpb_ref,
     slg_ref, slb_ref, sgw_ref, sgb_ref, scw_ref, ong_ref, wout_ref, rwt_ref, rb_ref,
     x1_ref, rec_ref, cls_ref,
     ea_ref, eg_ref, ed_ref) = refs

    j = pl.program_id(1)
    gw = GROUP_W
    nslab = gw // LANES
    if has_prev:
        yprev_ref = _gather_moe_rows(pos_ref, ys_hbm, ybuf, gsem, ts)

    sh1 = mod_ref[0, 0, 0:1, :]
    sc1 = mod_ref[0, 0, 1:2, :]
    g1 = mod_ref[0, 0, 2:3, :]
    sh2 = mod_ref[0, 0, 3:4, :]
    sc2 = mod_ref[0, 0, 4:5, :]
    gain1 = n1g_ref[0] * (1.0 + sc1)
    gain2 = n2g_ref[0] * (1.0 + sc2)

    _carry_tail(ea_ref, j, ts, POOL_HALO)
    _carry_tail(eg_ref, j, ts, CONV_HALO)
    _carry_tail(ed_ref, j, ts, SC_HALO)

    rowi = lax.broadcasted_iota(I32, (SG_CHUNK, SG_CHUNK), 0)
    coli = lax.broadcasted_iota(I32, (SG_CHUNK, SG_CHUNK), 1)
    tril = coli <= rowi
    wcat = jnp.concatenate(
        [jnp.where(tril, sgw_ref[0, hh], 0.0) for hh in range(SG_HEADS)], axis=1).astype(BF16)
    lane_g = lax.broadcasted_iota(I32, (SG_CHUNK, gw), 1) // (gw // SG_HEADS)
    zero_bf = jnp.zeros((SG_CHUNK, gw), BF16)
    lane = lax.broadcasted_iota(I32, (sb, LANES), 1)
    first = lane < POOL_CG

    for r0 in range(0, ts, sb):
        x = x_ref[r0:r0 + sb, :]
        if has_prev:
            yprev = jnp.concatenate([_rows_of(yprev_ref, c, sb, r0) for c in range(REC)], axis=1)
            x = x + modp_ref[0, 0, 5:6, :] * yprev

        r1 = lax.rsqrt(jnp.mean(x * x, axis=-1, keepdims=True) + EPS)
        h = x * r1 * gain1 + sh1
        p = jnp.dot(h.astype(BF16), win_ref[0], preferred_element_type=F32)

        a = p[:, 0:gw]
        _append_rows(ea_ref, a, POOL_HALO + r0)

        def a_shift(s, cb):
            return ea_ref[cb, POOL_HALO + r0 - s:POOL_HALO + r0 - s + sb, :]

        pos1 = (lax.broadcasted_iota(I32, (sb, LANES), 0) + (j * ts + r0 + 1)).astype(F32)
        a_lo = a[:, 0:LANES]
        s01 = a_lo + a_shift(1, 0)
        s03 = s01 + (a_shift(2, 0) + a_shift(3, 0))
        num_lo = jnp.where(first, s01, s03)
        den_lo = jnp.minimum(pos1, jnp.where(first, float(POOL_WINDOWS[0]), float(POOL_WINDOWS[1])))
        a_hi = a[:, LANES:gw]
        s07 = a_hi
        for s in range(1, 8):
            s07 = s07 + a_shift(s, 1)
        s815 = a_shift(8, 1)
        for s in range(9, 16):
            s815 = s815 + a_shift(s, 1)
        num_hi = jnp.where(first, s07, s07 + s815)
        den_hi = jnp.minimum(pos1, jnp.where(first, float(POOL_WINDOWS[2]), float(POOL_WINDOWS[3])))
        d_pool = jnp.concatenate([num_lo / den_lo - a_lo, num_hi / den_hi - a_hi], axis=1)
        y_a = jnp.dot(d_pool.astype(BF16), poolw_ref[0], preferred_element_type=F32) * pools_ref[0]

        glu = p[:, gw:2 * gw] * jax.nn.sigmoid(p[:, 2 * gw:3 * gw])
        _append_rows(eg_ref, glu, CONV_HALO + r0)
        conv_cols = []
        for cb in range(nslab):
            conv_chunks = []
            for rr in range(r0, r0 + sb, CONV_ROWS):
                acc = None
                for k in range(CONV_WIDTH):
                    off = CONV_HALO - (CONV_WIDTH - 1) + k + rr
                    term = (eg_ref[cb, off:off + CONV_ROWS, :]
                            * cdw_ref[0, k:k + 1, cb * LANES:(cb + 1) * LANES])
                    acc = term if acc is None else acc + term
                conv_chunks.append(acc)
            conv_cols.append(jnp.concatenate(conv_chunks, axis=0))
        hb = jnp.concatenate(conv_cols, axis=1) + cdb_ref[0]
        hb = jax.nn.silu(_layer_norm_rows(hb, clg_ref[0], clb_ref[0]))
        y_b = jnp.dot(hb.astype(BF16), cpw_ref[0], preferred_element_type=F32) + cpb_ref[0]

        u = p[:, 3 * gw:4 * gw]
        vln = _layer_norm_rows(p[:, 4 * gw:5 * gw], slg_ref[0], slb_ref[0]).astype(BF16)
        yc_chunks = []
        for n in range(sb // SG_CHUNK):
            vch = vln[n * SG_CHUNK:(n + 1) * SG_CHUNK, :]
            vbd = jnp.concatenate([jnp.where(lane_g == hh, vch, zero_bf) for hh in range(SG_HEADS)],
                                  axis=0)
            mixed = jnp.dot(wcat, vbd, preferred_element_type=F32) + sgb_ref[0]
            yc_chunks.append(u[n * SG_CHUNK:(n + 1) * SG_CHUNK, :] * mixed)
        y_c = jnp.concatenate(yc_chunks, axis=0)

        cx = p[:, 6 * gw:7 * gw] * p[:, 7 * gw:8 * gw]
        _append_rows(ed_ref, cx, SC_HALO + r0)
        cd_cols = []
        for cb in range(nslab):
            ls = slice(cb * LANES, (cb + 1) * LANES)
            at = SC_HALO + r0
            cd_cols.append(cx[:, ls] * scw_ref[0, 2:3, ls]
                           + ed_ref[cb, at - 1:at - 1 + sb, :] * scw_ref[0, 1:2, ls]
                           + ed_ref[cb, at - 2:at - 2 + sb, :] * scw_ref[0, 0:1, ls])
        y_d = p[:, 5 * gw:6 * gw] * jnp.concatenate(cd_cols, axis=1)

        normed = []
        for gi, yg in enumerate((y_a, y_b, y_c, y_d)):
            rg = lax.rsqrt(jnp.mean(yg * yg, axis=-1, keepdims=True) + EPS)
            normed.append((yg * rg * ong_ref[0, :, gi * gw:(gi + 1) * gw]).astype(BF16))
        yn = jnp.concatenate(normed, axis=1)
        x1 = x + g1 * jnp.dot(yn, wout_ref[0], preferred_element_type=F32)
        x1_ref[r0:r0 + sb, :] = x1

        r2 = lax.rsqrt(jnp.mean(x1 * x1, axis=-1, keepdims=True) + EPS)
        h2 = x1 * r2 * gain2 + sh2
        for c in range(REC):
            rec_ref[pl.ds(r0 * REC + c, sb, stride=REC), :] = h2[:, c * LANES:(c + 1) * LANES]
        logits = lax.dot_general(rwt_ref[...], h2.astype(BF16), (((1,), (1,)), ((), ())),
                                 preferred_element_type=F32)
        sel = jax.nn.sigmoid(logits) + rb_ref[...]
        epg = EXPERTS_PER_GROUP
        sel_s = [sel[jj * N_EXPERT_GROUPS:(jj + 1) * N_EXPERT_GROUPS, :] for jj in range(epg)]
        top2 = None
        for ja in range(epg):
            for jb in range(ja + 1, epg):
                pair = sel_s[ja] + sel_s[jb]
                top2 = pair if top2 is None else jnp.maximum(top2, pair)
        gidx, _ = _first_argmax([top2[g:g + 1, :] for g in range(N_EXPERT_GROUPS)])
        sel_c = [_pick(gidx, [sel_s[jj][g:g + 1, :] for g in range(N_EXPERT_GROUPS)])
                 for jj in range(epg)]
        i1, _ = _first_argmax(sel_c)
        i2, _ = _first_argmax([jnp.where(i1 == float(jj), -jnp.inf, sel_c[jj]) for jj in range(epg)])
        lo = jnp.minimum(i1, i2)
        hi = jnp.maximum(i1, i2)
        base = jnp.where(lo == 0.0, 0.0, jnp.where(lo == 1.0, 3.0, 5.0))
        cls = gidx * float(PAIRS_PER_GROUP) + base + (hi - lo - 1.0)
        cls_ref[0, :, r0:r0 + sb] = cls.astype(I32)


def _mixer_layer(l, x, ys_prev, pos_prev, mods, lw, ts, sb, batch):
    t, d = x.shape
    nj = t // batch // ts
    has_prev = ys_prev is not None
    tile = lambda bi, j, *_: (bi * nj + j, 0)

    args = [x]
    in_specs = [pl.BlockSpec((ts, d), tile)]
    if has_prev:
        args += [ys_prev, mods]
        in_specs += [pl.BlockSpec(memory_space=pl.ANY),
                     pl.BlockSpec((1, 1, N_MOD, d), lambda bi, j, *_: (l - 1, bi, 0, 0))]
    args += [mods]
    in_specs += [pl.BlockSpec((1, 1, N_MOD, d), lambda bi, j, *_: (l, bi, 0, 0))]
    for name in ("n1g", "n2g", "w_in", "pool_w", "pool_scale", "conv_dw", "conv_db", "conv_ln_g",
                 "conv_ln_b", "conv_pw", "conv_pb", "sg_ln_g", "sg_ln_b", "sg_w", "sg_b", "sc_w",
                 "out_norm_g", "w_out"):
        arr = lw[name]
        args.append(arr)
        in_specs.append(pl.BlockSpec((1,) + arr.shape[1:],
                                     lambda bi, j, *_, nd=arr.ndim: (l,) + (0,) * (nd - 1)))
    for name in ("router_wt", "router_b"):
        arr = lw[name]
        args.append(arr)
        in_specs.append(pl.BlockSpec(arr.shape, lambda bi, j, *_: (0, 0)))

    out_shape = (jax.ShapeDtypeStruct((t, d), F32),
                 jax.ShapeDtypeStruct((t * REC, LANES), F32),
                 jax.ShapeDtypeStruct((t // ts, 1, ts), I32))
    out_specs = (pl.BlockSpec((ts, d), tile),
                 pl.BlockSpec((ts * REC, LANES), tile),
                 pl.BlockSpec((1, 1, ts), lambda bi, j, *_: (bi * nj + j, 0, 0)))
    nslab = GROUP_W // LANES
    scratch = [pltpu.VMEM((nslab, POOL_HALO + ts, LANES), F32),
               pltpu.VMEM((nslab, CONV_HALO + ts, LANES), F32),
               pltpu.VMEM((nslab, SC_HALO + ts, LANES), F32)]
    if has_prev:
        scratch += [pltpu.VMEM((2, ts * REC, LANES), F32), pltpu.SemaphoreType.DMA((2,))]
        args = [pos_prev] + args
    grid_spec = pltpu.PrefetchScalarGridSpec(
        num_scalar_prefetch=1 if has_prev else 0,
        grid=(batch, nj),
        in_specs=in_specs,
        out_specs=out_specs,
        scratch_shapes=scratch,
    )
    return pl.pallas_call(
        functools.partial(_mixer_body, has_prev, ts, sb),
        grid_spec=grid_spec,
        out_shape=out_shape,
        compiler_params=pltpu.CompilerParams(
            dimension_semantics=("arbitrary", "arbitrary"), vmem_limit_bytes=VMEM_LIMIT),
        name="mixer_layer",
    )(*args)


def _plan_body(tm, cls_ref, pos_ref, meta_ref):
    cls = cls_ref[...]
    nrow = cls.shape[0]
    ii = lax.broadcasted_iota(I32, (LANES, LANES), 0)
    jj = lax.broadcasted_iota(I32, (LANES, LANES), 1)
    upper = jnp.where(ii <= jj, 1.0, 0.0).astype(BF16)
    ri = lax.broadcasted_iota(I32, (nrow, nrow), 0)
    rj = lax.broadcasted_iota(I32, (nrow, nrow), 1)
    lower = jnp.where(rj < ri, 1.0, 0.0).astype(BF16)
    lane_i = lax.broadcasted_iota(I32, (1, LANES), 1)
    tile_i = lane_i.astype(F32)
    start = jnp.zeros((1, 1), F32)
    cum_tiles = jnp.zeros((1, 1), F32)
    pos = jnp.zeros((nrow, LANES), F32)
    tclass = jnp.zeros((1, LANES), F32)
    seg_end = jnp.zeros((1, LANES), F32)
    for k in range(N_CLASSES):
        hit = cls == k
        oh = jnp.where(hit, 1.0, 0.0)
        ohb = oh.astype(BF16)
        incl = jnp.dot(ohb, upper, preferred_element_type=F32)
        above = jnp.sum(jnp.dot(lower, ohb, preferred_element_type=F32), axis=1, keepdims=True)
        rank = incl - oh + above
        pos = pos + jnp.where(hit, start + rank, 0.0)
        cnt = jnp.sum(jnp.sum(oh, axis=1, keepdims=True), axis=0, keepdims=True)
        n_tiles = jnp.floor((cnt + float(tm - 1)) * (1.0 / tm))
        start = start + n_tiles * float(tm)
        cum_tiles = cum_tiles + n_tiles
        tclass = tclass + jnp.where(tile_i >= cum_tiles, 1.0, 0.0)
        seg_end = jnp.where(lane_i == k, start, seg_end)
    pos_ref[...] = pos.astype(I32)

    valid = tclass < float(N_CLASSES)
    kc = jnp.minimum(tclass, float(N_CLASSES - 1))
    grp = jnp.zeros_like(kc)
    for g in range(1, N_EXPERT_GROUPS):
        grp = grp + jnp.where(kc >= float(g * PAIRS_PER_GROUP), 1.0, 0.0)
    pr = kc - grp * float(PAIRS_PER_GROUP)
    lo = jnp.where(pr >= 3.0, 1.0, 0.0) + jnp.where(pr >= 5.0, 1.0, 0.0)
    base = jnp.where(lo == 0.0, 0.0, jnp.where(lo == 1.0, 3.0, 5.0))
    hi = pr - base + lo + 1.0
    rows = {META_EA: grp * float(EXPERTS_PER_GROUP) + lo,
            META_EB: grp * float(EXPERTS_PER_GROUP) + hi,
            META_VALID: jnp.where(valid, 1.0, 0.0),
            META_BLK: jnp.minimum(tile_i, cum_tiles - 1.0),
            META_END: seg_end}
    rid = lax.broadcasted_iota(I32, (SUBLANES, LANES), 0)
    meta = jnp.zeros((SUBLANES, LANES), F32)
    for k, row in rows.items():
        meta = jnp.where(rid == k, row, meta)
    meta_ref[...] = meta.astype(I32)


def _route_plan(cls2d, tm):
    nrow = cls2d.shape[0]
    return pl.pallas_call(
        functools.partial(_plan_body, tm),
        in_specs=[pl.BlockSpec((nrow, LANES), lambda: (0, 0))],
        out_specs=(pl.BlockSpec((nrow, LANES), lambda: (0, 0)),
                   pl.BlockSpec((SUBLANES, LANES), lambda: (0, 0))),
        out_shape=(jax.ShapeDtypeStruct((nrow, LANES), I32),
                   jax.ShapeDtypeStruct((SUBLANES, LANES), I32)),
        name="route_plan",
    )(cls2d)


def _dispatch_body(tm, nt, pos_ref, end_ref, valid_ref, src_ref, dst_hbm, zbuf, zsem, sem):
    g = pl.program_id(0)
    n = PERMUTE_ROWS

    @pl.when(g == 0)
    def _():
        zbuf[...] = jnp.zeros_like(zbuf)

        def zero_tile(first_slot):
            cp = pltpu.make_async_copy(
                zbuf, dst_hbm.at[pl.ds(pl.multiple_of(first_slot * REC, REC), tm * REC), :], zsem)
            cp.start()
            cp.wait()

        for k in range(N_CLASSES):
            prev_end = end_ref[k - 1] if k > 0 else 0

            @pl.when(end_ref[k] > prev_end)
            def _():
                zero_tile(end_ref[k] - tm)

        for i in range(nt):
            @pl.when(valid_ref[i] == 0)
            def _():
                zero_tile(i * tm)

    base = g * n
    for r0 in range(0, n, PERMUTE_GROUP):
        slots = [pos_ref[base + r0 + k] for k in range(PERMUTE_GROUP)]
        for k, slot in enumerate(slots):
            pltpu.make_async_copy(
                src_ref.at[pl.ds((r0 + k) * REC, REC), :],
                dst_hbm.at[pl.ds(pl.multiple_of(slot * REC, REC), REC), :],
                sem).start(priority=k % 2)

    pltpu.make_async_copy(src_ref, dst_hbm.at[pl.ds(0, n * REC), :], sem).wait()


def _row_dispatch(rec, pos, meta, tm, nt):
    n_tok = pos.shape[0]
    grid_spec = pltpu.PrefetchScalarGridSpec(
        num_scalar_prefetch=3,
        grid=(n_tok // PERMUTE_ROWS,),
        in_specs=[pl.BlockSpec((PERMUTE_ROWS * REC, LANES), lambda g, pos, end, valid: (g, 0))],
        out_specs=pl.BlockSpec(memory_space=pl.ANY),
        scratch_shapes=[pltpu.VMEM((tm * REC, LANES), rec.dtype), pltpu.SemaphoreType.DMA(()),
                        pltpu.SemaphoreType.DMA(())],
    )
    return pl.pallas_call(
        functools.partial(_dispatch_body, tm, nt),
        grid_spec=grid_spec,
        out_shape=jax.ShapeDtypeStruct((nt * tm * REC, LANES), rec.dtype),
        compiler_params=pltpu.CompilerParams(
            dimension_semantics=("arbitrary",), vmem_limit_bytes=VMEM_LIMIT),
        name="row_dispatch",
    )(pos, meta[META_END, :N_CLASSES], meta[META_VALID, :nt], rec)


def _moe_body(tm, ea_ref, eb_ref, valid_ref, blk_ref, h_ref, rwa_ref, rwb_ref,
              wga_ref, wgb_ref, wua_ref, wub_ref, wda_ref, wdb_ref, y_ref):
    i = pl.program_id(0)

    @pl.when(valid_ref[i] == 1)
    def _():
        xb = jnp.concatenate([_rows_of(h_ref, c, tm) for c in range(REC)], axis=1).astype(BF16)
        xf = xb.astype(F32)
        sa = jax.nn.sigmoid(jnp.sum(xf * rwa_ref[0].astype(F32), axis=-1, keepdims=True))
        sb = jax.nn.sigmoid(jnp.sum(xf * rwb_ref[0].astype(F32), axis=-1, keepdims=True))
        wa = sa / (sa + sb)
        wb = sb / (sa + sb)

        def expert(wg_ref, wu_ref, wd_ref, w):
            hg = jnp.dot(xb, wg_ref[0, 0], preferred_element_type=F32)
            hu = jnp.dot(xb, wu_ref[0, 0], preferred_element_type=F32)
            act = jax.nn.silu(hg) * hu * w
            return jnp.dot(act.astype(BF16), wd_ref[0, 0], preferred_element_type=F32)

        y = expert(wga_ref, wua_ref, wda_ref, wa) + expert(wgb_ref, wub_ref, wdb_ref, wb)
        for c in range(REC):
            y_ref[pl.ds(c, tm, stride=REC), :] = y[:, c * LANES:(c + 1) * LANES]

    @pl.when(valid_ref[i] == 0)
    def _():
        y_ref[...] = jnp.zeros_like(y_ref)


def _moe_tiles(l, hs, meta, router_rows, wg, wu, wd, tm, nt):
    d = D_MODEL
    rmap_a = lambda i, ea, eb, valid, blk: (ea[i], 0, 0)
    rmap_b = lambda i, ea, eb, valid, blk: (eb[i], 0, 0)
    wmap_a = lambda i, ea, eb, valid, blk: (l, ea[i], 0, 0)
    wmap_b = lambda i, ea, eb, valid, blk: (l, eb[i], 0, 0)
    grid_spec = pltpu.PrefetchScalarGridSpec(
        num_scalar_prefetch=4,
        grid=(nt,),
        in_specs=[
            pl.BlockSpec((tm * REC, LANES), lambda i, ea, eb, valid, blk: (blk[i], 0)),
            pl.BlockSpec((1, 1, d), rmap_a),
            pl.BlockSpec((1, 1, d), rmap_b),
            pl.BlockSpec((1, 1, d, D_EXPERT), wmap_a),
            pl.BlockSpec((1, 1, d, D_EXPERT), wmap_b),
            pl.BlockSpec((1, 1, d, D_EXPERT), wmap_a),
            pl.BlockSpec((1, 1, d, D_EXPERT), wmap_b),
            pl.BlockSpec((1, 1, D_EXPERT, d), wmap_a),
            pl.BlockSpec((1, 1, D_EXPERT, d), wmap_b),
        ],
        out_specs=pl.BlockSpec((tm * REC, LANES), lambda i, ea, eb, valid, blk: (i, 0)),
    )
    return pl.pallas_call(
        functools.partial(_moe_body, tm),
        grid_spec=grid_spec,
        out_shape=jax.ShapeDtypeStruct((nt * tm * REC, LANES), F32),
        compiler_params=pltpu.CompilerParams(
            dimension_semantics=("arbitrary",), vmem_limit_bytes=VMEM_LIMIT),
        name="moe_tiles",
    )(meta[META_EA, :nt], meta[META_EB, :nt], meta[META_VALID, :nt], meta[META_BLK, :nt],
      hs, router_rows, router_rows, wg, wg, wu, wu, wd, wd)


def _final_body(ts, pos_ref, x_ref, ys_hbm, mod_ref, g_ref, o_ref, ybuf, gsem):
    y_ref = _gather_moe_rows(pos_ref, ys_hbm, ybuf, gsem, ts)
    y = jnp.concatenate([_rows_of(y_ref, c, ts) for c in range(REC)], axis=1)
    x = x_ref[...] + mod_ref[0, 0, 5:6, :] * y
    r = lax.rsqrt(jnp.mean(x * x, axis=-1, keepdims=True) + EPS)
    o_ref[...] = x * r * g_ref[...]


def _final_norm(x, ys, pos, mods, final_g, ts, batch):
    t, d = x.shape
    nj = t // batch // ts
    last = mods.shape[0] - 1
    tile = lambda bi, j, pos: (bi * nj + j, 0)
    grid_spec = pltpu.PrefetchScalarGridSpec(
        num_scalar_prefetch=1,
        grid=(batch, nj),
        in_specs=[pl.BlockSpec((ts, d), tile), pl.BlockSpec(memory_space=pl.ANY),
                  pl.BlockSpec((1, 1, N_MOD, d), lambda bi, j, pos: (last, bi, 0, 0)),
                  pl.BlockSpec((1, d), lambda bi, j, pos: (0, 0))],
        out_specs=pl.BlockSpec((ts, d), tile),
        scratch_shapes=[pltpu.VMEM((2, ts * REC, LANES), F32), pltpu.SemaphoreType.DMA((2,))],
    )
    return pl.pallas_call(
        functools.partial(_final_body, ts),
        grid_spec=grid_spec,
        out_shape=jax.ShapeDtypeStruct((t, d), F32),
        compiler_params=pltpu.CompilerParams(
            dimension_semantics=("arbitrary", "arbitrary"), vmem_limit_bytes=VMEM_LIMIT),
        name="final_norm",
    )(pos, x, ys, mods, final_g.reshape(1, d))


def _stacked_weights(norm1_g, norm2_g, w_in, pool_w, pool_scale, conv_dw, conv_db, conv_ln_g,
                     conv_ln_b, conv_pw, conv_pb, sg_ln_g, sg_ln_b, sg_w, sg_b, sc_w, out_norm_g, w_out):
    depth = w_in.shape[0]
    row = lambda v: v.reshape(depth, 1, -1)
    pool_bd = jnp.zeros((depth, GROUP_W, GROUP_W), pool_w.dtype)
    for i in range(len(POOL_WINDOWS)):
        pool_bd = pool_bd.at[:, i * POOL_CG:(i + 1) * POOL_CG, i * POOL_CG:(i + 1) * POOL_CG].set(
            pool_w[:, i])
    return {
        "n1g": row(norm1_g), "n2g": row(norm2_g),
        "w_in": w_in.astype(BF16),
        "pool_w": pool_bd.astype(BF16),
        "pool_scale": row(pool_scale),
        "conv_dw": jnp.pad(conv_dw, ((0, 0), (0, 1), (0, 0))),
        "conv_db": row(conv_db), "conv_ln_g": row(conv_ln_g), "conv_ln_b": row(conv_ln_b),
        "conv_pw": conv_pw.astype(BF16), "conv_pb": row(conv_pb),
        "sg_ln_g": row(sg_ln_g), "sg_ln_b": row(sg_ln_b),
        "sg_w": sg_w,
        "sg_b": jnp.repeat(jnp.swapaxes(sg_b, 1, 2), GROUP_W // SG_HEADS, axis=2),
        "sc_w": jnp.pad(sc_w, ((0, 0), (0, SUBLANES - SC_WIDTH), (0, 0))),
        "out_norm_g": row(out_norm_g),
        "w_out": w_out.astype(BF16),
    }


def kernel(x, c, norm1_g, norm2_g, w_ada, b_ada, w_in, pool_w, pool_scale, conv_dw, conv_db, conv_ln_g, conv_ln_b, conv_pw, conv_pb, sg_ln_g, sg_ln_b, sg_w, sg_b, sc_w, out_norm_g, w_out, router_w, router_bias, exp_w_gate, exp_w_up, exp_w_down, final_g):
    b, s, d = x.shape
    depth = w_in.shape[0]
    t = b * s
    ts = 512
    sb = 256
    tm = 256
    nt = t // tm + N_CLASSES
    assert d == D_MODEL and t % LANES == 0 and s % ts == 0 and t % tm == 0 and nt <= LANES
    assert t % PERMUTE_ROWS == 0 and ts % sb == 0 and sb % SG_CHUNK == 0

    mods = _ada_mod(c, w_ada, b_ada)

    perm = lambda v: v.reshape(N_EXPERT_GROUPS, EXPERTS_PER_GROUP, -1).transpose(1, 0, 2).reshape(
        N_EXPERTS, -1)
    lw = _stacked_weights(norm1_g, norm2_g, w_in, pool_w, pool_scale, conv_dw, conv_db, conv_ln_g,
                          conv_ln_b, conv_pw, conv_pb, sg_ln_g, sg_ln_b, sg_w, sg_b, sc_w, out_norm_g,
                          w_out)
    lw["router_wt"] = perm(router_w.T).astype(BF16)
    lw["router_b"] = perm(router_bias.reshape(N_EXPERTS, 1))
    router_rows = router_w.T.astype(BF16).reshape(N_EXPERTS, 1, d)
    wg = exp_w_gate.astype(BF16)
    wu = exp_w_up.astype(BF16)
    wd = exp_w_down.astype(BF16)

    xcur, ys, pos = x.reshape(t, d), None, None
    for l in range(depth):
        xcur, rec, cls = _mixer_layer(l, xcur, ys, pos, mods, lw, ts, sb, b)
        pos2d, meta = _route_plan(cls.reshape(t // LANES, LANES), tm)
        pos = pos2d.reshape(t)
        hs = _row_dispatch(rec, pos, meta, tm, nt)
        ys = _moe_tiles(l, hs, meta, router_rows, wg, wu, wd, tm, nt)
    return _final_norm(xcur, ys, pos, mods, final_g, ts, b).reshape(b, s, d)
```

```python
import functools

import jax
import jax.numpy as jnp
from jax import lax
from jax.experimental import pallas as pl
from jax.experimental.pallas import tpu as pltpu

D_MODEL = 1024
GROUP_W = 256
POOL_WINDOWS = (2, 4, 8, 16)
POOL_CG = 64
CONV_WIDTH = 31
SG_CHUNK = 128
SG_HEADS = 4
SC_WIDTH = 3
N_EXPERTS = 16
N_EXPERT_GROUPS = 4
EXPERTS_PER_GROUP = 4
PAIRS_PER_GROUP = 6
N_CLASSES = N_EXPERT_GROUPS * PAIRS_PER_GROUP
D_EXPERT = 256
N_MOD = 6
EPS = 1e-6
LN_EPS = 1e-5

LANES = 128
SUBLANES = 8
REC = SUBLANES
CONV_HALO = 32
POOL_HALO = 16
SC_HALO = 8
CONV_ROWS = 32
META_EA, META_EB, META_VALID, META_BLK, META_END = 0, 1, 2, 3, 4
PERMUTE_ROWS = 1024
PERMUTE_GROUP = 16
VMEM_LIMIT = 56 * 1024 * 1024

F32 = jnp.float32
BF16 = jnp.bfloat16
I32 = jnp.int32


def _rows_of(ref, c, n, first=0):
    return ref[pl.ds(first * REC + c, n, stride=REC), :]


def _ada_body(c_ref, w_ref, b_ref, o_ref):
    ca = jax.nn.silu(c_ref[...])
    o_ref[0] = jnp.dot(ca.astype(BF16), w_ref[0].astype(BF16), preferred_element_type=F32) + b_ref[0]


def _ada_mod(c, w_ada, b_ada):
    depth, d, n = w_ada.shape
    b = c.shape[0]
    tn = 1024
    cp = jnp.pad(c, ((0, SUBLANES - b), (0, 0)))
    out = pl.pallas_call(
        _ada_body,
        grid=(depth, n // tn),
        in_specs=[
            pl.BlockSpec((SUBLANES, d), lambda l, j: (0, 0)),
            pl.BlockSpec((1, d, tn), lambda l, j: (l, 0, j)),
            pl.BlockSpec((1, 1, tn), lambda l, j: (l, 0, j)),
        ],
        out_specs=pl.BlockSpec((1, SUBLANES, tn), lambda l, j: (l, 0, j)),
        out_shape=jax.ShapeDtypeStruct((depth, SUBLANES, n), F32),
        compiler_params=pltpu.CompilerParams(
            dimension_semantics=("arbitrary", "arbitrary"), vmem_limit_bytes=VMEM_LIMIT),
        name="ada_mod",
    )(cp, w_ada, b_ada.reshape(depth, 1, n))
    return out[:, :b].reshape(depth, b, N_MOD, d)


def _layer_norm_rows(v, g, b):
    mu = jnp.mean(v, axis=-1, keepdims=True)
    vc = v - mu
    var = jnp.mean(vc * vc, axis=-1, keepdims=True)
    return vc * lax.rsqrt(var + LN_EPS) * g + b


def _carry_tail(ext_ref, j, ts, halo):
    nslab = ext_ref.shape[0]

    @pl.when(j == 0)
    def _():
        ext_ref[:, 0:halo, :] = jnp.zeros((nslab, halo, LANES), F32)

    @pl.when(j > 0)
    def _():
        ext_ref[:, 0:halo, :] = ext_ref[:, ts:ts + halo, :]


def _append_rows(ext_ref, cur, first):
    for cb in range(ext_ref.shape[0]):
        ext_ref[cb, first:first + cur.shape[0], :] = cur[:, cb * LANES:(cb + 1) * LANES]


def _first_argmax(vals):
    best = vals[0]
    idx = jnp.zeros_like(best)
    for k in range(1, len(vals)):
        better = vals[k] > best
        idx = jnp.where(better, float(k), idx)
        best = jnp.where(better, vals[k], best)
    return idx, best


def _pick(idx, vals):
    out = vals[0]
    for k in range(1, len(vals)):
        out = jnp.where(idx == float(k), vals[k], out)
    return out


def _gather_moe_rows(pos_ref, ys_hbm, ybuf, gsem, ts):
    g = pl.program_id(0) * pl.num_programs(1) + pl.program_id(1)
    n_steps = pl.num_programs(0) * pl.num_programs(1)

    def start(step, buf):
        base = step * ts
        for r0 in range(0, ts, PERMUTE_GROUP):
            slots = [pos_ref[base + r0 + k] for k in range(PERMUTE_GROUP)]
            for k, slot in enumerate(slots):
                pltpu.make_async_copy(
                    ys_hbm.at[pl.ds(pl.multiple_of(slot * REC, REC), REC), :],
                    ybuf.at[buf, pl.ds((r0 + k) * REC, REC), :],
                    gsem.at[buf]).start(priority=k % 2)

    @pl.when(g == 0)
    def _():
        start(0, 0)

    @pl.when(g + 1 < n_steps)
    def _():
        start(g + 1, (g + 1) % 2)

    cur = g % 2
    pltpu.make_async_copy(ys_hbm.at[pl.ds(0, ts * REC), :], ybuf.at[cur], gsem.at[cur]).wait()
    return ybuf.at[cur]


def _mixer_body(has_prev, ts, sb, *refs):
    refs = list(refs)
    if has_prev:
        pos_ref = refs.pop(0)
    x_ref = refs.pop(0)
    if has_prev:
        ys_hbm = refs.pop(0)
        modp_ref = refs.pop(0)
        gsem = refs.pop()
        ybuf = refs.pop()
    (mod_ref, n1g_ref, n2g_ref, win_ref, poolw_ref, pools_ref,
     cdw_ref, cdb_ref, clg_ref, clb_ref, cpw_ref, cpb_ref,
     slg_ref, slb_ref, sgw_ref, sgb_ref, scw_ref, ong_ref, wout_ref, rwt_ref, rb_ref,
     x1_ref, rec_ref, cls_ref,
     ea_ref, eg_ref, ed_ref) = refs

    j = pl.program_id(1)
    gw = GROUP_W
    nslab = gw // LANES
    if has_prev:
        yprev_ref = _gather_moe_rows(pos_ref, ys_hbm, ybuf, gsem, ts)

    sh1 = mod_ref[0, 0, 0:1, :]
    sc1 = mod_ref[0, 0, 1:2, :]
    g1 = mod_ref[0, 0, 2:3, :]
    sh2 = mod_ref[0, 0, 3:4, :]
    sc2 = mod_ref[0, 0, 4:5, :]
    gain1 = n1g_ref[0] * (1.0 + sc1)
    gain2 = n2g_ref[0] * (1.0 + sc2)

    _carry_tail(ea_ref, j, ts, POOL_HALO)
    _carry_tail(eg_ref, j, ts, CONV_HALO)
    _carry_tail(ed_ref, j, ts, SC_HALO)

    rowi = lax.broadcasted_iota(I32, (SG_CHUNK, SG_CHUNK), 0)
    coli = lax.broadcasted_iota(I32, (SG_CHUNK, SG_CHUNK), 1)
    tril = coli <= rowi
    wcat = jnp.concatenate(
        [jnp.where(tril, sgw_ref[0, hh], 0.0) for hh in range(SG_HEADS)], axis=1).astype(BF16)
    lane_g = lax.broadcasted_iota(I32, (SG_CHUNK, gw), 1) // (gw // SG_HEADS)
    zero_bf = jnp.zeros((SG_CHUNK, gw), BF16)
    lane = lax.broadcasted_iota(I32, (sb, LANES), 1)
    first = lane < POOL_CG

    for r0 in range(0, ts, sb):
        x = x_ref[r0:r0 + sb, :]
        if has_prev:
            yprev = jnp.concatenate([_rows_of(yprev_ref, c, sb, r0) for c in range(REC)], axis=1)
            x = x + modp_ref[0, 0, 5:6, :] * yprev

        r1 = lax.rsqrt(jnp.mean(x * x, axis=-1, keepdims=True) + EPS)
        h = x * r1 * gain1 + sh1
        p = jnp.dot(h.astype(BF16), win_ref[0], preferred_element_type=F32)

        a = p[:, 0:gw]
        _append_rows(ea_ref, a, POOL_HALO + r0)

        def a_shift(s, cb):
            return ea_ref[cb, POOL_HALO + r0 - s:POOL_HALO + r0 - s + sb, :]

        pos1 = (lax.broadcasted_iota(I32, (sb, LANES), 0) + (j * ts + r0 + 1)).astype(F32)
        a_lo = a[:, 0:LANES]
        s01 = a_lo + a_shift(1, 0)
        s03 = s01 + (a_shift(2, 0) + a_shift(3, 0))
        num_lo = jnp.where(first, s01, s03)
        den_lo = jnp.minimum(pos1, jnp.where(first, float(POOL_WINDOWS[0]), float(POOL_WINDOWS[1])))
        a_hi = a[:, LANES:gw]
        s07 = a_hi
        for s in range(1, 8):
            s07 = s07 + a_shift(s, 1)
        s815 = a_shift(8, 1)
        for s in range(9, 16):
            s815 = s815 + a_shift(s, 1)
        num_hi = jnp.where(first, s07, s07 + s815)
        den_hi = jnp.minimum(pos1, jnp.where(first, float(POOL_WINDOWS[2]), float(POOL_WINDOWS[3])))
        d_pool = jnp.concatenate([num_lo / den_lo - a_lo, num_hi / den_hi - a_hi], axis=1)
        y_a = jnp.dot(d_pool.astype(BF16), poolw_ref[0], preferred_element_type=F32) * pools_ref[0]

        glu = p[:, gw:2 * gw] * jax.nn.sigmoid(p[:, 2 * gw:3 * gw])
        _append_rows(eg_ref, glu, CONV_HALO + r0)
        conv_cols = []
        for cb in range(nslab):
            conv_chunks = []
            for rr in range(r0, r0 + sb, CONV_ROWS):
                acc = None
                for k in range(CONV_WIDTH):
                    off = CONV_HALO - (CONV_WIDTH - 1) + k + rr
                    term = (eg_ref[cb, off:off + CONV_ROWS, :]
                            * cdw_ref[0, k:k + 1, cb * LANES:(cb + 1) * LANES])
                    acc = term if acc is None else acc + term
                conv_chunks.append(acc)
            conv_cols.append(jnp.concatenate(conv_chunks, axis=0))
        hb = jnp.concatenate(conv_cols, axis=1) + cdb_ref[0]
        hb = jax.nn.silu(_layer_norm_rows(hb, clg_ref[0], clb_ref[0]))
        y_b = jnp.dot(hb.astype(BF16), cpw_ref[0], preferred_element_type=F32) + cpb_ref[0]

        u = p[:, 3 * gw:4 * gw]
        vln = _layer_norm_rows(p[:, 4 * gw:5 * gw], slg_ref[0], slb_ref[0]).astype(BF16)
        yc_chunks = []
        for n in range(sb // SG_CHUNK):
            vch = vln[n * SG_CHUNK:(n + 1) * SG_CHUNK, :]
            vbd = jnp.concatenate([jnp.where(lane_g == hh, vch, zero_bf) for hh in range(SG_HEADS)],
                                  axis=0)
            mixed = jnp.dot(wcat, vbd, preferred_element_type=F32) + sgb_ref[0]
            yc_chunks.append(u[n * SG_CHUNK:(n + 1) * SG_CHUNK, :] * mixed)
        y_c = jnp.concatenate(yc_chunks, axis=0)

        cx = p[:, 6 * gw:7 * gw] * p[:, 7 * gw:8 * gw]
        _append_rows(ed_ref, cx, SC_HALO + r0)
        cd_cols = []
        for cb in range(nslab):
            ls = slice(cb * LANES, (cb + 1) * LANES)
            at = SC_HALO + r0
            cd_cols.append(cx[:, ls] * scw_ref[0, 2:3, ls]
                           + ed_ref[cb, at - 1:at - 1 + sb, :] * scw_ref[0, 1:2, ls]
                           + ed_ref[cb, at - 2:at - 2 + sb, :] * scw_ref[0, 0:1, ls])
        y_d = p[:, 5 * gw:6 * gw] * jnp.concatenate(cd_cols, axis=1)

        normed = []
        for gi, yg in enumerate((y_a, y_b, y_c, y_d)):
            rg = lax.rsqrt(jnp.mean(yg * yg, axis=-1, keepdims=True) + EPS)
            normed.append((yg * rg * ong_ref[0, :, gi * gw:(gi + 1) * gw]).astype(BF16))
        yn = jnp.concatenate(normed, axis=1)
        x1 = x + g1 * jnp.dot(yn, wout_ref[0], preferred_element_type=F32)
        x1_ref[r0:r0 + sb, :] = x1

        r2 = lax.rsqrt(jnp.mean(x1 * x1, axis=-1, keepdims=True) + EPS)
        h2 = x1 * r2 * gain2 + sh2
        for c in range(REC):
            rec_ref[pl.ds(r0 * REC + c, sb, stride=REC), :] = h2[:, c * LANES:(c + 1) * LANES]
        logits = lax.dot_general(rwt_ref[...], h2.astype(BF16), (((1,), (1,)), ((), ())),
                                 preferred_element_type=F32)
        sel = jax.nn.sigmoid(logits) + rb_ref[...]
        epg = EXPERTS_PER_GROUP
        sel_s = [sel[jj * N_EXPERT_GROUPS:(jj + 1) * N_EXPERT_GROUPS, :] for jj in range(epg)]
        top2 = None
        for ja in range(epg):
            for jb in range(ja + 1, epg):
                pair = sel_s[ja] + sel_s[jb]
                top2 = pair if top2 is None else jnp.maximum(top2, pair)
        gidx, _ = _first_argmax([top2[g:g + 1, :] for g in range(N_EXPERT_GROUPS)])
        sel_c = [_pick(gidx, [sel_s[jj][g:g + 1, :] for g in range(N_EXPERT_GROUPS)])
                 for jj in range(epg)]
        i1, _ = _first_argmax(sel_c)
        i2, _ = _first_argmax([jnp.where(i1 == float(jj), -jnp.inf, sel_c[jj]) for jj in range(epg)])
        lo = jnp.minimum(i1, i2)
        hi = jnp.maximum(i1, i2)
        base = jnp.where(lo == 0.0, 0.0, jnp.where(lo == 1.0, 3.0, 5.0))
        cls = gidx * float(PAIRS_PER_GROUP) + base + (hi - lo - 1.0)
        cls_ref[0, :, r0:r0 + sb] = cls.astype(I32)


def _mixer_layer(l, x, ys_prev, pos_prev, mods, lw, ts, sb, batch):
    t, d = x.shape
    nj = t // batch // ts
    has_prev = ys_prev is not None
    tile = lambda bi, j, *_: (bi * nj + j, 0)

    args = [x]
    in_specs = [pl.BlockSpec((ts, d), tile)]
    if has_prev:
        args += [ys_prev, mods]
        in_specs += [pl.BlockSpec(memory_space=pl.ANY),
                     pl.BlockSpec((1, 1, N_MOD, d), lambda bi, j, *_: (l - 1, bi, 0, 0))]
    args += [mods]
    in_specs += [pl.BlockSpec((1, 1, N_MOD, d), lambda bi, j, *_: (l, bi, 0, 0))]
    for name in ("n1g", "n2g", "w_in", "pool_w", "pool_scale", "conv_dw", "conv_db", "conv_ln_g",
                 "conv_ln_b", "conv_pw", "conv_pb", "sg_ln_g", "sg_ln_b", "sg_w", "sg_b", "sc_w",
                 "out_norm_g", "w_out"):
        arr = lw[name]
        args.append(arr)
        in_specs.append(pl.BlockSpec((1,) + arr.shape[1:],
                                     lambda bi, j, *_, nd=arr.ndim: (l,) + (0,) * (nd - 1)))
    for name in ("router_wt", "router_b"):
        arr = lw[name]
        args.append(arr)
        in_specs.append(pl.BlockSpec(arr.shape, lambda bi, j, *_: (0, 0)))

    out_shape = (jax.ShapeDtypeStruct((t, d), F32),
                 jax.ShapeDtypeStruct((t * REC, LANES), F32),
                 jax.ShapeDtypeStruct((t // ts, 1, ts), I32))
    out_specs = (pl.BlockSpec((ts, d), tile),
                 pl.BlockSpec((ts * REC, LANES), tile),
                 pl.BlockSpec((1, 1, ts), lambda bi, j, *_: (bi * nj + j, 0, 0)))
    nslab = GROUP_W // LANES
    scratch = [pltpu.VMEM((nslab, POOL_HALO + ts, LANES), F32),
               pltpu.VMEM((nslab, CONV_HALO + ts, LANES), F32),
               pltpu.VMEM((nslab, SC_HALO + ts, LANES), F32)]
    if has_prev:
        scratch += [pltpu.VMEM((2, ts * REC, LANES), F32), pltpu.SemaphoreType.DMA((2,))]
        args = [pos_prev] + args
    grid_spec = pltpu.PrefetchScalarGridSpec(
        num_scalar_prefetch=1 if has_prev else 0,
        grid=(batch, nj),
        in_specs=in_specs,
        out_specs=out_specs,
        scratch_shapes=scratch,
    )
    return pl.pallas_call(
        functools.partial(_mixer_body, has_prev, ts, sb),
        grid_spec=grid_spec,
        out_shape=out_shape,
        compiler_params=pltpu.CompilerParams(
            dimension_semantics=("arbitrary", "arbitrary"), vmem_limit_bytes=VMEM_LIMIT),
        name="mixer_layer",
    )(*args)


def _plan_body(tm, cls_ref, pos_ref, meta_ref):
    cls = cls_ref[...]
    nrow = cls.shape[0]
    ii = lax.broadcasted_iota(I32, (LANES, LANES), 0)
    jj = lax.broadcasted_iota(I32, (LANES, LANES), 1)
    upper = jnp.where(ii <= jj, 1.0, 0.0).astype(BF16)
    ri = lax.broadcasted_iota(I32, (nrow, nrow), 0)
    rj = lax.broadcasted_iota(I32, (nrow, nrow), 1)
    lower = jnp.where(rj < ri, 1.0, 0.0).astype(BF16)
    lane_i = lax.broadcasted_iota(I32, (1, LANES), 1)
    tile_i = lane_i.astype(F32)
    start = jnp.zeros((1, 1), F32)
    cum_tiles = jnp.zeros((1, 1), F32)
    pos = jnp.zeros((nrow, LANES), F32)
    tclass = jnp.zeros((1, LANES), F32)
    seg_end = jnp.zeros((1, LANES), F32)
    for k in range(N_CLASSES):
        hit = cls == k
        oh = jnp.where(hit, 1.0, 0.0)
        ohb = oh.astype(BF16)
        incl = jnp.dot(ohb, upper, preferred_element_type=F32)
        above = jnp.sum(jnp.dot(lower, ohb, preferred_element_type=F32), axis=1, keepdims=True)
        rank = incl - oh + above
        pos = pos + jnp.where(hit, start + rank, 0.0)
        cnt = jnp.sum(jnp.sum(oh, axis=1, keepdims=True), axis=0, keepdims=True)
        n_tiles = jnp.floor((cnt + float(tm - 1)) * (1.0 / tm))
        start = start + n_tiles * float(tm)
        cum_tiles = cum_tiles + n_tiles
        tclass = tclass + jnp.where(tile_i >= cum_tiles, 1.0, 0.0)
        seg_end = jnp.where(lane_i == k, start, seg_end)
    pos_ref[...] = pos.astype(I32)

    valid = tclass < float(N_CLASSES)
    kc = jnp.minimum(tclass, float(N_CLASSES - 1))
    grp = jnp.zeros_like(kc)
    for g in range(1, N_EXPERT_GROUPS):
        grp = grp + jnp.where(kc >= float(g * PAIRS_PER_GROUP), 1.0, 0.0)
    pr = kc - grp * float(PAIRS_PER_GROUP)
    lo = jnp.where(pr >= 3.0, 1.0, 0.0) + jnp.where(pr >= 5.0, 1.0, 0.0)
    base = jnp.where(lo == 0.0, 0.0, jnp.where(lo == 1.0, 3.0, 5.0))
    hi = pr - base + lo + 1.0
    rows = {META_EA: grp * float(EXPERTS_PER_GROUP) + lo,
            META_EB: grp * float(EXPERTS_PER_GROUP) + hi,
            META_VALID: jnp.where(valid, 1.0, 0.0),
            META_BLK: jnp.minimum(tile_i, cum_tiles - 1.0),
            META_END: seg_end}
    rid = lax.broadcasted_iota(I32, (SUBLANES, LANES), 0)
    meta = jnp.zeros((SUBLANES, LANES), F32)
    for k, row in rows.items():
        meta = jnp.where(rid == k, row, meta)
    meta_ref[...] = meta.astype(I32)


def _route_plan(cls2d, tm):
    nrow = cls2d.shape[0]
    return pl.pallas_call(
        functools.partial(_plan_body, tm),
        in_specs=[pl.BlockSpec((nrow, LANES), lambda: (0, 0))],
        out_specs=(pl.BlockSpec((nrow, LANES), lambda: (0, 0)),
                   pl.BlockSpec((SUBLANES, LANES), lambda: (0, 0))),
        out_shape=(jax.ShapeDtypeStruct((nrow, LANES), I32),
                   jax.ShapeDtypeStruct((SUBLANES, LANES), I32)),
        name="route_plan",
    )(cls2d)


def _dispatch_body(tm, nt, pos_ref, end_ref, valid_ref, src_ref, dst_hbm, zbuf, zsem, sem):
    g = pl.program_id(0)
    n = PERMUTE_ROWS

    @pl.when(g == 0)
    def _():
        zbuf[...] = jnp.zeros_like(zbuf)

        def zero_tile(first_slot, wait):
            cp = pltpu.make_async_copy(
                zbuf, dst_hbm.at[pl.ds(pl.multiple_of(first_slot * REC, REC), tm * REC), :], zsem)
            cp.wait() if wait else cp.start()

        for wait in (False, True):
            for k in range(N_CLASSES):
                prev_end = end_ref[k - 1] if k > 0 else 0

                @pl.when(end_ref[k] > prev_end)
                def _():
                    zero_tile(end_ref[k] - tm, wait)

            for i in range(nt):
                @pl.when(valid_ref[i] == 0)
                def _():
                    zero_tile(i * tm, wait)

    base = g * n
    for r0 in range(0, n, PERMUTE_GROUP):
        slots = [pos_ref[base + r0 + k] for k in range(PERMUTE_GROUP)]
        for k, slot in enumerate(slots):
            pltpu.make_async_copy(
                src_ref.at[pl.ds((r0 + k) * REC, REC), :],
                dst_hbm.at[pl.ds(pl.multiple_of(slot * REC, REC), REC), :],
                sem).start(priority=k % 2)

    pltpu.make_async_copy(src_ref, dst_hbm.at[pl.ds(0, n * REC), :], sem).wait()


def _row_dispatch(rec, pos, meta, tm, nt):
    n_tok = pos.shape[0]
    grid_spec = pltpu.PrefetchScalarGridSpec(
        num_scalar_prefetch=3,
        grid=(n_tok // PERMUTE_ROWS,),
        in_specs=[pl.BlockSpec((PERMUTE_ROWS * REC, LANES), lambda g, pos, end, valid: (g, 0))],
        out_specs=pl.BlockSpec(memory_space=pl.ANY),
        scratch_shapes=[pltpu.VMEM((tm * REC, LANES), rec.dtype), pltpu.SemaphoreType.DMA(()),
                        pltpu.SemaphoreType.DMA(())],
    )
    return pl.pallas_call(
        functools.partial(_dispatch_body, tm, nt),
        grid_spec=grid_spec,
        out_shape=jax.ShapeDtypeStruct((nt * tm * REC, LANES), rec.dtype),
        compiler_params=pltpu.CompilerParams(
            dimension_semantics=("arbitrary",), vmem_limit_bytes=VMEM_LIMIT),
        name="row_dispatch",
    )(pos, meta[META_END, :N_CLASSES], meta[META_VALID, :nt], rec)


def _moe_body(tm, ea_ref, eb_ref, valid_ref, blk_ref, h_ref, rwa_ref, rwb_ref,
              wga_ref, wgb_ref, wua_ref, wub_ref, wda_ref, wdb_ref, y_ref):
    i = pl.program_id(0)

    @pl.when(valid_ref[i] == 1)
    def _():
        xb = jnp.concatenate([_rows_of(h_ref, c, tm) for c in range(REC)], axis=1).astype(BF16)
        xf = xb.astype(F32)
        sa = jax.nn.sigmoid(jnp.sum(xf * rwa_ref[0].astype(F32), axis=-1, keepdims=True))
        sb = jax.nn.sigmoid(jnp.sum(xf * rwb_ref[0].astype(F32), axis=-1, keepdims=True))
        wa = sa / (sa + sb)
        wb = sb / (sa + sb)

        def expert(wg_ref, wu_ref, wd_ref, w):
            hg = jnp.dot(xb, wg_ref[0, 0], preferred_element_type=F32)
            hu = jnp.dot(xb, wu_ref[0, 0], preferred_element_type=F32)
            act = jax.nn.silu(hg) * hu * w
            return jnp.dot(act.astype(BF16), wd_ref[0, 0], preferred_element_type=F32)

        y = expert(wga_ref, wua_ref, wda_ref, wa) + expert(wgb_ref, wub_ref, wdb_ref, wb)
        for c in range(REC):
            y_ref[pl.ds(c, tm, stride=REC), :] = y[:, c * LANES:(c + 1) * LANES]

    @pl.when(valid_ref[i] == 0)
    def _():
        y_ref[...] = jnp.zeros_like(y_ref)


def _moe_tiles(l, hs, meta, router_rows, wg, wu, wd, tm, nt):
    d = D_MODEL
    rmap_a = lambda i, ea, eb, valid, blk: (ea[i], 0, 0)
    rmap_b = lambda i, ea, eb, valid, blk: (eb[i], 0, 0)
    wmap_a = lambda i, ea, eb, valid, blk: (l, ea[i], 0, 0)
    wmap_b = lambda i, ea, eb, valid, blk: (l, eb[i], 0, 0)
    grid_spec = pltpu.PrefetchScalarGridSpec(
        num_scalar_prefetch=4,
        grid=(nt,),
        in_specs=[
            pl.BlockSpec((tm * REC, LANES), lambda i, ea, eb, valid, blk: (blk[i], 0)),
            pl.BlockSpec((1, 1, d), rmap_a),
            pl.BlockSpec((1, 1, d), rmap_b),
            pl.BlockSpec((1, 1, d, D_EXPERT), wmap_a),
            pl.BlockSpec((1, 1, d, D_EXPERT), wmap_b),
            pl.BlockSpec((1, 1, d, D_EXPERT), wmap_a),
            pl.BlockSpec((1, 1, d, D_EXPERT), wmap_b),
            pl.BlockSpec((1, 1, D_EXPERT, d), wmap_a),
            pl.BlockSpec((1, 1, D_EXPERT, d), wmap_b),
        ],
        out_specs=pl.BlockSpec((tm * REC, LANES), lambda i, ea, eb, valid, blk: (i, 0)),
    )
    return pl.pallas_call(
        functools.partial(_moe_body, tm),
        grid_spec=grid_spec,
        out_shape=jax.ShapeDtypeStruct((nt * tm * REC, LANES), F32),
        compiler_params=pltpu.CompilerParams(
            dimension_semantics=("arbitrary",), vmem_limit_bytes=VMEM_LIMIT),
        name="moe_tiles",
    )(meta[META_EA, :nt], meta[META_EB, :nt], meta[META_VALID, :nt], meta[META_BLK, :nt],
      hs, router_rows, router_rows, wg, wg, wu, wu, wd, wd)


def _final_body(ts, pos_ref, x_ref, ys_hbm, mod_ref, g_ref, o_ref, ybuf, gsem):
    y_ref = _gather_moe_rows(pos_ref, ys_hbm, ybuf, gsem, ts)
    y = jnp.concatenate([_rows_of(y_ref, c, ts) for c in range(REC)], axis=1)
    x = x_ref[...] + mod_ref[0, 0, 5:6, :] * y
    r = lax.rsqrt(jnp.mean(x * x, axis=-1, keepdims=True) + EPS)
    o_ref[...] = x * r * g_ref[...]


def _final_norm(x, ys, pos, mods, final_g, ts, batch):
    t, d = x.shape
    nj = t // batch // ts
    last = mods.shape[0] - 1
    tile = lambda bi, j, pos: (bi * nj + j, 0)
    grid_spec = pltpu.PrefetchScalarGridSpec(
        num_scalar_prefetch=1,
        grid=(batch, nj),
        in_specs=[pl.BlockSpec((ts, d), tile), pl.BlockSpec(memory_space=pl.ANY),
                  pl.BlockSpec((1, 1, N_MOD, d), lambda bi, j, pos: (last, bi, 0, 0)),
                  pl.BlockSpec((1, d), lambda bi, j, pos: (0, 0))],
        out_specs=pl.BlockSpec((ts, d), tile),
        scratch_shapes=[pltpu.VMEM((2, ts * REC, LANES), F32), pltpu.SemaphoreType.DMA((2,))],
    )
    return pl.pallas_call(
        functools.partial(_final_body, ts),
        grid_spec=grid_spec,
        out_shape=jax.ShapeDtypeStruct((t, d), F32),
        compiler_params=pltpu.CompilerParams(
            dimension_semantics=("arbitrary", "arbitrary"), vmem_limit_bytes=VMEM_LIMIT),
        name="final_norm",
    )(pos, x, ys, mods, final_g.reshape(1, d))


def _stacked_weights(norm1_g, norm2_g, w_in, pool_w, pool_scale, conv_dw, conv_db, conv_ln_g,
                     conv_ln_b, conv_pw, conv_pb, sg_ln_g, sg_ln_b, sg_w, sg_b, sc_w, out_norm_g, w_out):
    depth = w_in.shape[0]
    row = lambda v: v.reshape(depth, 1, -1)
    pool_bd = jnp.zeros((depth, GROUP_W, GROUP_W), pool_w.dtype)
    for i in range(len(POOL_WINDOWS)):
        pool_bd = pool_bd.at[:, i * POOL_CG:(i + 1) * POOL_CG, i * POOL_CG:(i + 1) * POOL_CG].set(
            pool_w[:, i])
    return {
        "n1g": row(norm1_g), "n2g": row(norm2_g),
        "w_in": w_in.astype(BF16),
        "pool_w": pool_bd.astype(BF16),
        "pool_scale": row(pool_scale),
        "conv_dw": jnp.pad(conv_dw, ((0, 0), (0, 1), (0, 0))),
        "conv_db": row(conv_db), "conv_ln_g": row(conv_ln_g), "conv_ln_b": row(conv_ln_b),
        "conv_pw": conv_pw.astype(BF16), "conv_pb": row(conv_pb),
        "sg_ln_g": row(sg_ln_g), "sg_ln_b": row(sg_ln_b),
        "sg_w": sg_w,
        "sg_b": jnp.repeat(jnp.swapaxes(sg_b, 1, 2), GROUP_W // SG_HEADS, axis=2),
        "sc_w": jnp.pad(sc_w, ((0, 0), (0, SUBLANES - SC_WIDTH), (0, 0))),
        "out_norm_g": row(out_norm_g),
        "w_out": w_out.astype(BF16),
    }


def kernel(x, c, norm1_g, norm2_g, w_ada, b_ada, w_in, pool_w, pool_scale, conv_dw, conv_db, conv_ln_g, conv_ln_b, conv_pw, conv_pb, sg_ln_g, sg_ln_b, sg_w, sg_b, sc_w, out_norm_g, w_out, router_w, router_bias, exp_w_gate, exp_w_up, exp_w_down, final_g):
    b, s, d = x.shape
    depth = w_in.shape[0]
    t = b * s
    ts = 512
    sb = 256
    tm = 256
    nt = t // tm + N_CLASSES
    assert d == D_MODEL and t % LANES == 0 and s % ts == 0 and t % tm == 0 and nt <= LANES
    assert t % PERMUTE_ROWS == 0 and ts % sb == 0 and sb % SG_CHUNK == 0

    mods = _ada_mod(c, w_ada, b_ada)

    perm = lambda v: v.reshape(N_EXPERT_GROUPS, EXPERTS_PER_GROUP, -1).transpose(1, 0, 2).reshape(
        N_EXPERTS, -1)
    lw = _stacked_weights(norm1_g, norm2_g, w_in, pool_w, pool_scale, conv_dw, conv_db, conv_ln_g,
                          conv_ln_b, conv_pw, conv_pb, sg_ln_g, sg_ln_b, sg_w, sg_b, sc_w, out_norm_g,
                          w_out)
    lw["router_wt"] = perm(router_w.T).astype(BF16)
    lw["router_b"] = perm(router_bias.reshape(N_EXPERTS, 1))
    router_rows = router_w.T.astype(BF16).reshape(N_EXPERTS, 1, d)
    wg = exp_w_gate.astype(BF16)
    wu = exp_w_up.astype(BF16)
    wd = exp_w_down.astype(BF16)

    xcur, ys, pos = x.reshape(t, d), None, None
    for l in range(depth):
        xcur, rec, cls = _mixer_layer(l, xcur, ys, pos, mods, lw, ts, sb, b)
        pos2d, meta = _route_plan(cls.reshape(t // LANES, LANES), tm)
        pos = pos2d.reshape(t)
        hs = _row_dispatch(rec, pos, meta, tm, nt)
        ys = _moe_tiles(l, hs, meta, router_rows, wg, wu, wd, tm, nt)
    return _final_norm(xcur, ys, pos, mods, final_g, ts, b).reshape(b, s, d)
```

```python
import functools

import jax
import jax.numpy as jnp
from jax import lax
from jax.experimental import pallas as pl
from jax.experimental.pallas import tpu as pltpu

D_MODEL = 1024
GROUP_W = 256
POOL_WINDOWS = (2, 4, 8, 16)
POOL_CG = 64
CONV_WIDTH = 31
SG_CHUNK = 128
SG_HEADS = 4
SC_WIDTH = 3
N_EXPERTS = 16
N_EXPERT_GROUPS = 4
EXPERTS_PER_GROUP = 4
PAIRS_PER_GROUP = 6
N_CLASSES = N_EXPERT_GROUPS * PAIRS_PER_GROUP
D_EXPERT = 256
N_MOD = 6
EPS = 1e-6
LN_EPS = 1e-5

LANES = 128
SUBLANES = 8
REC = SUBLANES
CONV_HALO = 32
POOL_HALO = 16
SC_HALO = 8
CONV_ROWS = 32
META_EA, META_EB, META_VALID, META_BLK, META_END = 0, 1, 2, 3, 4
PERMUTE_ROWS = 1024
PERMUTE_GROUP = 16
VMEM_LIMIT = 56 * 1024 * 1024

F32 = jnp.float32
BF16 = jnp.bfloat16
I32 = jnp.int32


def _rows_of(ref, c, n, first=0):
    return ref[pl.ds(first * REC + c, n, stride=REC), :]


def _ada_body(c_ref, w_ref, b_ref, o_ref):
    ca = jax.nn.silu(c_ref[...])
    o_ref[0] = jnp.dot(ca.astype(BF16), w_ref[0].astype(BF16), preferred_element_type=F32) + b_ref[0]


def _ada_mod(c, w_ada, b_ada):
    depth, d, n = w_ada.shape
    b = c.shape[0]
    tn = 1024
    cp = jnp.pad(c, ((0, SUBLANES - b), (0, 0)))
    out = pl.pallas_call(
        _ada_body,
        grid=(depth, n // tn),
        in_specs=[
            pl.BlockSpec((SUBLANES, d), lambda l, j: (0, 0)),
            pl.BlockSpec((1, d, tn), lambda l, j: (l, 0, j)),
            pl.BlockSpec((1, 1, tn), lambda l, j: (l, 0, j)),
        ],
        out_specs=pl.BlockSpec((1, SUBLANES, tn), lambda l, j: (l, 0, j)),
        out_shape=jax.ShapeDtypeStruct((depth, SUBLANES, n), F32),
        compiler_params=pltpu.CompilerParams(
            dimension_semantics=("arbitrary", "arbitrary"), vmem_limit_bytes=VMEM_LIMIT),
        name="ada_mod",
    )(cp, w_ada, b_ada.reshape(depth, 1, n))
    return out[:, :b].reshape(depth, b, N_MOD, d)


def _layer_norm_rows(v, g, b):
    mu = jnp.mean(v, axis=-1, keepdims=True)
    vc = v - mu
    var = jnp.mean(vc * vc, axis=-1, keepdims=True)
    return vc * lax.rsqrt(var + LN_EPS) * g + b


def _carry_tail(ext_ref, j, ts, halo):
    nslab = ext_ref.shape[0]

    @pl.when(j == 0)
    def _():
        ext_ref[:, 0:halo, :] = jnp.zeros((nslab, halo, LANES), F32)

    @pl.when(j > 0)
    def _():
        ext_ref[:, 0:halo, :] = ext_ref[:, ts:ts + halo, :]


def _append_rows(ext_ref, cur, first):
    for cb in range(ext_ref.shape[0]):
        ext_ref[cb, first:first + cur.shape[0], :] = cur[:, cb * LANES:(cb + 1) * LANES]


def _first_argmax(vals):
    best = vals[0]
    idx = jnp.zeros_like(best)
    for k in range(1, len(vals)):
        better = vals[k] > best
        idx = jnp.where(better, float(k), idx)
        best = jnp.where(better, vals[k], best)
    return idx, best


def _pick(idx, vals):
    out = vals[0]
    for k in range(1, len(vals)):
        out = jnp.where(idx == float(k), vals[k], out)
    return out


def _gather_moe_rows(pos_ref, ys_hbm, ybuf, gsem, ts):
    g = pl.program_id(0) * pl.num_programs(1) + pl.program_id(1)
    n_steps = pl.num_programs(0) * pl.num_programs(1)

    def start(step, buf):
        base = step * ts
        for r0 in range(0, ts, PERMUTE_GROUP):
            slots = [pos_ref[base + r0 + k] for k in range(PERMUTE_GROUP)]
            for k, slot in enumerate(slots):
                pltpu.make_async_copy(
                    ys_hbm.at[pl.ds(pl.multiple_of(slot * REC, REC), REC), :],
                    ybuf.at[buf, pl.ds((r0 + k) * REC, REC), :],
                    gsem.at[buf]).start(priority=k % 2)

    @pl.when(g == 0)
    def _():
        start(0, 0)

    @pl.when(g + 1 < n_steps)
    def _():
        start(g + 1, (g + 1) % 2)

    cur = g % 2
    pltpu.make_async_copy(ys_hbm.at[pl.ds(0, ts * REC), :], ybuf.at[cur], gsem.at[cur]).wait()
    return ybuf.at[cur]


def _mixer_body(has_prev, ts, sb, *refs):
    refs = list(refs)
    if has_prev:
        pos_ref = refs.pop(0)
    x_ref = refs.pop(0)
    if has_prev:
        ys_hbm = refs.pop(0)
        modp_ref = refs.pop(0)
        gsem = refs.pop()
        ybuf = refs.pop()
    (mod_ref, n1g_ref, n2g_ref, win_ref, poolw_ref, pools_ref,
     cdw_ref, cdb_ref, clg_ref, clb_ref, cpw_ref, cpb_ref,
     slg_ref, slb_ref, sgw_ref, sgb_ref, scw_ref, ong_ref, wout_ref, rwt_ref, rb_ref,
     x1_ref, rec_ref, cls_ref,
     ea_ref, eg_ref, ed_ref) = refs

    j = pl.program_id(1)
    gw = GROUP_W
    nslab = gw // LANES
    if has_prev:
        yprev_ref = _gather_moe_rows(pos_ref, ys_hbm, ybuf, gsem, ts)

    sh1 = mod_ref[0, 0, 0:1, :]
    sc1 = mod_ref[0, 0, 1:2, :]
    g1 = mod_ref[0, 0, 2:3, :]
    sh2 = mod_ref[0, 0, 3:4, :]
    sc2 = mod_ref[0, 0, 4:5, :]
    gain1 = n1g_ref[0] * (1.0 + sc1)
    gain2 = n2g_ref[0] * (1.0 + sc2)

    _carry_tail(ea_ref, j, ts, POOL_HALO)
    _carry_tail(eg_ref, j, ts, CONV_HALO)
    _carry_tail(ed_ref, j, ts, SC_HALO)

    rowi = lax.broadcasted_iota(I32, (SG_CHUNK, SG_CHUNK), 0)
    coli = lax.broadcasted_iota(I32, (SG_CHUNK, SG_CHUNK), 1)
    tril = coli <= rowi
    wcat = jnp.concatenate(
        [jnp.where(tril, sgw_ref[0, hh], 0.0) for hh in range(SG_HEADS)], axis=1).astype(BF16)
    lane_g = lax.broadcasted_iota(I32, (SG_CHUNK, gw), 1) // (gw // SG_HEADS)
    zero_bf = jnp.zeros((SG_CHUNK, gw), BF16)
    lane = lax.broadcasted_iota(I32, (sb, LANES), 1)
    first = lane < POOL_CG

    for r0 in range(0, ts, sb):
        x = x_ref[r0:r0 + sb, :]
        if has_prev:
            yprev = jnp.concatenate([_rows_of(yprev_ref, c, sb, r0) for c in range(REC)], axis=1)
            x = x + modp_ref[0, 0, 5:6, :] * yprev

        r1 = lax.rsqrt(jnp.mean(x * x, axis=-1, keepdims=True) + EPS)
        h = x * r1 * gain1 + sh1
        p = jnp.dot(h.astype(BF16), win_ref[0], preferred_element_type=F32)

        a = p[:, 0:gw]
        _append_rows(ea_ref, a, POOL_HALO + r0)

        def a_shift(s, cb):
            return ea_ref[cb, POOL_HALO + r0 - s:POOL_HALO + r0 - s + sb, :]

        pos1 = (lax.broadcasted_iota(I32, (sb, LANES), 0) + (j * ts + r0 + 1)).astype(F32)
        a_lo = a[:, 0:LANES]
        s01 = a_lo + a_shift(1, 0)
        s03 = s01 + (a_shift(2, 0) + a_shift(3, 0))
        num_lo = jnp.where(first, s01, s03)
        den_lo = jnp.minimum(pos1, jnp.where(first, float(POOL_WINDOWS[0]), float(POOL_WINDOWS[1])))
        a_hi = a[:, LANES:gw]
        s07 = a_hi
        for s in range(1, 8):
            s07 = s07 + a_shift(s, 1)
        s815 = a_shift(8, 1)
        for s in range(9, 16):
            s815 = s815 + a_shift(s, 1)
        num_hi = jnp.where(first, s07, s07 + s815)
        den_hi = jnp.minimum(pos1, jnp.where(first, float(POOL_WINDOWS[2]), float(POOL_WINDOWS[3])))
        d_pool = jnp.concatenate([num_lo / den_lo - a_lo, num_hi / den_hi - a_hi], axis=1)
        y_a = jnp.dot(d_pool.astype(BF16), poolw_ref[0], preferred_element_type=F32) * pools_ref[0]

        glu = p[:, gw:2 * gw] * jax.nn.sigmoid(p[:, 2 * gw:3 * gw])
        _append_rows(eg_ref, glu, CONV_HALO + r0)
        conv_cols = []
        for cb in range(nslab):
            conv_chunks = []
            for rr in range(r0, r0 + sb, CONV_ROWS):
                acc = None
                for k in range(CONV_WIDTH):
                    off = CONV_HALO - (CONV_WIDTH - 1) + k + rr
                    term = (eg_ref[cb, off:off + CONV_ROWS, :]
                            * cdw_ref[0, k:k + 1, cb * LANES:(cb + 1) * LANES])
                    acc = term if acc is None else acc + term
                conv_chunks.append(acc)
            conv_cols.append(jnp.concatenate(conv_chunks, axis=0))
        hb = jnp.concatenate(conv_cols, axis=1) + cdb_ref[0]
        hb = jax.nn.silu(_layer_norm_rows(hb, clg_ref[0], clb_ref[0]))
        y_b = jnp.dot(hb.astype(BF16), cpw_ref[0], preferred_element_type=F32) + cpb_ref[0]

        u = p[:, 3 * gw:4 * gw]
        vln = _layer_norm_rows(p[:, 4 * gw:5 * gw], slg_ref[0], slb_ref[0]).astype(BF16)
        yc_chunks = []
        for n in range(sb // SG_CHUNK):
            vch = vln[n * SG_CHUNK:(n + 1) * SG_CHUNK, :]
            vbd = jnp.concatenate([jnp.where(lane_g == hh, vch, zero_bf) for hh in range(SG_HEADS)],
                                  axis=0)
            mixed = jnp.dot(wcat, vbd, preferred_element_type=F32) + sgb_ref[0]
            yc_chunks.append(u[n * SG_CHUNK:(n + 1) * SG_CHUNK, :] * mixed)
        y_c = jnp.concatenate(yc_chunks, axis=0)

        cx = p[:, 6 * gw:7 * gw] * p[:, 7 * gw:8 * gw]
        _append_rows(ed_ref, cx, SC_HALO + r0)
        cd_cols = []
        for cb in range(nslab):
            ls = slice(cb * LANES, (cb + 1) * LANES)
            at = SC_HALO + r0
            cd_cols.append(cx[:, ls] * scw_ref[0, 2:3, ls]
                           + ed_ref[cb, at - 1:at - 1 + sb, :] * scw_ref[0, 1:2, ls]
                           + ed_ref[cb, at - 2:at - 2 + sb, :] * scw_ref[0, 0:1, ls])
        y_d = p[:, 5 * gw:6 * gw] * jnp.concatenate(cd_cols, axis=1)

        normed = []
        for gi, yg in enumerate((y_a, y_b, y_c, y_d)):
            rg = lax.rsqrt(jnp.mean(yg * yg, axis=-1, keepdims=True) + EPS)
            normed.append((yg * rg * ong_ref[0, :, gi * gw:(gi + 1) * gw]).astype(BF16))
        yn = jnp.concatenate(normed, axis=1)
        x1 = x + g1 * jnp.dot(yn, wout_ref[0], preferred_element_type=F32)
        x1_ref[r0:r0 + sb, :] = x1

        r2 = lax.rsqrt(jnp.mean(x1 * x1, axis=-1, keepdims=True) + EPS)
        h2 = x1 * r2 * gain2 + sh2
        for c in range(REC):
            rec_ref[pl.ds(r0 * REC + c, sb, stride=REC), :] = h2[:, c * LANES:(c + 1) * LANES]
        logits = lax.dot_general(rwt_ref[...], h2.astype(BF16), (((1,), (1,)), ((), ())),
                                 preferred_element_type=F32)
        sel = jax.nn.sigmoid(logits) + rb_ref[...]
        epg = EXPERTS_PER_GROUP
        sel_s = [sel[jj * N_EXPERT_GROUPS:(jj + 1) * N_EXPERT_GROUPS, :] for jj in range(epg)]
        top2 = None
        for ja in range(epg):
            for jb in range(ja + 1, epg):
                pair = sel_s[ja] + sel_s[jb]
                top2 = pair if top2 is None else jnp.maximum(top2, pair)
        gidx, _ = _first_argmax([top2[g:g + 1, :] for g in range(N_EXPERT_GROUPS)])
        sel_c = [_pick(gidx, [sel_s[jj][g:g + 1, :] for g in range(N_EXPERT_GROUPS)])
                 for jj in range(epg)]
        i1, _ = _first_argmax(sel_c)
        i2, _ = _first_argmax([jnp.where(i1 == float(jj), -jnp.inf, sel_c[jj]) for jj in range(epg)])
        lo = jnp.minimum(i1, i2)
        hi = jnp.maximum(i1, i2)
        base = jnp.where(lo == 0.0, 0.0, jnp.where(lo == 1.0, 3.0, 5.0))
        cls = gidx * float(PAIRS_PER_GROUP) + base + (hi - lo - 1.0)
        cls_ref[0, :, r0:r0 + sb] = cls.astype(I32)


def _mixer_layer(l, x, ys_prev, pos_prev, mods, lw, ts, sb, batch):
    t, d = x.shape
    nj = t // batch // ts
    has_prev = ys_prev is not None
    tile = lambda bi, j, *_: (bi * nj + j, 0)

    args = [x]
    in_specs = [pl.BlockSpec((ts, d), tile)]
    if has_prev:
        args += [ys_prev, mods]
        in_specs += [pl.BlockSpec(memory_space=pl.ANY),
                     pl.BlockSpec((1, 1, N_MOD, d), lambda bi, j, *_: (l - 1, bi, 0, 0))]
    args += [mods]
    in_specs += [pl.BlockSpec((1, 1, N_MOD, d), lambda bi, j, *_: (l, bi, 0, 0))]
    for name in ("n1g", "n2g", "w_in", "pool_w", "pool_scale", "conv_dw", "conv_db", "conv_ln_g",
                 "conv_ln_b", "conv_pw", "conv_pb", "sg_ln_g", "sg_ln_b", "sg_w", "sg_b", "sc_w",
                 "out_norm_g", "w_out"):
        arr = lw[name]
        args.append(arr)
        in_specs.append(pl.BlockSpec((1,) + arr.shape[1:],
                                     lambda bi, j, *_, nd=arr.ndim: (l,) + (0,) * (nd - 1)))
    for name in ("router_wt", "router_b"):
        arr = lw[name]
        args.append(arr)
        in_specs.append(pl.BlockSpec(arr.shape, lambda bi, j, *_: (0, 0)))

    out_shape = (jax.ShapeDtypeStruct((t, d), F32),
                 jax.ShapeDtypeStruct((t * REC, LANES), F32),
                 jax.ShapeDtypeStruct((t // ts, 1, ts), I32))
    out_specs = (pl.BlockSpec((ts, d), tile),
                 pl.BlockSpec((ts * REC, LANES), tile),
                 pl.BlockSpec((1, 1, ts), lambda bi, j, *_: (bi * nj + j, 0, 0)))
    nslab = GROUP_W // LANES
    scratch = [pltpu.VMEM((nslab, POOL_HALO + ts, LANES), F32),
               pltpu.VMEM((nslab, CONV_HALO + ts, LANES), F32),
               pltpu.VMEM((nslab, SC_HALO + ts, LANES), F32)]
    if has_prev:
        scratch += [pltpu.VMEM((2, ts * REC, LANES), F32), pltpu.SemaphoreType.DMA((2,))]
        args = [pos_prev] + args
    grid_spec = pltpu.PrefetchScalarGridSpec(
        num_scalar_prefetch=1 if has_prev else 0,
        grid=(batch, nj),
        in_specs=in_specs,
        out_specs=out_specs,
        scratch_shapes=scratch,
    )
    return pl.pallas_call(
        functools.partial(_mixer_body, has_prev, ts, sb),
        grid_spec=grid_spec,
        out_shape=out_shape,
        compiler_params=pltpu.CompilerParams(
            dimension_semantics=("arbitrary", "arbitrary"), vmem_limit_bytes=VMEM_LIMIT),
        name="mixer_layer",
    )(*args)


def _plan_body(tm, cls_ref, pos_ref, meta_ref):
    cls = cls_ref[...]
    nrow = cls.shape[0]
    ii = lax.broadcasted_iota(I32, (LANES, LANES), 0)
    jj = lax.broadcasted_iota(I32, (LANES, LANES), 1)
    upper = jnp.where(ii <= jj, 1.0, 0.0).astype(BF16)
    ri = lax.broadcasted_iota(I32, (nrow, nrow), 0)
    rj = lax.broadcasted_iota(I32, (nrow, nrow), 1)
    lower = jnp.where(rj < ri, 1.0, 0.0).astype(BF16)
    lane_i = lax.broadcasted_iota(I32, (1, LANES), 1)
    tile_i = lane_i.astype(F32)
    start = jnp.zeros((1, 1), F32)
    cum_tiles = jnp.zeros((1, 1), F32)
    pos = jnp.zeros((nrow, LANES), F32)
    tclass = jnp.zeros((1, LANES), F32)
    seg_end = jnp.zeros((1, LANES), F32)
    for k in range(N_CLASSES):
        hit = cls == k
        oh = jnp.where(hit, 1.0, 0.0)
        ohb = oh.astype(BF16)
        incl = jnp.dot(ohb, upper, preferred_element_type=F32)
        above = jnp.sum(jnp.dot(lower, ohb, preferred_element_type=F32), axis=1, keepdims=True)
        rank = incl - oh + above
        pos = pos + jnp.where(hit, start + rank, 0.0)
        cnt = jnp.sum(jnp.sum(oh, axis=1, keepdims=True), axis=0, keepdims=True)
        n_tiles = jnp.floor((cnt + float(tm - 1)) * (1.0 / tm))
        start = start + n_tiles * float(tm)
        cum_tiles = cum_tiles + n_tiles
        tclass = tclass + jnp.where(tile_i >= cum_tiles, 1.0, 0.0)
        seg_end = jnp.where(lane_i == k, start, seg_end)
    pos_ref[...] = pos.astype(I32)

    valid = tclass < float(N_CLASSES)
    kc = jnp.minimum(tclass, float(N_CLASSES - 1))
    grp = jnp.zeros_like(kc)
    for g in range(1, N_EXPERT_GROUPS):
        grp = grp + jnp.where(kc >= float(g * PAIRS_PER_GROUP), 1.0, 0.0)
    pr = kc - grp * float(PAIRS_PER_GROUP)
    lo = jnp.where(pr >= 3.0, 1.0, 0.0) + jnp.where(pr >= 5.0, 1.0, 0.0)
    base = jnp.where(lo == 0.0, 0.0, jnp.where(lo == 1.0, 3.0, 5.0))
    hi = pr - base + lo + 1.0
    rows = {META_EA: grp * float(EXPERTS_PER_GROUP) + lo,
            META_EB: grp * float(EXPERTS_PER_GROUP) + hi,
            META_VALID: jnp.where(valid, 1.0, 0.0),
            META_BLK: jnp.minimum(tile_i, cum_tiles - 1.0),
            META_END: seg_end}
    rid = lax.broadcasted_iota(I32, (SUBLANES, LANES), 0)
    meta = jnp.zeros((SUBLANES, LANES), F32)
    for k, row in rows.items():
        meta = jnp.where(rid == k, row, meta)
    meta_ref[...] = meta.astype(I32)


def _route_plan(cls2d, tm):
    nrow = cls2d.shape[0]
    return pl.pallas_call(
        functools.partial(_plan_body, tm),
        in_specs=[pl.BlockSpec((nrow, LANES), lambda: (0, 0))],
        out_specs=(pl.BlockSpec((nrow, LANES), lambda: (0, 0)),
                   pl.BlockSpec((SUBLANES, LANES), lambda: (0, 0))),
        out_shape=(jax.ShapeDtypeStruct((nrow, LANES), I32),
                   jax.ShapeDtypeStruct((SUBLANES, LANES), I32)),
        name="route_plan",
    )(cls2d)


def _dispatch_body(tm, nt, pos_ref, end_ref, valid_ref, src_ref, dst_hbm, zbuf, zsem, sem):
    g = pl.program_id(0)
    n = PERMUTE_ROWS

    @pl.when(g == 0)
    def _():
        zbuf[...] = jnp.zeros_like(zbuf)

        def zero_tile(first_slot, wait):
            cp = pltpu.make_async_copy(
                zbuf, dst_hbm.at[pl.ds(pl.multiple_of(first_slot * REC, REC), tm * REC), :], zsem)
            cp.wait() if wait else cp.start()

        for wait in (False, True):
            for k in range(N_CLASSES):
                prev_end = end_ref[k - 1] if k > 0 else 0

                @pl.when(end_ref[k] > prev_end)
                def _():
                    zero_tile(end_ref[k] - tm, wait)

            for i in range(nt):
                @pl.when(valid_ref[i] == 0)
                def _():
                    zero_tile(i * tm, wait)

    base = g * n
    for r0 in range(0, n, PERMUTE_GROUP):
        slots = [pos_ref[base + r0 + k] for k in range(PERMUTE_GROUP)]
        for k, slot in enumerate(slots):
            pltpu.make_async_copy(
                src_ref.at[pl.ds((r0 + k) * REC, REC), :],
                dst_hbm.at[pl.ds(pl.multiple_of(slot * REC, REC), REC), :],
                sem).start(priority=k % 2)

    pltpu.make_async_copy(src_ref, dst_hbm.at[pl.ds(0, n * REC), :], sem).wait()


def _row_dispatch(rec, pos, meta, tm, nt):
    n_tok = pos.shape[0]
    grid_spec = pltpu.PrefetchScalarGridSpec(
        num_scalar_prefetch=3,
        grid=(n_tok // PERMUTE_ROWS,),
        in_specs=[pl.BlockSpec((PERMUTE_ROWS * REC, LANES), lambda g, pos, end, valid: (g, 0))],
        out_specs=pl.BlockSpec(memory_space=pl.ANY),
        scratch_shapes=[pltpu.VMEM((tm * REC, LANES), rec.dtype), pltpu.SemaphoreType.DMA(()),
                        pltpu.SemaphoreType.DMA(())],
    )
    return pl.pallas_call(
        functools.partial(_dispatch_body, tm, nt),
        grid_spec=grid_spec,
        out_shape=jax.ShapeDtypeStruct((nt * tm * REC, LANES), rec.dtype),
        compiler_params=pltpu.CompilerParams(
            dimension_semantics=("arbitrary",), vmem_limit_bytes=VMEM_LIMIT),
        name="row_dispatch",
    )(pos, meta[META_END, :N_CLASSES], meta[META_VALID, :nt], rec)


def _moe_body(tm, ea_ref, eb_ref, valid_ref, blk_ref, h_ref, rw_ref, wg_ref, wu_ref, wd_ref, y_ref):
    i = pl.program_id(0)

    @pl.when(valid_ref[i] == 1)
    def _():
        ea = ea_ref[i]
        eb = eb_ref[i]
        xb = jnp.concatenate([_rows_of(h_ref, c, tm) for c in range(REC)], axis=1).astype(BF16)
        xf = xb.astype(F32)

        def expert(e):
            score = jax.nn.sigmoid(jnp.sum(xf * rw_ref[e].astype(F32), axis=-1, keepdims=True))
            hg = jnp.dot(xb, wg_ref[0, e], preferred_element_type=F32)
            hu = jnp.dot(xb, wu_ref[0, e], preferred_element_type=F32)
            return score, jax.nn.silu(hg) * hu

        sa, ga = expert(ea)
        sb, gb = expert(eb)
        acta = ga * (sa / (sa + sb))
        actb = gb * (sb / (sa + sb))
        y = (jnp.dot(acta.astype(BF16), wd_ref[0, ea], preferred_element_type=F32)
             + jnp.dot(actb.astype(BF16), wd_ref[0, eb], preferred_element_type=F32))
        for c in range(REC):
            y_ref[pl.ds(c, tm, stride=REC), :] = y[:, c * LANES:(c + 1) * LANES]

    @pl.when(valid_ref[i] == 0)
    def _():
        y_ref[...] = jnp.zeros_like(y_ref)


def _moe_tiles(l, hs, meta, router_rows, wg, wu, wd, tm, nt):
    d = D_MODEL
    resident = pl.Buffered(1)
    layer = lambda i, ea, eb, valid, blk: (l, 0, 0, 0)
    grid_spec = pltpu.PrefetchScalarGridSpec(
        num_scalar_prefetch=4,
        grid=(nt,),
        in_specs=[
            pl.BlockSpec((tm * REC, LANES), lambda i, ea, eb, valid, blk: (blk[i], 0)),
            pl.BlockSpec((N_EXPERTS, 1, d), lambda i, ea, eb, valid, blk: (0, 0, 0),
                         pipeline_mode=resident),
            pl.BlockSpec((1, N_EXPERTS, d, D_EXPERT), layer, pipeline_mode=resident),
            pl.BlockSpec((1, N_EXPERTS, d, D_EXPERT), layer, pipeline_mode=resident),
            pl.BlockSpec((1, N_EXPERTS, D_EXPERT, d), layer, pipeline_mode=resident),
        ],
        out_specs=pl.BlockSpec((tm * REC, LANES), lambda i, ea, eb, valid, blk: (i, 0)),
    )
    return pl.pallas_call(
        functools.partial(_moe_body, tm),
        grid_spec=grid_spec,
        out_shape=jax.ShapeDtypeStruct((nt * tm * REC, LANES), F32),
        compiler_params=pltpu.CompilerParams(
            dimension_semantics=("arbitrary",), vmem_limit_bytes=VMEM_LIMIT),
        name="moe_tiles",
    )(meta[META_EA, :nt], meta[META_EB, :nt], meta[META_VALID, :nt], meta[META_BLK, :nt],
      hs, router_rows, wg, wu, wd)


def _final_body(ts, pos_ref, x_ref, ys_hbm, mod_ref, g_ref, o_ref, ybuf, gsem):
    y_ref = _gather_moe_rows(pos_ref, ys_hbm, ybuf, gsem, ts)
    y = jnp.concatenate([_rows_of(y_ref, c, ts) for c in range(REC)], axis=1)
    x = x_ref[...] + mod_ref[0, 0, 5:6, :] * y
    r = lax.rsqrt(jnp.mean(x * x, axis=-1, keepdims=True) + EPS)
    o_ref[...] = x * r * g_ref[...]


def _final_norm(x, ys, pos, mods, final_g, ts, batch):
    t, d = x.shape
    nj = t // batch // ts
    last = mods.shape[0] - 1
    tile = lambda bi, j, pos: (bi * nj + j, 0)
    grid_spec = pltpu.PrefetchScalarGridSpec(
        num_scalar_prefetch=1,
        grid=(batch, nj),
        in_specs=[pl.BlockSpec((ts, d), tile), pl.BlockSpec(memory_space=pl.ANY),
                  pl.BlockSpec((1, 1, N_MOD, d), lambda bi, j, pos: (last, bi, 0, 0)),
                  pl.BlockSpec((1, d), lambda bi, j, pos: (0, 0))],
        out_specs=pl.BlockSpec((ts, d), tile),
        scratch_shapes=[pltpu.VMEM((2, ts * REC, LANES), F32), pltpu.SemaphoreType.DMA((2,))],
    )
    return pl.pallas_call(
        functools.partial(_final_body, ts),
        grid_spec=grid_spec,
        out_shape=jax.ShapeDtypeStruct((t, d), F32),
        compiler_params=pltpu.CompilerParams(
            dimension_semantics=("arbitrary", "arbitrary"), vmem_limit_bytes=VMEM_LIMIT),
        name="final_norm",
    )(pos, x, ys, mods, final_g.reshape(1, d))


def _stacked_weights(norm1_g, norm2_g, w_in, pool_w, pool_scale, conv_dw, conv_db, conv_ln_g,
                     conv_ln_b, conv_pw, conv_pb, sg_ln_g, sg_ln_b, sg_w, sg_b, sc_w, out_norm_g, w_out):
    depth = w_in.shape[0]
    row = lambda v: v.reshape(depth, 1, -1)
    pool_bd = jnp.zeros((depth, GROUP_W, GROUP_W), pool_w.dtype)
    for i in range(len(POOL_WINDOWS)):
        pool_bd = pool_bd.at[:, i * POOL_CG:(i + 1) * POOL_CG, i * POOL_CG:(i + 1) * POOL_CG].set(
            pool_w[:, i])
    return {
        "n1g": row(norm1_g), "n2g": row(norm2_g),
        "w_in": w_in.astype(BF16),
        "pool_w": pool_bd.astype(BF16),
        "pool_scale": row(pool_scale),
        "conv_dw": jnp.pad(conv_dw, ((0, 0), (0, 1), (0, 0))),
        "conv_db": row(conv_db), "conv_ln_g": row(conv_ln_g), "conv_ln_b": row(conv_ln_b),
        "conv_pw": conv_pw.astype(BF16), "conv_pb": row(conv_pb),
        "sg_ln_g": row(sg_ln_g), "sg_ln_b": row(sg_ln_b),
        "sg_w": sg_w,
        "sg_b": jnp.repeat(jnp.swapaxes(sg_b, 1, 2), GROUP_W // SG_HEADS, axis=2),
        "sc_w": jnp.pad(sc_w, ((0, 0), (0, SUBLANES - SC_WIDTH), (0, 0))),
        "out_norm_g": row(out_norm_g),
        "w_out": w_out.astype(BF16),
    }


def kernel(x, c, norm1_g, norm2_g, w_ada, b_ada, w_in, pool_w, pool_scale, conv_dw, conv_db, conv_ln_g, conv_ln_b, conv_pw, conv_pb, sg_ln_g, sg_ln_b, sg_w, sg_b, sc_w, out_norm_g, w_out, router_w, router_bias, exp_w_gate, exp_w_up, exp_w_down, final_g):
    b, s, d = x.shape
    depth = w_in.shape[0]
    t = b * s
    ts = 512
    sb = 256
    tm = 256
    nt = t // tm + N_CLASSES
    assert d == D_MODEL and t % LANES == 0 and s % ts == 0 and t % tm == 0 and nt <= LANES
    assert t % PERMUTE_ROWS == 0 and ts % sb == 0 and sb % SG_CHUNK == 0

    mods = _ada_mod(c, w_ada, b_ada)

    perm = lambda v: v.reshape(N_EXPERT_GROUPS, EXPERTS_PER_GROUP, -1).transpose(1, 0, 2).reshape(
        N_EXPERTS, -1)
    lw = _stacked_weights(norm1_g, norm2_g, w_in, pool_w, pool_scale, conv_dw, conv_db, conv_ln_g,
                          conv_ln_b, conv_pw, conv_pb, sg_ln_g, sg_ln_b, sg_w, sg_b, sc_w, out_norm_g,
                          w_out)
    lw["router_wt"] = perm(router_w.T).astype(BF16)
    lw["router_b"] = perm(router_bias.reshape(N_EXPERTS, 1))
    router_rows = router_w.T.astype(BF16).reshape(N_EXPERTS, 1, d)
    wg = exp_w_gate.astype(BF16)
    wu = exp_w_up.astype(BF16)
    wd = exp_w_down.astype(BF16)

    xcur, ys, pos = x.reshape(t, d), None, None
    for l in range(depth):
        xcur, rec, cls = _mixer_layer(l, xcur, ys, pos, mods, lw, ts, sb, b)
        pos2d, meta = _route_plan(cls.reshape(t // LANES, LANES), tm)
        pos = pos2d.reshape(t)
        hs = _row_dispatch(rec, pos, meta, tm, nt)
        ys = _moe_tiles(l, hs, meta, router_rows, wg, wu, wd, tm, nt)
    return _final_norm(xcur, ys, pos, mods, final_g, ts, b).reshape(b, s, d)
```

```python
import functools

import jax
import jax.numpy as jnp
from jax import lax
from jax.experimental import pallas as pl
from jax.experimental.pallas import tpu as pltpu

D_MODEL = 1024
GROUP_W = 256
POOL_WINDOWS = (2, 4, 8, 16)
POOL_CG = 64
CONV_WIDTH = 31
SG_CHUNK = 128
SG_HEADS = 4
SC_WIDTH = 3
N_EXPERTS = 16
N_EXPERT_GROUPS = 4
EXPERTS_PER_GROUP = 4
PAIRS_PER_GROUP = 6
N_CLASSES = N_EXPERT_GROUPS * PAIRS_PER_GROUP
D_EXPERT = 256
N_MOD = 6
EPS = 1e-6
LN_EPS = 1e-5

LANES = 128
SUBLANES = 8
REC = SUBLANES
CONV_HALO = 32
POOL_HALO = 16
SC_HALO = 8
CONV_ROWS = 32
META_EA, META_EB, META_VALID, META_BLK, META_END = 0, 1, 2, 3, 4
PERMUTE_ROWS = 1024
PERMUTE_GROUP = 16
VMEM_LIMIT = 56 * 1024 * 1024

F32 = jnp.float32
BF16 = jnp.bfloat16
I32 = jnp.int32


def _rows_of(ref, c, n, first=0):
    return ref[pl.ds(first * REC + c, n, stride=REC), :]


def _ada_body(c_ref, w_ref, b_ref, o_ref):
    ca = jax.nn.silu(c_ref[...])
    o_ref[0] = jnp.dot(ca.astype(BF16), w_ref[0].astype(BF16), preferred_element_type=F32) + b_ref[0]


def _ada_mod(c, w_ada, b_ada):
    depth, d, n = w_ada.shape
    b = c.shape[0]
    tn = 1024
    cp = jnp.pad(c, ((0, SUBLANES - b), (0, 0)))
    out = pl.pallas_call(
        _ada_body,
        grid=(depth, n // tn),
        in_specs=[
            pl.BlockSpec((SUBLANES, d), lambda l, j: (0, 0)),
            pl.BlockSpec((1, d, tn), lambda l, j: (l, 0, j)),
            pl.BlockSpec((1, 1, tn), lambda l, j: (l, 0, j)),
        ],
        out_specs=pl.BlockSpec((1, SUBLANES, tn), lambda l, j: (l, 0, j)),
        out_shape=jax.ShapeDtypeStruct((depth, SUBLANES, n), F32),
        compiler_params=pltpu.CompilerParams(
            dimension_semantics=("arbitrary", "arbitrary"), vmem_limit_bytes=VMEM_LIMIT),
        name="ada_mod",
    )(cp, w_ada, b_ada.reshape(depth, 1, n))
    return out[:, :b].reshape(depth, b, N_MOD, d)


def _layer_norm_rows(v, g, b):
    mu = jnp.mean(v, axis=-1, keepdims=True)
    vc = v - mu
    var = jnp.mean(vc * vc, axis=-1, keepdims=True)
    return vc * lax.rsqrt(var + LN_EPS) * g + b


def _carry_tail(ext_ref, j, ts, halo):
    nslab = ext_ref.shape[0]

    @pl.when(j == 0)
    def _():
        ext_ref[:, 0:halo, :] = jnp.zeros((nslab, halo, LANES), F32)

    @pl.when(j > 0)
    def _():
        ext_ref[:, 0:halo, :] = ext_ref[:, ts:ts + halo, :]


def _append_rows(ext_ref, cur, first):
    for cb in range(ext_ref.shape[0]):
        ext_ref[cb, first:first + cur.shape[0], :] = cur[:, cb * LANES:(cb + 1) * LANES]


def _first_argmax(vals):
    best = vals[0]
    idx = jnp.zeros_like(best)
    for k in range(1, len(vals)):
        better = vals[k] > best
        idx = jnp.where(better, float(k), idx)
        best = jnp.where(better, vals[k], best)
    return idx, best


def _pick(idx, vals):
    out = vals[0]
    for k in range(1, len(vals)):
        out = jnp.where(idx == float(k), vals[k], out)
    return out


def _gather_moe_rows(pos_ref, ys_hbm, ybuf, gsem, ts):
    g = pl.program_id(0) * pl.num_programs(1) + pl.program_id(1)
    n_steps = pl.num_programs(0) * pl.num_programs(1)

    def start(step, buf):
        base = step * ts
        for r0 in range(0, ts, PERMUTE_GROUP):
            slots = [pos_ref[base + r0 + k] for k in range(PERMUTE_GROUP)]
            for k, slot in enumerate(slots):
                pltpu.make_async_copy(
                    ys_hbm.at[pl.ds(pl.multiple_of(slot * REC, REC), REC), :],
                    ybuf.at[buf, pl.ds((r0 + k) * REC, REC), :],
                    gsem.at[buf]).start(priority=k % 2)

    @pl.when(g == 0)
    def _():
        start(0, 0)

    @pl.when(g + 1 < n_steps)
    def _():
        start(g + 1, (g + 1) % 2)

    cur = g % 2
    pltpu.make_async_copy(ys_hbm.at[pl.ds(0, ts * REC), :], ybuf.at[cur], gsem.at[cur]).wait()
    return ybuf.at[cur]


def _mixer_body(has_prev, ts, sb, *refs):
    refs = list(refs)
    if has_prev:
        pos_ref = refs.pop(0)
    x_ref = refs.pop(0)
    if has_prev:
        ys_hbm = refs.pop(0)
        modp_ref = refs.pop(0)
        gsem = refs.pop()
        ybuf = refs.pop()
    (mod_ref, n1g_ref, n2g_ref, win_ref, poolw_ref, pools_ref,
     cdw_ref, cdb_ref, clg_ref, clb_ref, cpw_ref, cpb_ref,
     slg_ref, slb_ref, sgw_ref, sgb_ref, scw_ref, ong_ref, wout_ref, rwt_ref, rb_ref,
     x1_ref, rec_ref, cls_ref,
     ea_ref, eg_ref, ed_ref) = refs

    j = pl.program_id(1)
    gw = GROUP_W
    nslab = gw // LANES
    if has_prev:
        yprev_ref = _gather_moe_rows(pos_ref, ys_hbm, ybuf, gsem, ts)

    sh1 = mod_ref[0, 0, 0:1, :]
    sc1 = mod_ref[0, 0, 1:2, :]
    g1 = mod_ref[0, 0, 2:3, :]
    sh2 = mod_ref[0, 0, 3:4, :]
    sc2 = mod_ref[0, 0, 4:5, :]
    gain1 = n1g_ref[0] * (1.0 + sc1)
    gain2 = n2g_ref[0] * (1.0 + sc2)

    _carry_tail(ea_ref, j, ts, POOL_HALO)
    _carry_tail(eg_ref, j, ts, CONV_HALO)
    _carry_tail(ed_ref, j, ts, SC_HALO)

    rowi = lax.broadcasted_iota(I32, (SG_CHUNK, SG_CHUNK), 0)
    coli = lax.broadcasted_iota(I32, (SG_CHUNK, SG_CHUNK), 1)
    tril = coli <= rowi
    wcat = jnp.concatenate(
        [jnp.where(tril, sgw_ref[0, hh], 0.0) for hh in range(SG_HEADS)], axis=1).astype(BF16)
    lane_g = lax.broadcasted_iota(I32, (SG_CHUNK, gw), 1) // (gw // SG_HEADS)
    zero_bf = jnp.zeros((SG_CHUNK, gw), BF16)
    lane = lax.broadcasted_iota(I32, (sb, LANES), 1)
    first = lane < POOL_CG

    for r0 in range(0, ts, sb):
        x = x_ref[r0:r0 + sb, :]
        if has_prev:
            yprev = jnp.concatenate([_rows_of(yprev_ref, c, sb, r0) for c in range(REC)], axis=1)
            x = x + modp_ref[0, 0, 5:6, :] * yprev

        r1 = lax.rsqrt(jnp.mean(x * x, axis=-1, keepdims=True) + EPS)
        h = x * r1 * gain1 + sh1
        p = jnp.dot(h.astype(BF16), win_ref[0], preferred_element_type=F32)

        a = p[:, 0:gw]
        _append_rows(ea_ref, a, POOL_HALO + r0)

        def a_shift(s, cb):
            return ea_ref[cb, POOL_HALO + r0 - s:POOL_HALO + r0 - s + sb, :]

        pos1 = (lax.broadcasted_iota(I32, (sb, LANES), 0) + (j * ts + r0 + 1)).astype(F32)
        a_lo = a[:, 0:LANES]
        s01 = a_lo + a_shift(1, 0)
        s03 = s01 + (a_shift(2, 0) + a_shift(3, 0))
        num_lo = jnp.where(first, s01, s03)
        den_lo = jnp.minimum(pos1, jnp.where(first, float(POOL_WINDOWS[0]), float(POOL_WINDOWS[1])))
        a_hi = a[:, LANES:gw]
        s07 = a_hi
        for s in range(1, 8):
            s07 = s07 + a_shift(s, 1)
        s815 = a_shift(8, 1)
        for s in range(9, 16):
            s815 = s815 + a_shift(s, 1)
        num_hi = jnp.where(first, s07, s07 + s815)
        den_hi = jnp.minimum(pos1, jnp.where(first, float(POOL_WINDOWS[2]), float(POOL_WINDOWS[3])))
        d_pool = jnp.concatenate([num_lo / den_lo - a_lo, num_hi / den_hi - a_hi], axis=1)
        y_a = jnp.dot(d_pool.astype(BF16), poolw_ref[0], preferred_element_type=F32) * pools_ref[0]

        glu = p[:, gw:2 * gw] * jax.nn.sigmoid(p[:, 2 * gw:3 * gw])
        _append_rows(eg_ref, glu, CONV_HALO + r0)
        conv_cols = []
        for cb in range(nslab):
            conv_chunks = []
            for rr in range(r0, r0 + sb, CONV_ROWS):
                acc = None
                for k in range(CONV_WIDTH):
                    off = CONV_HALO - (CONV_WIDTH - 1) + k + rr
                    term = (eg_ref[cb, off:off + CONV_ROWS, :]
                            * cdw_ref[0, k:k + 1, cb * LANES:(cb + 1) * LANES])
                    acc = term if acc is None else acc + term
                conv_chunks.append(acc)
            conv_cols.append(jnp.concatenate(conv_chunks, axis=0))
        hb = jnp.concatenate(conv_cols, axis=1) + cdb_ref[0]
        hb = jax.nn.silu(_layer_norm_rows(hb, clg_ref[0], clb_ref[0]))
        y_b = jnp.dot(hb.astype(BF16), cpw_ref[0], preferred_element_type=F32) + cpb_ref[0]

        u = p[:, 3 * gw:4 * gw]
        vln = _layer_norm_rows(p[:, 4 * gw:5 * gw], slg_ref[0], slb_ref[0]).astype(BF16)
        yc_chunks = []
        for n in range(sb // SG_CHUNK):
            vch = vln[n * SG_CHUNK:(n + 1) * SG_CHUNK, :]
            vbd = jnp.concatenate([jnp.where(lane_g == hh, vch, zero_bf) for hh in range(SG_HEADS)],
                                  axis=0)
            mixed = jnp.dot(wcat, vbd, preferred_element_type=F32) + sgb_ref[0]
            yc_chunks.append(u[n * SG_CHUNK:(n + 1) * SG_CHUNK, :] * mixed)
        y_c = jnp.concatenate(yc_chunks, axis=0)

        cx = p[:, 6 * gw:7 * gw] * p[:, 7 * gw:8 * gw]
        _append_rows(ed_ref, cx, SC_HALO + r0)
        cd_cols = []
        for cb in range(nslab):
            ls = slice(cb * LANES, (cb + 1) * LANES)
            at = SC_HALO + r0
            cd_cols.append(cx[:, ls] * scw_ref[0, 2:3, ls]
                           + ed_ref[cb, at - 1:at - 1 + sb, :] * scw_ref[0, 1:2, ls]
                           + ed_ref[cb, at - 2:at - 2 + sb, :] * scw_ref[0, 0:1, ls])
        y_d = p[:, 5 * gw:6 * gw] * jnp.concatenate(cd_cols, axis=1)

        normed = []
        for gi, yg in enumerate((y_a, y_b, y_c, y_d)):
            rg = lax.rsqrt(jnp.mean(yg * yg, axis=-1, keepdims=True) + EPS)
            normed.append((yg * rg * ong_ref[0, :, gi * gw:(gi + 1) * gw]).astype(BF16))
        yn = jnp.concatenate(normed, axis=1)
        x1 = x + g1 * jnp.dot(yn, wout_ref[0], preferred_element_type=F32)
        x1_ref[r0:r0 + sb, :] = x1

        r2 = lax.rsqrt(jnp.mean(x1 * x1, axis=-1, keepdims=True) + EPS)
        h2 = x1 * r2 * gain2 + sh2
        for c in range(REC):
            rec_ref[pl.ds(r0 * REC + c, sb, stride=REC), :] = h2[:, c * LANES:(c + 1) * LANES]
        logits = lax.dot_general(rwt_ref[...], h2.astype(BF16), (((1,), (1,)), ((), ())),
                                 preferred_element_type=F32)
        sel = jax.nn.sigmoid(logits) + rb_ref[...]
        epg = EXPERTS_PER_GROUP
        sel_s = [sel[jj * N_EXPERT_GROUPS:(jj + 1) * N_EXPERT_GROUPS, :] for jj in range(epg)]
        top2 = None
        for ja in range(epg):
            for jb in range(ja + 1, epg):
                pair = sel_s[ja] + sel_s[jb]
                top2 = pair if top2 is None else jnp.maximum(top2, pair)
        gidx, _ = _first_argmax([top2[g:g + 1, :] for g in range(N_EXPERT_GROUPS)])
        sel_c = [_pick(gidx, [sel_s[jj][g:g + 1, :] for g in range(N_EXPERT_GROUPS)])
                 for jj in range(epg)]
        i1, _ = _first_argmax(sel_c)
        i2, _ = _first_argmax([jnp.where(i1 == float(jj), -jnp.inf, sel_c[jj]) for jj in range(epg)])
        lo = jnp.minimum(i1, i2)
        hi = jnp.maximum(i1, i2)
        base = jnp.where(lo == 0.0, 0.0, jnp.where(lo == 1.0, 3.0, 5.0))
        cls = gidx * float(PAIRS_PER_GROUP) + base + (hi - lo - 1.0)
        cls_ref[0, :, r0:r0 + sb] = cls.astype(I32)


def _mixer_layer(l, x, ys_prev, pos_prev, mods, lw, ts, sb, batch):
    t, d = x.shape
    nj = t // batch // ts
    has_prev = ys_prev is not None
    tile = lambda bi, j, *_: (bi * nj + j, 0)

    args = [x]
    in_specs = [pl.BlockSpec((ts, d), tile)]
    if has_prev:
        args += [ys_prev, mods]
        in_specs += [pl.BlockSpec(memory_space=pl.ANY),
                     pl.BlockSpec((1, 1, N_MOD, d), lambda bi, j, *_: (l - 1, bi, 0, 0))]
    args += [mods]
    in_specs += [pl.BlockSpec((1, 1, N_MOD, d), lambda bi, j, *_: (l, bi, 0, 0))]
    for name in ("n1g", "n2g", "w_in", "pool_w", "pool_scale", "conv_dw", "conv_db", "conv_ln_g",
                 "conv_ln_b", "conv_pw", "conv_pb", "sg_ln_g", "sg_ln_b", "sg_w", "sg_b", "sc_w",
                 "out_norm_g", "w_out"):
        arr = lw[name]
        args.append(arr)
        in_specs.append(pl.BlockSpec((1,) + arr.shape[1:],
                                     lambda bi, j, *_, nd=arr.ndim: (l,) + (0,) * (nd - 1)))
    for name in ("router_wt", "router_b"):
        arr = lw[name]
        args.append(arr)
        in_specs.append(pl.BlockSpec(arr.shape, lambda bi, j, *_: (0, 0)))

    out_shape = (jax.ShapeDtypeStruct((t, d), F32),
                 jax.ShapeDtypeStruct((t * REC, LANES), F32),
                 jax.ShapeDtypeStruct((t // ts, 1, ts), I32))
    out_specs = (pl.BlockSpec((ts, d), tile),
                 pl.BlockSpec((ts * REC, LANES), tile),
                 pl.BlockSpec((1, 1, ts), lambda bi, j, *_: (bi * nj + j, 0, 0)))
    nslab = GROUP_W // LANES
    scratch = [pltpu.VMEM((nslab, POOL_HALO + ts, LANES), F32),
               pltpu.VMEM((nslab, CONV_HALO + ts, LANES), F32),
               pltpu.VMEM((nslab, SC_HALO + ts, LANES), F32)]
    if has_prev:
        scratch += [pltpu.VMEM((2, ts * REC, LANES), F32), pltpu.SemaphoreType.DMA((2,))]
        args = [pos_prev] + args
    grid_spec = pltpu.PrefetchScalarGridSpec(
        num_scalar_prefetch=1 if has_prev else 0,
        grid=(batch, nj),
        in_specs=in_specs,
        out_specs=out_specs,
        scratch_shapes=scratch,
    )
    return pl.pallas_call(
        functools.partial(_mixer_body, has_prev, ts, sb),
        grid_spec=grid_spec,
        out_shape=out_shape,
        compiler_params=pltpu.CompilerParams(
            dimension_semantics=("arbitrary", "arbitrary"), vmem_limit_bytes=VMEM_LIMIT),
        name="mixer_layer",
    )(*args)


def _plan_body(tm, cls_ref, pos_ref, meta_ref):
    cls = cls_ref[...]
    nrow = cls.shape[0]
    ii = lax.broadcasted_iota(I32, (LANES, LANES), 0)
    jj = lax.broadcasted_iota(I32, (LANES, LANES), 1)
    upper = jnp.where(ii <= jj, 1.0, 0.0).astype(BF16)
    ri = lax.broadcasted_iota(I32, (nrow, nrow), 0)
    rj = lax.broadcasted_iota(I32, (nrow, nrow), 1)
    lower = jnp.where(rj < ri, 1.0, 0.0).astype(BF16)
    lane_i = lax.broadcasted_iota(I32, (1, LANES), 1)
    tile_i = lane_i.astype(F32)
    start = jnp.zeros((1, 1), F32)
    cum_tiles = jnp.zeros((1, 1), F32)
    pos = jnp.zeros((nrow, LANES), F32)
    tclass = jnp.zeros((1, LANES), F32)
    seg_end = jnp.zeros((1, LANES), F32)
    for k in range(N_CLASSES):
        hit = cls == k
        oh = jnp.where(hit, 1.0, 0.0)
        ohb = oh.astype(BF16)
        incl = jnp.dot(ohb, upper, preferred_element_type=F32)
        above = jnp.sum(jnp.dot(lower, ohb, preferred_element_type=F32), axis=1, keepdims=True)
        rank = incl - oh + above
        pos = pos + jnp.where(hit, start + rank, 0.0)
        cnt = jnp.sum(jnp.sum(oh, axis=1, keepdims=True), axis=0, keepdims=True)
        n_tiles = jnp.floor((cnt + float(tm - 1)) * (1.0 / tm))
        start = start + n_tiles * float(tm)
        cum_tiles = cum_tiles + n_tiles
        tclass = tclass + jnp.where(tile_i >= cum_tiles, 1.0, 0.0)
        seg_end = jnp.where(lane_i == k, start, seg_end)
    pos_ref[...] = pos.astype(I32)

    valid = tclass < float(N_CLASSES)
    kc = jnp.minimum(tclass, float(N_CLASSES - 1))
    grp = jnp.zeros_like(kc)
    for g in range(1, N_EXPERT_GROUPS):
        grp = grp + jnp.where(kc >= float(g * PAIRS_PER_GROUP), 1.0, 0.0)
    pr = kc - grp * float(PAIRS_PER_GROUP)
    lo = jnp.where(pr >= 3.0, 1.0, 0.0) + jnp.where(pr >= 5.0, 1.0, 0.0)
    base = jnp.where(lo == 0.0, 0.0, jnp.where(lo == 1.0, 3.0, 5.0))
    hi = pr - base + lo + 1.0
    rows = {META_EA: grp * float(EXPERTS_PER_GROUP) + lo,
            META_EB: grp * float(EXPERTS_PER_GROUP) + hi,
            META_VALID: jnp.where(valid, 1.0, 0.0),
            META_BLK: jnp.minimum(tile_i, cum_tiles - 1.0),
            META_END: seg_end}
    rid = lax.broadcasted_iota(I32, (SUBLANES, LANES), 0)
    meta = jnp.zeros((SUBLANES, LANES), F32)
    for k, row in rows.items():
        meta = jnp.where(rid == k, row, meta)
    meta_ref[...] = meta.astype(I32)


def _route_plan(cls2d, tm):
    nrow = cls2d.shape[0]
    return pl.pallas_call(
        functools.partial(_plan_body, tm),
        in_specs=[pl.BlockSpec((nrow, LANES), lambda: (0, 0))],
        out_specs=(pl.BlockSpec((nrow, LANES), lambda: (0, 0)),
                   pl.BlockSpec((SUBLANES, LANES), lambda: (0, 0))),
        out_shape=(jax.ShapeDtypeStruct((nrow, LANES), I32),
                   jax.ShapeDtypeStruct((SUBLANES, LANES), I32)),
        name="route_plan",
    )(cls2d)


def _dispatch_body(tm, nt, pos_ref, end_ref, valid_ref, src_ref, dst_hbm, zbuf, zsem, sem):
    g = pl.program_id(0)
    n = PERMUTE_ROWS

    @pl.when(g == 0)
    def _():
        zbuf[...] = jnp.zeros_like(zbuf)

        def zero_tile(first_slot, wait):
            cp = pltpu.make_async_copy(
                zbuf, dst_hbm.at[pl.ds(pl.multiple_of(first_slot * REC, REC), tm * REC), :], zsem)
            cp.wait() if wait else cp.start()

        for wait in (False, True):
            for k in range(N_CLASSES):
                prev_end = end_ref[k - 1] if k > 0 else 0

                @pl.when(end_ref[k] > prev_end)
                def _():
                    zero_tile(end_ref[k] - tm, wait)

            for i in range(nt):
                @pl.when(valid_ref[i] == 0)
                def _():
                    zero_tile(i * tm, wait)

    base = g * n
    for r0 in range(0, n, PERMUTE_GROUP):
        slots = [pos_ref[base + r0 + k] for k in range(PERMUTE_GROUP)]
        for k, slot in enumerate(slots):
            pltpu.make_async_copy(
                src_ref.at[pl.ds((r0 + k) * REC, REC), :],
                dst_hbm.at[pl.ds(pl.multiple_of(slot * REC, REC), REC), :],
                sem).start(priority=k % 2)

    pltpu.make_async_copy(src_ref, dst_hbm.at[pl.ds(0, n * REC), :], sem).wait()


def _row_dispatch(rec, pos, meta, tm, nt):
    n_tok = pos.shape[0]
    grid_spec = pltpu.PrefetchScalarGridSpec(
        num_scalar_prefetch=3,
        grid=(n_tok // PERMUTE_ROWS,),
        in_specs=[pl.BlockSpec((PERMUTE_ROWS * REC, LANES), lambda g, pos, end, valid: (g, 0))],
        out_specs=pl.BlockSpec(memory_space=pl.ANY),
        scratch_shapes=[pltpu.VMEM((tm * REC, LANES), rec.dtype), pltpu.SemaphoreType.DMA(()),
                        pltpu.SemaphoreType.DMA(())],
    )
    return pl.pallas_call(
        functools.partial(_dispatch_body, tm, nt),
        grid_spec=grid_spec,
        out_shape=jax.ShapeDtypeStruct((nt * tm * REC, LANES), rec.dtype),
        compiler_params=pltpu.CompilerParams(
            dimension_semantics=("arbitrary",), vmem_limit_bytes=VMEM_LIMIT),
        name="row_dispatch",
    )(pos, meta[META_END, :N_CLASSES], meta[META_VALID, :nt], rec)


def _moe_body(tm, l, ea_ref, eb_ref, valid_ref, blk_ref, h_ref, rw_ref, wg_hbm, wu_hbm, wd_hbm, y_ref,
              wg_ref, wu_ref, wd_ref, sg_ref, su_ref, sd_ref, wsem):
    i = pl.program_id(0)

    @pl.when(i == 0)
    def _():
        def copies(e, buf):
            return (pltpu.make_async_copy(wg_hbm.at[l, e], sg_ref.at[buf], wsem.at[buf]),
                    pltpu.make_async_copy(wu_hbm.at[l, e], su_ref.at[buf], wsem.at[buf]),
                    pltpu.make_async_copy(wd_hbm.at[l, e], sd_ref.at[buf], wsem.at[buf]))

        for cp in copies(0, 0):
            cp.start()

        def stage(e, carry):
            buf = e % 2

            @pl.when(e + 1 < N_EXPERTS)
            def _():
                for cp in copies(e + 1, 1 - buf):
                    cp.start()

            for cp in copies(e, buf):
                cp.wait()
            wg_ref[e] = sg_ref[buf].astype(BF16)
            wu_ref[e] = su_ref[buf].astype(BF16)
            wd_ref[e] = sd_ref[buf].astype(BF16)
            return carry

        lax.fori_loop(0, N_EXPERTS, stage, 0)

    @pl.when(valid_ref[i] == 1)
    def _():
        ea = ea_ref[i]
        eb = eb_ref[i]
        xb = jnp.concatenate([_rows_of(h_ref, c, tm) for c in range(REC)], axis=1).astype(BF16)
        xf = xb.astype(F32)

        def expert(e):
            score = jax.nn.sigmoid(jnp.sum(xf * rw_ref[e].astype(F32), axis=-1, keepdims=True))
            hg = jnp.dot(xb, wg_ref[e], preferred_element_type=F32)
            hu = jnp.dot(xb, wu_ref[e], preferred_element_type=F32)
            return score, jax.nn.silu(hg) * hu

        sa, ga = expert(ea)
        sb, gb = expert(eb)
        acta = ga * (sa / (sa + sb))
        actb = gb * (sb / (sa + sb))
        y = (jnp.dot(acta.astype(BF16), wd_ref[ea], preferred_element_type=F32)
             + jnp.dot(actb.astype(BF16), wd_ref[eb], preferred_element_type=F32))
        for c in range(REC):
            y_ref[pl.ds(c, tm, stride=REC), :] = y[:, c * LANES:(c + 1) * LANES]

    @pl.when(valid_ref[i] == 0)
    def _():
        y_ref[...] = jnp.zeros_like(y_ref)


def _moe_tiles(l, hs, meta, router_rows, wg, wu, wd, tm, nt):
    d = D_MODEL
    whole = pl.BlockSpec(memory_space=pl.ANY)
    grid_spec = pltpu.PrefetchScalarGridSpec(
        num_scalar_prefetch=4,
        grid=(nt,),
        in_specs=[
            pl.BlockSpec((tm * REC, LANES), lambda i, ea, eb, valid, blk: (blk[i], 0)),
            pl.BlockSpec((N_EXPERTS, 1, d), lambda i, ea, eb, valid, blk: (0, 0, 0),
                         pipeline_mode=pl.Buffered(1)),
            whole, whole, whole,
        ],
        out_specs=pl.BlockSpec((tm * REC, LANES), lambda i, ea, eb, valid, blk: (i, 0)),
        scratch_shapes=[pltpu.VMEM((N_EXPERTS, d, D_EXPERT), BF16),
                        pltpu.VMEM((N_EXPERTS, d, D_EXPERT), BF16),
                        pltpu.VMEM((N_EXPERTS, D_EXPERT, d), BF16),
                        pltpu.VMEM((2, d, D_EXPERT), F32),
                        pltpu.VMEM((2, d, D_EXPERT), F32),
                        pltpu.VMEM((2, D_EXPERT, d), F32),
                        pltpu.SemaphoreType.DMA((2,))],
    )
    return pl.pallas_call(
        functools.partial(_moe_body, tm, l),
        grid_spec=grid_spec,
        out_shape=jax.ShapeDtypeStruct((nt * tm * REC, LANES), F32),
        compiler_params=pltpu.CompilerParams(
            dimension_semantics=("arbitrary",), vmem_limit_bytes=VMEM_LIMIT),
        name="moe_tiles",
    )(meta[META_EA, :nt], meta[META_EB, :nt], meta[META_VALID, :nt], meta[META_BLK, :nt],
      hs, router_rows, wg, wu, wd)


def _final_body(ts, pos_ref, x_ref, ys_hbm, mod_ref, g_ref, o_ref, ybuf, gsem):
    y_ref = _gather_moe_rows(pos_ref, ys_hbm, ybuf, gsem, ts)
    y = jnp.concatenate([_rows_of(y_ref, c, ts) for c in range(REC)], axis=1)
    x = x_ref[...] + mod_ref[0, 0, 5:6, :] * y
    r = lax.rsqrt(jnp.mean(x * x, axis=-1, keepdims=True) + EPS)
    o_ref[...] = x * r * g_ref[...]


def _final_norm(x, ys, pos, mods, final_g, ts, batch):
    t, d = x.shape
    nj = t // batch // ts
    last = mods.shape[0] - 1
    tile = lambda bi, j, pos: (bi * nj + j, 0)
    grid_spec = pltpu.PrefetchScalarGridSpec(
        num_scalar_prefetch=1,
        grid=(batch, nj),
        in_specs=[pl.BlockSpec((ts, d), tile), pl.BlockSpec(memory_space=pl.ANY),
                  pl.BlockSpec((1, 1, N_MOD, d), lambda bi, j, pos: (last, bi, 0, 0)),
                  pl.BlockSpec((1, d), lambda bi, j, pos: (0, 0))],
        out_specs=pl.BlockSpec((ts, d), tile),
        scratch_shapes=[pltpu.VMEM((2, ts * REC, LANES), F32), pltpu.SemaphoreType.DMA((2,))],
    )
    return pl.pallas_call(
        functools.partial(_final_body, ts),
        grid_spec=grid_spec,
        out_shape=jax.ShapeDtypeStruct((t, d), F32),
        compiler_params=pltpu.CompilerParams(
            dimension_semantics=("arbitrary", "arbitrary"), vmem_limit_bytes=VMEM_LIMIT),
        name="final_norm",
    )(pos, x, ys, mods, final_g.reshape(1, d))


def _stacked_weights(norm1_g, norm2_g, w_in, pool_w, pool_scale, conv_dw, conv_db, conv_ln_g,
                     conv_ln_b, conv_pw, conv_pb, sg_ln_g, sg_ln_b, sg_w, sg_b, sc_w, out_norm_g, w_out):
    depth = w_in.shape[0]
    row = lambda v: v.reshape(depth, 1, -1)
    pool_bd = jnp.zeros((depth, GROUP_W, GROUP_W), pool_w.dtype)
    for i in range(len(POOL_WINDOWS)):
        pool_bd = pool_bd.at[:, i * POOL_CG:(i + 1) * POOL_CG, i * POOL_CG:(i + 1) * POOL_CG].set(
            pool_w[:, i])
    return {
        "n1g": row(norm1_g), "n2g": row(norm2_g),
        "w_in": w_in.astype(BF16),
        "pool_w": pool_bd.astype(BF16),
        "pool_scale": row(pool_scale),
        "conv_dw": jnp.pad(conv_dw, ((0, 0), (0, 1), (0, 0))),
        "conv_db": row(conv_db), "conv_ln_g": row(conv_ln_g), "conv_ln_b": row(conv_ln_b),
        "conv_pw": conv_pw.astype(BF16), "conv_pb": row(conv_pb),
        "sg_ln_g": row(sg_ln_g), "sg_ln_b": row(sg_ln_b),
        "sg_w": sg_w,
        "sg_b": jnp.repeat(jnp.swapaxes(sg_b, 1, 2), GROUP_W // SG_HEADS, axis=2),
        "sc_w": jnp.pad(sc_w, ((0, 0), (0, SUBLANES - SC_WIDTH), (0, 0))),
        "out_norm_g": row(out_norm_g),
        "w_out": w_out.astype(BF16),
    }


def kernel(x, c, norm1_g, norm2_g, w_ada, b_ada, w_in, pool_w, pool_scale, conv_dw, conv_db, conv_ln_g, conv_ln_b, conv_pw, conv_pb, sg_ln_g, sg_ln_b, sg_w, sg_b, sc_w, out_norm_g, w_out, router_w, router_bias, exp_w_gate, exp_w_up, exp_w_down, final_g):
    b, s, d = x.shape
    depth = w_in.shape[0]
    t = b * s
    ts = 1024
    sb = 256
    tm = 256
    nt = t // tm + N_CLASSES
    assert d == D_MODEL and t % LANES == 0 and s % ts == 0 and t % tm == 0 and nt <= LANES
    assert t % PERMUTE_ROWS == 0 and ts % sb == 0 and sb % SG_CHUNK == 0

    mods = _ada_mod(c, w_ada, b_ada)

    perm = lambda v: v.reshape(N_EXPERT_GROUPS, EXPERTS_PER_GROUP, -1).transpose(1, 0, 2).reshape(
        N_EXPERTS, -1)
    lw = _stacked_weights(norm1_g, norm2_g, w_in, pool_w, pool_scale, conv_dw, conv_db, conv_ln_g,
                          conv_ln_b, conv_pw, conv_pb, sg_ln_g, sg_ln_b, sg_w, sg_b, sc_w, out_norm_g,
                          w_out)
    lw["router_wt"] = perm(router_w.T).astype(BF16)
    lw["router_b"] = perm(router_bias.reshape(N_EXPERTS, 1))
    router_rows = router_w.T.astype(BF16).reshape(N_EXPERTS, 1, d)

    xcur, ys, pos = x.reshape(t, d), None, None
    for l in range(depth):
        xcur, rec, cls = _mixer_layer(l, xcur, ys, pos, mods, lw, ts, sb, b)
        pos2d, meta = _route_plan(cls.reshape(t // LANES, LANES), tm)
        pos = pos2d.reshape(t)
        hs = _row_dispatch(rec, pos, meta, tm, nt)
        ys = _moe_tiles(l, hs, meta, router_rows, exp_w_gate, exp_w_up, exp_w_down, tm, nt)
    return _final_norm(xcur, ys, pos, mods, final_g, ts, b).reshape(b, s, d)
```

```python
import functools

import jax
import jax.numpy as jnp
from jax import lax
from jax.experimental import pallas as pl
from jax.experimental.pallas import tpu as pltpu

D_MODEL = 1024
GROUP_W = 256
POOL_WINDOWS = (2, 4, 8, 16)
POOL_CG = 64
CONV_WIDTH = 31
SG_CHUNK = 128
SG_HEADS = 4
SC_WIDTH = 3
N_EXPERTS = 16
N_EXPERT_GROUPS = 4
EXPERTS_PER_GROUP = 4
PAIRS_PER_GROUP = 6
N_CLASSES = N_EXPERT_GROUPS * PAIRS_PER_GROUP
D_EXPERT = 256
N_MOD = 6
EPS = 1e-6
LN_EPS = 1e-5

LANES = 128
SUBLANES = 8
REC = SUBLANES
CONV_HALO = 32
POOL_HALO = 16
SC_HALO = 8
CONV_ROWS = 32
META_EA, META_EB, META_VALID, META_BLK, META_END = 0, 1, 2, 3, 4
PERMUTE_ROWS = 1024
PERMUTE_GROUP = 16
VMEM_LIMIT = 56 * 1024 * 1024

F32 = jnp.float32
BF16 = jnp.bfloat16
I32 = jnp.int32


def _rows_of(ref, c, n, first=0):
    return ref[pl.ds(first * REC + c, n, stride=REC), :]


def _ada_body(c_ref, w_ref, b_ref, o_ref):
    ca = jax.nn.silu(c_ref[...])
    o_ref[0] = jnp.dot(ca.astype(BF16), w_ref[0].astype(BF16), preferred_element_type=F32) + b_ref[0]


def _ada_mod(c, w_ada, b_ada):
    depth, d, n = w_ada.shape
    b = c.shape[0]
    tn = 1024
    cp = jnp.pad(c, ((0, SUBLANES - b), (0, 0)))
    out = pl.pallas_call(
        _ada_body,
        grid=(depth, n // tn),
        in_specs=[
            pl.BlockSpec((SUBLANES, d), lambda l, j: (0, 0)),
            pl.BlockSpec((1, d, tn), lambda l, j: (l, 0, j)),
            pl.BlockSpec((1, 1, tn), lambda l, j: (l, 0, j)),
        ],
        out_specs=pl.BlockSpec((1, SUBLANES, tn), lambda l, j: (l, 0, j)),
        out_shape=jax.ShapeDtypeStruct((depth, SUBLANES, n), F32),
        compiler_params=pltpu.CompilerParams(
            dimension_semantics=("arbitrary", "arbitrary"), vmem_limit_bytes=VMEM_LIMIT),
        name="ada_mod",
    )(cp, w_ada, b_ada.reshape(depth, 1, n))
    return out[:, :b].reshape(depth, b, N_MOD, d)


def _layer_norm_rows(v, g, b):
    mu = jnp.mean(v, axis=-1, keepdims=True)
    vc = v - mu
    var = jnp.mean(vc * vc, axis=-1, keepdims=True)
    return vc * lax.rsqrt(var + LN_EPS) * g + b


def _carry_tail(ext_ref, j, ts, halo):
    nslab = ext_ref.shape[0]

    @pl.when(j == 0)
    def _():
        ext_ref[:, 0:halo, :] = jnp.zeros((nslab, halo, LANES), F32)

    @pl.when(j > 0)
    def _():
        ext_ref[:, 0:halo, :] = ext_ref[:, ts:ts + halo, :]


def _append_rows(ext_ref, cur, first):
    for cb in range(ext_ref.shape[0]):
        ext_ref[cb, first:first + cur.shape[0], :] = cur[:, cb * LANES:(cb + 1) * LANES]


def _first_argmax(vals):
    best = vals[0]
    idx = jnp.zeros_like(best)
    for k in range(1, len(vals)):
        better = vals[k] > best
        idx = jnp.where(better, float(k), idx)
        best = jnp.where(better, vals[k], best)
    return idx, best


def _pick(idx, vals):
    out = vals[0]
    for k in range(1, len(vals)):
        out = jnp.where(idx == float(k), vals[k], out)
    return out


def _gather_moe_rows(pos_ref, ys_hbm, ybuf, gsem, ts):
    g = pl.program_id(0) * pl.num_programs(1) + pl.program_id(1)
    n_steps = pl.num_programs(0) * pl.num_programs(1)

    def start(step, buf):
        base = step * ts
        for r0 in range(0, ts, PERMUTE_GROUP):
            slots = [pos_ref[base + r0 + k] for k in range(PERMUTE_GROUP)]
            for k, slot in enumerate(slots):
                pltpu.make_async_copy(
                    ys_hbm.at[pl.ds(pl.multiple_of(slot * REC, REC), REC), :],
                    ybuf.at[buf, pl.ds((r0 + k) * REC, REC), :],
                    gsem.at[buf]).start(priority=k % 2)

    @pl.when(g == 0)
    def _():
        start(0, 0)

    @pl.when(g + 1 < n_steps)
    def _():
        start(g + 1, (g + 1) % 2)

    cur = g % 2
    pltpu.make_async_copy(ys_hbm.at[pl.ds(0, ts * REC), :], ybuf.at[cur], gsem.at[cur]).wait()
    return ybuf.at[cur]


def _mixer_body(has_prev, ts, sb, *refs):
    refs = list(refs)
    if has_prev:
        pos_ref = refs.pop(0)
    x_ref = refs.pop(0)
    if has_prev:
        ys_hbm = refs.pop(0)
        modp_ref = refs.pop(0)
        gsem = refs.pop()
        ybuf = refs.pop()
    (mod_ref, n1g_ref, n2g_ref, win_ref, poolw_ref, pools_ref,
     cdw_ref, cdb_ref, clg_ref, clb_ref, cpw_ref, cpb_ref,
     slg_ref, slb_ref, sgw_ref, sgb_ref, scw_ref, ong_ref, wout_ref, rwt_ref, rb_ref,
     x1_ref, rec_ref, cls_ref,
     ea_ref, eg_ref, ed_ref) = refs

    j = pl.program_id(1)
    gw = GROUP_W
    nslab = gw // LANES
    if has_prev:
        yprev_ref = _gather_moe_rows(pos_ref, ys_hbm, ybuf, gsem, ts)

    sh1 = mod_ref[0, 0, 0:1, :]
    sc1 = mod_ref[0, 0, 1:2, :]
    g1 = mod_ref[0, 0, 2:3, :]
    sh2 = mod_ref[0, 0, 3:4, :]
    sc2 = mod_ref[0, 0, 4:5, :]
    gain1 = n1g_ref[0] * (1.0 + sc1)
    gain2 = n2g_ref[0] * (1.0 + sc2)

    _carry_tail(ea_ref, j, ts, POOL_HALO)
    _carry_tail(eg_ref, j, ts, CONV_HALO)
    _carry_tail(ed_ref, j, ts, SC_HALO)

    rowi = lax.broadcasted_iota(I32, (SG_CHUNK, SG_CHUNK), 0)
    coli = lax.broadcasted_iota(I32, (SG_CHUNK, SG_CHUNK), 1)
    tril = coli <= rowi
    wcat = jnp.concatenate(
        [jnp.where(tril, sgw_ref[0, hh], 0.0) for hh in range(SG_HEADS)], axis=1).astype(BF16)
    lane_g = lax.broadcasted_iota(I32, (SG_CHUNK, gw), 1) // (gw // SG_HEADS)
    zero_bf = jnp.zeros((SG_CHUNK, gw), BF16)
    lane = lax.broadcasted_iota(I32, (sb, LANES), 1)
    first = lane < POOL_CG

    for r0 in range(0, ts, sb):
        x = x_ref[r0:r0 + sb, :]
        if has_prev:
            yprev = jnp.concatenate([_rows_of(yprev_ref, c, sb, r0) for c in range(REC)], axis=1)
            x = x + modp_ref[0, 0, 5:6, :] * yprev

        r1 = lax.rsqrt(jnp.mean(x * x, axis=-1, keepdims=True) + EPS)
        h = x * r1 * gain1 + sh1
        p = jnp.dot(h.astype(BF16), win_ref[0], preferred_element_type=F32)

        a = p[:, 0:gw]
        _append_rows(ea_ref, a, POOL_HALO + r0)

        def a_shift(s, cb):
            return ea_ref[cb, POOL_HALO + r0 - s:POOL_HALO + r0 - s + sb, :]

        pos1 = (lax.broadcasted_iota(I32, (sb, LANES), 0) + (j * ts + r0 + 1)).astype(F32)
        a_lo = a[:, 0:LANES]
        s01 = a_lo + a_shift(1, 0)
        s03 = s01 + (a_shift(2, 0) + a_shift(3, 0))
        num_lo = jnp.where(first, s01, s03)
        den_lo = jnp.minimum(pos1, jnp.where(first, float(POOL_WINDOWS[0]), float(POOL_WINDOWS[1])))
        a_hi = a[:, LANES:gw]
        s07 = a_hi
        for s in range(1, 8):
            s07 = s07 + a_shift(s, 1)
        s815 = a_shift(8, 1)
        for s in range(9, 16):
            s815 = s815 + a_shift(s, 1)
        num_hi = jnp.where(first, s07, s07 + s815)
        den_hi = jnp.minimum(pos1, jnp.where(first, float(POOL_WINDOWS[2]), float(POOL_WINDOWS[3])))
        d_pool = jnp.concatenate([num_lo / den_lo - a_lo, num_hi / den_hi - a_hi], axis=1)
        y_a = jnp.dot(d_pool.astype(BF16), poolw_ref[0], preferred_element_type=F32) * pools_ref[0]

        glu = p[:, gw:2 * gw] * jax.nn.sigmoid(p[:, 2 * gw:3 * gw])
        _append_rows(eg_ref, glu, CONV_HALO + r0)
        conv_cols = []
        for cb in range(nslab):
            conv_chunks = []
            for rr in range(r0, r0 + sb, CONV_ROWS):
                acc = None
                for k in range(CONV_WIDTH):
                    off = CONV_HALO - (CONV_WIDTH - 1) + k + rr
                    term = (eg_ref[cb, off:off + CONV_ROWS, :]
                            * cdw_ref[0, k:k + 1, cb * LANES:(cb + 1) * LANES])
                    acc = term if acc is None else acc + term
                conv_chunks.append(acc)
            conv_cols.append(jnp.concatenate(conv_chunks, axis=0))
        hb = jnp.concatenate(conv_cols, axis=1) + cdb_ref[0]
        hb = jax.nn.silu(_layer_norm_rows(hb, clg_ref[0], clb_ref[0]))
        y_b = jnp.dot(hb.astype(BF16), cpw_ref[0], preferred_element_type=F32) + cpb_ref[0]

        u = p[:, 3 * gw:4 * gw]
        vln = _layer_norm_rows(p[:, 4 * gw:5 * gw], slg_ref[0], slb_ref[0]).astype(BF16)
        yc_chunks = []
        for n in range(sb // SG_CHUNK):
            vch = vln[n * SG_CHUNK:(n + 1) * SG_CHUNK, :]
            vbd = jnp.concatenate([jnp.where(lane_g == hh, vch, zero_bf) for hh in range(SG_HEADS)],
                                  axis=0)
            mixed = jnp.dot(wcat, vbd, preferred_element_type=F32) + sgb_ref[0]
            yc_chunks.append(u[n * SG_CHUNK:(n + 1) * SG_CHUNK, :] * mixed)
        y_c = jnp.concatenate(yc_chunks, axis=0)

        cx = p[:, 6 * gw:7 * gw] * p[:, 7 * gw:8 * gw]
        _append_rows(ed_ref, cx, SC_HALO + r0)
        cd_cols = []
        for cb in range(nslab):
            ls = slice(cb * LANES, (cb + 1) * LANES)
            at = SC_HALO + r0
            cd_cols.append(cx[:, ls] * scw_ref[0, 2:3, ls]
                           + ed_ref[cb, at - 1:at - 1 + sb, :] * scw_ref[0, 1:2, ls]
                           + ed_ref[cb, at - 2:at - 2 + sb, :] * scw_ref[0, 0:1, ls])
        y_d = p[:, 5 * gw:6 * gw] * jnp.concatenate(cd_cols, axis=1)

        normed = []
        for gi, yg in enumerate((y_a, y_b, y_c, y_d)):
            rg = lax.rsqrt(jnp.mean(yg * yg, axis=-1, keepdims=True) + EPS)
            normed.append((yg * rg * ong_ref[0, :, gi * gw:(gi + 1) * gw]).astype(BF16))
        yn = jnp.concatenate(normed, axis=1)
        x1 = x + g1 * jnp.dot(yn, wout_ref[0], preferred_element_type=F32)
        x1_ref[r0:r0 + sb, :] = x1

        r2 = lax.rsqrt(jnp.mean(x1 * x1, axis=-1, keepdims=True) + EPS)
        h2 = x1 * r2 * gain2 + sh2
        for c in range(REC):
            rec_ref[pl.ds(r0 * REC + c, sb, stride=REC), :] = h2[:, c * LANES:(c + 1) * LANES]
        logits = lax.dot_general(rwt_ref[...], h2.astype(BF16), (((1,), (1,)), ((), ())),
                                 preferred_element_type=F32)
        sel = jax.nn.sigmoid(logits) + rb_ref[...]
        epg = EXPERTS_PER_GROUP
        sel_s = [sel[jj * N_EXPERT_GROUPS:(jj + 1) * N_EXPERT_GROUPS, :] for jj in range(epg)]
        top2 = None
        for ja in range(epg):
            for jb in range(ja + 1, epg):
                pair = sel_s[ja] + sel_s[jb]
                top2 = pair if top2 is None else jnp.maximum(top2, pair)
        gidx, _ = _first_argmax([top2[g:g + 1, :] for g in range(N_EXPERT_GROUPS)])
        sel_c = [_pick(gidx, [sel_s[jj][g:g + 1, :] for g in range(N_EXPERT_GROUPS)])
                 for jj in range(epg)]
        i1, _ = _first_argmax(sel_c)
        i2, _ = _first_argmax([jnp.where(i1 == float(jj), -jnp.inf, sel_c[jj]) for jj in range(epg)])
        lo = jnp.minimum(i1, i2)
        hi = jnp.maximum(i1, i2)
        base = jnp.where(lo == 0.0, 0.0, jnp.where(lo == 1.0, 3.0, 5.0))
        cls = gidx * float(PAIRS_PER_GROUP) + base + (hi - lo - 1.0)
        cls_ref[0, :, r0:r0 + sb] = cls.astype(I32)


def _mixer_layer(l, x, ys_prev, pos_prev, mods, lw, ts, sb, batch):
    t, d = x.shape
    nj = t // batch // ts
    has_prev = ys_prev is not None
    tile = lambda bi, j, *_: (bi * nj + j, 0)

    args = [x]
    in_specs = [pl.BlockSpec((ts, d), tile)]
    if has_prev:
        args += [ys_prev, mods]
        in_specs += [pl.BlockSpec(memory_space=pl.ANY),
                     pl.BlockSpec((1, 1, N_MOD, d), lambda bi, j, *_: (l - 1, bi, 0, 0))]
    args += [mods]
    in_specs += [pl.BlockSpec((1, 1, N_MOD, d), lambda bi, j, *_: (l, bi, 0, 0))]
    for name in ("n1g", "n2g", "w_in", "pool_w", "pool_scale", "conv_dw", "conv_db", "conv_ln_g",
                 "conv_ln_b", "conv_pw", "conv_pb", "sg_ln_g", "sg_ln_b", "sg_w", "sg_b", "sc_w",
                 "out_norm_g", "w_out"):
        arr = lw[name]
        args.append(arr)
        in_specs.append(pl.BlockSpec((1,) + arr.shape[1:],
                                     lambda bi, j, *_, nd=arr.ndim: (l,) + (0,) * (nd - 1)))
    for name in ("router_wt", "router_b"):
        arr = lw[name]
        args.append(arr)
        in_specs.append(pl.BlockSpec(arr.shape, lambda bi, j, *_: (0, 0)))

    out_shape = (jax.ShapeDtypeStruct((t, d), F32),
                 jax.ShapeDtypeStruct((t * REC, LANES), F32),
                 jax.ShapeDtypeStruct((t // ts, 1, ts), I32))
    out_specs = (pl.BlockSpec((ts, d), tile),
                 pl.BlockSpec((ts * REC, LANES), tile),
                 pl.BlockSpec((1, 1, ts), lambda bi, j, *_: (bi * nj + j, 0, 0)))
    nslab = GROUP_W // LANES
    scratch = [pltpu.VMEM((nslab, POOL_HALO + ts, LANES), F32),
               pltpu.VMEM((nslab, CONV_HALO + ts, LANES), F32),
               pltpu.VMEM((nslab, SC_HALO + ts, LANES), F32)]
    if has_prev:
        scratch += [pltpu.VMEM((2, ts * REC, LANES), F32), pltpu.SemaphoreType.DMA((2,))]
        args = [pos_prev] + args
    grid_spec = pltpu.PrefetchScalarGridSpec(
        num_scalar_prefetch=1 if has_prev else 0,
        grid=(batch, nj),
        in_specs=in_specs,
        out_specs=out_specs,
        scratch_shapes=scratch,
    )
    return pl.pallas_call(
        functools.partial(_mixer_body, has_prev, ts, sb),
        grid_spec=grid_spec,
        out_shape=out_shape,
        compiler_params=pltpu.CompilerParams(
            dimension_semantics=("arbitrary", "arbitrary"), vmem_limit_bytes=VMEM_LIMIT),
        name="mixer_layer",
    )(*args)


def _plan_body(tm, cls_ref, pos_ref, meta_ref):
    cls = cls_ref[...]
    nrow = cls.shape[0]
    ii = lax.broadcasted_iota(I32, (LANES, LANES), 0)
    jj = lax.broadcasted_iota(I32, (LANES, LANES), 1)
    upper = jnp.where(ii <= jj, 1.0, 0.0).astype(BF16)
    ri = lax.broadcasted_iota(I32, (nrow, nrow), 0)
    rj = lax.broadcasted_iota(I32, (nrow, nrow), 1)
    lower = jnp.where(rj < ri, 1.0, 0.0).astype(BF16)
    lane_i = lax.broadcasted_iota(I32, (1, LANES), 1)
    tile_i = lane_i.astype(F32)
    start = jnp.zeros((1, 1), F32)
    cum_tiles = jnp.zeros((1, 1), F32)
    pos = jnp.zeros((nrow, LANES), F32)
    tclass = jnp.zeros((1, LANES), F32)
    seg_end = jnp.zeros((1, LANES), F32)
    for k in range(N_CLASSES):
        hit = cls == k
        oh = jnp.where(hit, 1.0, 0.0)
        ohb = oh.astype(BF16)
        incl = jnp.dot(ohb, upper, preferred_element_type=F32)
        above = jnp.sum(jnp.dot(lower, ohb, preferred_element_type=F32), axis=1, keepdims=True)
        rank = incl - oh + above
        pos = pos + jnp.where(hit, start + rank, 0.0)
        cnt = jnp.sum(jnp.sum(oh, axis=1, keepdims=True), axis=0, keepdims=True)
        n_tiles = jnp.floor((cnt + float(tm - 1)) * (1.0 / tm))
        start = start + n_tiles * float(tm)
        cum_tiles = cum_tiles + n_tiles
        tclass = tclass + jnp.where(tile_i >= cum_tiles, 1.0, 0.0)
        seg_end = jnp.where(lane_i == k, start, seg_end)
    pos_ref[...] = pos.astype(I32)

    valid = tclass < float(N_CLASSES)
    kc = jnp.minimum(tclass, float(N_CLASSES - 1))
    grp = jnp.zeros_like(kc)
    for g in range(1, N_EXPERT_GROUPS):
        grp = grp + jnp.where(kc >= float(g * PAIRS_PER_GROUP), 1.0, 0.0)
    pr = kc - grp * float(PAIRS_PER_GROUP)
    lo = jnp.where(pr >= 3.0, 1.0, 0.0) + jnp.where(pr >= 5.0, 1.0, 0.0)
    base = jnp.where(lo == 0.0, 0.0, jnp.where(lo == 1.0, 3.0, 5.0))
    hi = pr - base + lo + 1.0
    rows = {META_EA: grp * float(EXPERTS_PER_GROUP) + lo,
            META_EB: grp * float(EXPERTS_PER_GROUP) + hi,
            META_VALID: jnp.where(valid, 1.0, 0.0),
            META_BLK: jnp.minimum(tile_i, cum_tiles - 1.0),
            META_END: seg_end}
    rid = lax.broadcasted_iota(I32, (SUBLANES, LANES), 0)
    meta = jnp.zeros((SUBLANES, LANES), F32)
    for k, row in rows.items():
        meta = jnp.where(rid == k, row, meta)
    meta_ref[...] = meta.astype(I32)


def _route_plan(cls2d, tm):
    nrow = cls2d.shape[0]
    return pl.pallas_call(
        functools.partial(_plan_body, tm),
        in_specs=[pl.BlockSpec((nrow, LANES), lambda: (0, 0))],
        out_specs=(pl.BlockSpec((nrow, LANES), lambda: (0, 0)),
                   pl.BlockSpec((SUBLANES, LANES), lambda: (0, 0))),
        out_shape=(jax.ShapeDtypeStruct((nrow, LANES), I32),
                   jax.ShapeDtypeStruct((SUBLANES, LANES), I32)),
        name="route_plan",
    )(cls2d)


def _dispatch_body(tm, nt, pos_ref, end_ref, valid_ref, src_ref, dst_hbm, zbuf, zsem, sem):
    g = pl.program_id(0)
    n = PERMUTE_ROWS

    @pl.when(g == 0)
    def _():
        zbuf[...] = jnp.zeros_like(zbuf)

        def zero_tile(first_slot, wait):
            cp = pltpu.make_async_copy(
                zbuf, dst_hbm.at[pl.ds(pl.multiple_of(first_slot * REC, REC), tm * REC), :], zsem)
            cp.wait() if wait else cp.start()

        for wait in (False, True):
            for k in range(N_CLASSES):
                prev_end = end_ref[k - 1] if k > 0 else 0

                @pl.when(end_ref[k] > prev_end)
                def _():
                    zero_tile(end_ref[k] - tm, wait)

            for i in range(nt):
                @pl.when(valid_ref[i] == 0)
                def _():
                    zero_tile(i * tm, wait)

    base = g * n
    for r0 in range(0, n, PERMUTE_GROUP):
        slots = [pos_ref[base + r0 + k] for k in range(PERMUTE_GROUP)]
        for k, slot in enumerate(slots):
            pltpu.make_async_copy(
                src_ref.at[pl.ds((r0 + k) * REC, REC), :],
                dst_hbm.at[pl.ds(pl.multiple_of(slot * REC, REC), REC), :],
                sem).start(priority=k % 2)

    pltpu.make_async_copy(src_ref, dst_hbm.at[pl.ds(0, n * REC), :], sem).wait()


def _row_dispatch(rec, pos, meta, tm, nt):
    n_tok = pos.shape[0]
    grid_spec = pltpu.PrefetchScalarGridSpec(
        num_scalar_prefetch=3,
        grid=(n_tok // PERMUTE_ROWS,),
        in_specs=[pl.BlockSpec((PERMUTE_ROWS * REC, LANES), lambda g, pos, end, valid: (g, 0))],
        out_specs=pl.BlockSpec(memory_space=pl.ANY),
        scratch_shapes=[pltpu.VMEM((tm * REC, LANES), rec.dtype), pltpu.SemaphoreType.DMA(()),
                        pltpu.SemaphoreType.DMA(())],
    )
    return pl.pallas_call(
        functools.partial(_dispatch_body, tm, nt),
        grid_spec=grid_spec,
        out_shape=jax.ShapeDtypeStruct((nt * tm * REC, LANES), rec.dtype),
        compiler_params=pltpu.CompilerParams(
            dimension_semantics=("arbitrary",), vmem_limit_bytes=VMEM_LIMIT),
        name="row_dispatch",
    )(pos, meta[META_END, :N_CLASSES], meta[META_VALID, :nt], rec)


def _moe_body(tm, l, ea_ref, eb_ref, valid_ref, blk_ref, h_ref, rw_ref, wg_hbm, wu_hbm, wd_hbm, y_ref,
              wg_ref, wu_ref, wd_ref, sg_ref, su_ref, sd_ref, wsem):
    i = pl.program_id(0)

    @pl.when(i == 0)
    def _():
        def copies(e, buf):
            return (pltpu.make_async_copy(wg_hbm.at[l, e], sg_ref.at[buf], wsem.at[buf]),
                    pltpu.make_async_copy(wu_hbm.at[l, e], su_ref.at[buf], wsem.at[buf]),
                    pltpu.make_async_copy(wd_hbm.at[l, e], sd_ref.at[buf], wsem.at[buf]))

        for cp in copies(0, 0):
            cp.start()

        def stage(e, carry):
            buf = e % 2

            @pl.when(e + 1 < N_EXPERTS)
            def _():
                for cp in copies(e + 1, 1 - buf):
                    cp.start()

            for cp in copies(e, buf):
                cp.wait()
            wg_ref[e] = sg_ref[buf].astype(BF16)
            wu_ref[e] = su_ref[buf].astype(BF16)
            wd_ref[e] = sd_ref[buf].astype(BF16)
            return carry

        lax.fori_loop(0, N_EXPERTS, stage, 0)

    @pl.when(valid_ref[i] == 1)
    def _():
        ea = ea_ref[i]
        eb = eb_ref[i]
        xb = jnp.concatenate([_rows_of(h_ref, c, tm) for c in range(REC)], axis=1).astype(BF16)
        xf = xb.astype(F32)

        def expert(e):
            score = jax.nn.sigmoid(jnp.sum(xf * rw_ref[e].astype(F32), axis=-1, keepdims=True))
            hg = jnp.dot(xb, wg_ref[e], preferred_element_type=F32)
            hu = jnp.dot(xb, wu_ref[e], preferred_element_type=F32)
            return score, jax.nn.silu(hg) * hu

        sa, ga = expert(ea)
        sb, gb = expert(eb)
        acta = ga * (sa / (sa + sb))
        actb = gb * (sb / (sa + sb))
        y = (jnp.dot(acta.astype(BF16), wd_ref[ea], preferred_element_type=F32)
             + jnp.dot(actb.astype(BF16), wd_ref[eb], preferred_element_type=F32))
        for c in range(REC):
            y_ref[pl.ds(c, tm, stride=REC), :] = y[:, c * LANES:(c + 1) * LANES]

    @pl.when(valid_ref[i] == 0)
    def _():
        y_ref[...] = jnp.zeros_like(y_ref)


def _moe_tiles(l, hs, meta, router_rows, wg, wu, wd, tm, nt):
    d = D_MODEL
    whole = pl.BlockSpec(memory_space=pl.ANY)
    grid_spec = pltpu.PrefetchScalarGridSpec(
        num_scalar_prefetch=4,
        grid=(nt,),
        in_specs=[
            pl.BlockSpec((tm * REC, LANES), lambda i, ea, eb, valid, blk: (blk[i], 0)),
            pl.BlockSpec((N_EXPERTS, 1, d), lambda i, ea, eb, valid, blk: (0, 0, 0),
                         pipeline_mode=pl.Buffered(1)),
            whole, whole, whole,
        ],
        out_specs=pl.BlockSpec((tm * REC, LANES), lambda i, ea, eb, valid, blk: (i, 0)),
        scratch_shapes=[pltpu.VMEM((N_EXPERTS, d, D_EXPERT), BF16),
                        pltpu.VMEM((N_EXPERTS, d, D_EXPERT), BF16),
                        pltpu.VMEM((N_EXPERTS, D_EXPERT, d), BF16),
                        pltpu.VMEM((2, d, D_EXPERT), F32),
                        pltpu.VMEM((2, d, D_EXPERT), F32),
                        pltpu.VMEM((2, D_EXPERT, d), F32),
                        pltpu.SemaphoreType.DMA((2,))],
    )
    return pl.pallas_call(
        functools.partial(_moe_body, tm, l),
        grid_spec=grid_spec,
        out_shape=jax.ShapeDtypeStruct((nt * tm * REC, LANES), F32),
        compiler_params=pltpu.CompilerParams(
            dimension_semantics=("arbitrary",), vmem_limit_bytes=VMEM_LIMIT),
        name="moe_tiles",
    )(meta[META_EA, :nt], meta[META_EB, :nt], meta[META_VALID, :nt], meta[META_BLK, :nt],
      hs, router_rows, wg, wu, wd)


def _final_body(ts, pos_ref, x_ref, ys_hbm, mod_ref, g_ref, o_ref, ybuf, gsem):
    y_ref = _gather_moe_rows(pos_ref, ys_hbm, ybuf, gsem, ts)
    y = jnp.concatenate([_rows_of(y_ref, c, ts) for c in range(REC)], axis=1)
    x = x_ref[...] + mod_ref[0, 0, 5:6, :] * y
    r = lax.rsqrt(jnp.mean(x * x, axis=-1, keepdims=True) + EPS)
    o_ref[...] = x * r * g_ref[...]


def _final_norm(x, ys, pos, mods, final_g, ts, batch):
    t, d = x.shape
    nj = t // batch // ts
    last = mods.shape[0] - 1
    tile = lambda bi, j, pos: (bi * nj + j, 0)
    grid_spec = pltpu.PrefetchScalarGridSpec(
        num_scalar_prefetch=1,
        grid=(batch, nj),
        in_specs=[pl.BlockSpec((ts, d), tile), pl.BlockSpec(memory_space=pl.ANY),
                  pl.BlockSpec((1, 1, N_MOD, d), lambda bi, j, pos: (last, bi, 0, 0)),
                  pl.BlockSpec((1, d), lambda bi, j, pos: (0, 0))],
        out_specs=pl.BlockSpec((ts, d), tile),
        scratch_shapes=[pltpu.VMEM((2, ts * REC, LANES), F32), pltpu.SemaphoreType.DMA((2,))],
    )
    return pl.pallas_call(
        functools.partial(_final_body, ts),
        grid_spec=grid_spec,
        out_shape=jax.ShapeDtypeStruct((t, d), F32),
        compiler_params=pltpu.CompilerParams(
            dimension_semantics=("arbitrary", "arbitrary"), vmem_limit_bytes=VMEM_LIMIT),
        name="final_norm",
    )(pos, x, ys, mods, final_g.reshape(1, d))


def _stacked_weights(norm1_g, norm2_g, w_in, pool_w, pool_scale, conv_dw, conv_db, conv_ln_g,
                     conv_ln_b, conv_pw, conv_pb, sg_ln_g, sg_ln_b, sg_w, sg_b, sc_w, out_norm_g, w_out):
    depth = w_in.shape[0]
    row = lambda v: v.reshape(depth, 1, -1)
    pool_bd = jnp.zeros((depth, GROUP_W, GROUP_W), pool_w.dtype)
    for i in range(len(POOL_WINDOWS)):
        pool_bd = pool_bd.at[:, i * POOL_CG:(i + 1) * POOL_CG, i * POOL_CG:(i + 1) * POOL_CG].set(
            pool_w[:, i])
    return {
        "n1g": row(norm1_g), "n2g": row(norm2_g),
        "w_in": w_in.astype(BF16),
        "pool_w": pool_bd.astype(BF16),
        "pool_scale": row(pool_scale),
        "conv_dw": jnp.pad(conv_dw, ((0, 0), (0, 1), (0, 0))),
        "conv_db": row(conv_db), "conv_ln_g": row(conv_ln_g), "conv_ln_b": row(conv_ln_b),
        "conv_pw": conv_pw.astype(BF16), "conv_pb": row(conv_pb),
        "sg_ln_g": row(sg_ln_g), "sg_ln_b": row(sg_ln_b),
        "sg_w": sg_w,
        "sg_b": jnp.repeat(jnp.swapaxes(sg_b, 1, 2), GROUP_W // SG_HEADS, axis=2),
        "sc_w": jnp.pad(sc_w, ((0, 0), (0, SUBLANES - SC_WIDTH), (0, 0))),
        "out_norm_g": row(out_norm_g),
        "w_out": w_out.astype(BF16),
    }


def kernel(x, c, norm1_g, norm2_g, w_ada, b_ada, w_in, pool_w, pool_scale, conv_dw, conv_db, conv_ln_g, conv_ln_b, conv_pw, conv_pb, sg_ln_g, sg_ln_b, sg_w, sg_b, sc_w, out_norm_g, w_out, router_w, router_bias, exp_w_gate, exp_w_up, exp_w_down, final_g):
    b, s, d = x.shape
    depth = w_in.shape[0]
    t = b * s
    ts = 512
    ts_first = 1024
    sb = 256
    tm = 256
    nt = t // tm + N_CLASSES
    assert d == D_MODEL and t % LANES == 0 and s % ts_first == 0 and t % tm == 0 and nt <= LANES
    assert t % PERMUTE_ROWS == 0 and ts_first % ts == 0 and ts % sb == 0 and sb % SG_CHUNK == 0

    mods = _ada_mod(c, w_ada, b_ada)

    perm = lambda v: v.reshape(N_EXPERT_GROUPS, EXPERTS_PER_GROUP, -1).transpose(1, 0, 2).reshape(
        N_EXPERTS, -1)
    lw = _stacked_weights(norm1_g, norm2_g, w_in, pool_w, pool_scale, conv_dw, conv_db, conv_ln_g,
                          conv_ln_b, conv_pw, conv_pb, sg_ln_g, sg_ln_b, sg_w, sg_b, sc_w, out_norm_g,
                          w_out)
    lw["router_wt"] = perm(router_w.T).astype(BF16)
    lw["router_b"] = perm(router_bias.reshape(N_EXPERTS, 1))
    router_rows = router_w.T.astype(BF16).reshape(N_EXPERTS, 1, d)

    xcur, ys, pos = x.reshape(t, d), None, None
    for l in range(depth):
        xcur, rec, cls = _mixer_layer(l, xcur, ys, pos, mods, lw, ts if l else ts_first, sb, b)
        pos2d, meta = _route_plan(cls.reshape(t // LANES, LANES), tm)
        pos = pos2d.reshape(t)
        hs = _row_dispatch(rec, pos, meta, tm, nt)
        ys = _moe_tiles(l, hs, meta, router_rows, exp_w_gate, exp_w_up, exp_w_down, tm, nt)
    return _final_norm(xcur, ys, pos, mods, final_g, ts, b).reshape(b, s, d)
```

```python
import functools

import jax
import jax.numpy as jnp
from jax import lax
from jax.experimental import pallas as pl
from jax.experimental.pallas import tpu as pltpu

D_MODEL = 1024
GROUP_W = 256
POOL_WINDOWS = (2, 4, 8, 16)
POOL_CG = 64
CONV_WIDTH = 31
SG_CHUNK = 128
SG_HEADS = 4
SC_WIDTH = 3
N_EXPERTS = 16
N_EXPERT_GROUPS = 4
EXPERTS_PER_GROUP = 4
PAIRS_PER_GROUP = 6
N_CLASSES = N_EXPERT_GROUPS * PAIRS_PER_GROUP
D_EXPERT = 256
N_MOD = 6
EPS = 1e-6
LN_EPS = 1e-5

LANES = 128
SUBLANES = 8
REC = SUBLANES
CONV_HALO = 32
POOL_HALO = 16
SC_HALO = 8
CONV_ROWS = 32
META_EA, META_EB, META_VALID, META_BLK, META_END = 0, 1, 2, 3, 4
PERMUTE_ROWS = 1024
PERMUTE_GROUP = 16
VMEM_LIMIT = 56 * 1024 * 1024

F32 = jnp.float32
BF16 = jnp.bfloat16
I32 = jnp.int32


def _rows_of(ref, c, n, first=0):
    return ref[pl.ds(first * REC + c, n, stride=REC), :]


def _ada_body(c_ref, w_ref, b_ref, o_ref):
    ca = jax.nn.silu(c_ref[...])
    o_ref[0] = jnp.dot(ca.astype(BF16), w_ref[0].astype(BF16), preferred_element_type=F32) + b_ref[0]


def _ada_mod(c, w_ada, b_ada):
    depth, d, n = w_ada.shape
    b = c.shape[0]
    tn = 1024
    cp = jnp.pad(c, ((0, SUBLANES - b), (0, 0)))
    out = pl.pallas_call(
        _ada_body,
        grid=(depth, n // tn),
        in_specs=[
            pl.BlockSpec((SUBLANES, d), lambda l, j: (0, 0)),
            pl.BlockSpec((1, d, tn), lambda l, j: (l, 0, j)),
            pl.BlockSpec((1, 1, tn), lambda l, j: (l, 0, j)),
        ],
        out_specs=pl.BlockSpec((1, SUBLANES, tn), lambda l, j: (l, 0, j)),
        out_shape=jax.ShapeDtypeStruct((depth, SUBLANES, n), F32),
        compiler_params=pltpu.CompilerParams(
            dimension_semantics=("arbitrary", "arbitrary"), vmem_limit_bytes=VMEM_LIMIT),
        name="ada_mod",
    )(cp, w_ada, b_ada.reshape(depth, 1, n))
    return out[:, :b].reshape(depth, b, N_MOD, d)


def _layer_norm_rows(v, g, b):
    mu = jnp.mean(v, axis=-1, keepdims=True)
    vc = v - mu
    var = jnp.mean(vc * vc, axis=-1, keepdims=True)
    return vc * lax.rsqrt(var + LN_EPS) * g + b


def _carry_tail(ext_ref, j, ts, halo):
    nslab = ext_ref.shape[0]

    @pl.when(j == 0)
    def _():
        ext_ref[:, 0:halo, :] = jnp.zeros((nslab, halo, LANES), F32)

    @pl.when(j > 0)
    def _():
        ext_ref[:, 0:halo, :] = ext_ref[:, ts:ts + halo, :]


def _append_rows(ext_ref, cur, first):
    for cb in range(ext_ref.shape[0]):
        ext_ref[cb, first:first + cur.shape[0], :] = cur[:, cb * LANES:(cb + 1) * LANES]


def _first_argmax(vals):
    best = vals[0]
    idx = jnp.zeros_like(best)
    for k in range(1, len(vals)):
        better = vals[k] > best
        idx = jnp.where(better, float(k), idx)
        best = jnp.where(better, vals[k], best)
    return idx, best


def _pick(idx, vals):
    out = vals[0]
    for k in range(1, len(vals)):
        out = jnp.where(idx == float(k), vals[k], out)
    return out


def _gather_moe_rows(pos_ref, ys_hbm, ybuf, gsem, ts):
    g = pl.program_id(0) * pl.num_programs(1) + pl.program_id(1)
    n_steps = pl.num_programs(0) * pl.num_programs(1)

    def start(step, buf):
        base_row = step * (ts // LANES)
        for r0 in range(0, ts, PERMUTE_GROUP):
            slots = [pos_ref[base_row + (r0 + k) // LANES, (r0 + k) % LANES]
                     for k in range(PERMUTE_GROUP)]
            for k, slot in enumerate(slots):
                pltpu.make_async_copy(
                    ys_hbm.at[pl.ds(pl.multiple_of(slot * REC, REC), REC), :],
                    ybuf.at[buf, pl.ds((r0 + k) * REC, REC), :],
                    gsem.at[buf]).start(priority=k % 2)

    @pl.when(g == 0)
    def _():
        start(0, 0)

    @pl.when(g + 1 < n_steps)
    def _():
        start(g + 1, (g + 1) % 2)

    cur = g % 2
    pltpu.make_async_copy(ys_hbm.at[pl.ds(0, ts * REC), :], ybuf.at[cur], gsem.at[cur]).wait()
    return ybuf.at[cur]


def _mixer_body(has_prev, ts, sb, *refs):
    refs = list(refs)
    if has_prev:
        pos_ref = refs.pop(0)
    x_ref = refs.pop(0)
    if has_prev:
        ys_hbm = refs.pop(0)
        modp_ref = refs.pop(0)
        gsem = refs.pop()
        ybuf = refs.pop()
    (mod_ref, n1g_ref, n2g_ref, win_ref, poolw_ref, pools_ref,
     cdw_ref, cdb_ref, clg_ref, clb_ref, cpw_ref, cpb_ref,
     slg_ref, slb_ref, sgw_ref, sgb_ref, scw_ref, ong_ref, wout_ref, rwt_ref, rb_ref,
     x1_ref, rec_ref, cls_ref,
     ea_ref, eg_ref, ed_ref) = refs

    j = pl.program_id(1)
    gw = GROUP_W
    nslab = gw // LANES
    if has_prev:
        yprev_ref = _gather_moe_rows(pos_ref, ys_hbm, ybuf, gsem, ts)

    sh1 = mod_ref[0, 0, 0:1, :]
    sc1 = mod_ref[0, 0, 1:2, :]
    g1 = mod_ref[0, 0, 2:3, :]
    sh2 = mod_ref[0, 0, 3:4, :]
    sc2 = mod_ref[0, 0, 4:5, :]
    gain1 = n1g_ref[0] * (1.0 + sc1)
    gain2 = n2g_ref[0] * (1.0 + sc2)

    _carry_tail(ea_ref, j, ts, POOL_HALO)
    _carry_tail(eg_ref, j, ts, CONV_HALO)
    _carry_tail(ed_ref, j, ts, SC_HALO)

    rowi = lax.broadcasted_iota(I32, (SG_CHUNK, SG_CHUNK), 0)
    coli = lax.broadcasted_iota(I32, (SG_CHUNK, SG_CHUNK), 1)
    tril = coli <= rowi
    wcat = jnp.concatenate(
        [jnp.where(tril, sgw_ref[0, hh], 0.0) for hh in range(SG_HEADS)], axis=1).astype(BF16)
    lane_g = lax.broadcasted_iota(I32, (SG_CHUNK, gw), 1) // (gw // SG_HEADS)
    zero_bf = jnp.zeros((SG_CHUNK, gw), BF16)
    lane = lax.broadcasted_iota(I32, (sb, LANES), 1)
    first = lane < POOL_CG

    for r0 in range(0, ts, sb):
        x = x_ref[r0:r0 + sb, :]
        if has_prev:
            yprev = jnp.concatenate([_rows_of(yprev_ref, c, sb, r0) for c in range(REC)], axis=1)
            x = x + modp_ref[0, 0, 5:6, :] * yprev

        r1 = lax.rsqrt(jnp.mean(x * x, axis=-1, keepdims=True) + EPS)
        h = x * r1 * gain1 + sh1
        p = jnp.dot(h.astype(BF16), win_ref[0], preferred_element_type=F32)

        a = p[:, 0:gw]
        _append_rows(ea_ref, a, POOL_HALO + r0)

        def a_shift(s, cb):
            return ea_ref[cb, POOL_HALO + r0 - s:POOL_HALO + r0 - s + sb, :]

        pos1 = (lax.broadcasted_iota(I32, (sb, LANES), 0) + (j * ts + r0 + 1)).astype(F32)
        a_lo = a[:, 0:LANES]
        s01 = a_lo + a_shift(1, 0)
        s03 = s01 + (a_shift(2, 0) + a_shift(3, 0))
        num_lo = jnp.where(first, s01, s03)
        den_lo = jnp.minimum(pos1, jnp.where(first, float(POOL_WINDOWS[0]), float(POOL_WINDOWS[1])))
        a_hi = a[:, LANES:gw]
        s07 = a_hi
        for s in range(1, 8):
            s07 = s07 + a_shift(s, 1)
        s815 = a_shift(8, 1)
        for s in range(9, 16):
            s815 = s815 + a_shift(s, 1)
        num_hi = jnp.where(first, s07, s07 + s815)
        den_hi = jnp.minimum(pos1, jnp.where(first, float(POOL_WINDOWS[2]), float(POOL_WINDOWS[3])))
        d_pool = jnp.concatenate([num_lo / den_lo - a_lo, num_hi / den_hi - a_hi], axis=1)
        y_a = jnp.dot(d_pool.astype(BF16), poolw_ref[0], preferred_element_type=F32) * pools_ref[0]

        glu = p[:, gw:2 * gw] * jax.nn.sigmoid(p[:, 2 * gw:3 * gw])
        _append_rows(eg_ref, glu, CONV_HALO + r0)
        conv_cols = []
        for cb in range(nslab):
            conv_chunks = []
            for rr in range(r0, r0 + sb, CONV_ROWS):
                acc = None
                for k in range(CONV_WIDTH):
                    off = CONV_HALO - (CONV_WIDTH - 1) + k + rr
                    term = (eg_ref[cb, off:off + CONV_ROWS, :]
                            * cdw_ref[0, k:k + 1, cb * LANES:(cb + 1) * LANES])
                    acc = term if acc is None else acc + term
                conv_chunks.append(acc)
            conv_cols.append(jnp.concatenate(conv_chunks, axis=0))
        hb = jnp.concatenate(conv_cols, axis=1) + cdb_ref[0]
        hb = jax.nn.silu(_layer_norm_rows(hb, clg_ref[0], clb_ref[0]))
        y_b = jnp.dot(hb.astype(BF16), cpw_ref[0], preferred_element_type=F32) + cpb_ref[0]

        u = p[:, 3 * gw:4 * gw]
        vln = _layer_norm_rows(p[:, 4 * gw:5 * gw], slg_ref[0], slb_ref[0]).astype(BF16)
        yc_chunks = []
        for n in range(sb // SG_CHUNK):
            vch = vln[n * SG_CHUNK:(n + 1) * SG_CHUNK, :]
            vbd = jnp.concatenate([jnp.where(lane_g == hh, vch, zero_bf) for hh in range(SG_HEADS)],
                                  axis=0)
            mixed = jnp.dot(wcat, vbd, preferred_element_type=F32) + sgb_ref[0]
            yc_chunks.append(u[n * SG_CHUNK:(n + 1) * SG_CHUNK, :] * mixed)
        y_c = jnp.concatenate(yc_chunks, axis=0)

        cx = p[:, 6 * gw:7 * gw] * p[:, 7 * gw:8 * gw]
        _append_rows(ed_ref, cx, SC_HALO + r0)
        cd_cols = []
        for cb in range(nslab):
            ls = slice(cb * LANES, (cb + 1) * LANES)
            at = SC_HALO + r0
            cd_cols.append(cx[:, ls] * scw_ref[0, 2:3, ls]
                           + ed_ref[cb, at - 1:at - 1 + sb, :] * scw_ref[0, 1:2, ls]
                           + ed_ref[cb, at - 2:at - 2 + sb, :] * scw_ref[0, 0:1, ls])
        y_d = p[:, 5 * gw:6 * gw] * jnp.concatenate(cd_cols, axis=1)

        normed = []
        for gi, yg in enumerate((y_a, y_b, y_c, y_d)):
            rg = lax.rsqrt(jnp.mean(yg * yg, axis=-1, keepdims=True) + EPS)
            normed.append((yg * rg * ong_ref[0, :, gi * gw:(gi + 1) * gw]).astype(BF16))
        yn = jnp.concatenate(normed, axis=1)
        x1 = x + g1 * jnp.dot(yn, wout_ref[0], preferred_element_type=F32)
        x1_ref[r0:r0 + sb, :] = x1

        r2 = lax.rsqrt(jnp.mean(x1 * x1, axis=-1, keepdims=True) + EPS)
        h2 = x1 * r2 * gain2 + sh2
        for c in range(REC):
            rec_ref[pl.ds(r0 * REC + c, sb, stride=REC), :] = h2[:, c * LANES:(c + 1) * LANES]
        logits = lax.dot_general(rwt_ref[...], h2.astype(BF16), (((1,), (1,)), ((), ())),
                                 preferred_element_type=F32)
        sel = jax.nn.sigmoid(logits) + rb_ref[...]
        epg = EXPERTS_PER_GROUP
        sel_s = [sel[jj * N_EXPERT_GROUPS:(jj + 1) * N_EXPERT_GROUPS, :] for jj in range(epg)]
        top2 = None
        for ja in range(epg):
            for jb in range(ja + 1, epg):
                pair = sel_s[ja] + sel_s[jb]
                top2 = pair if top2 is None else jnp.maximum(top2, pair)
        gidx, _ = _first_argmax([top2[g:g + 1, :] for g in range(N_EXPERT_GROUPS)])
        sel_c = [_pick(gidx, [sel_s[jj][g:g + 1, :] for g in range(N_EXPERT_GROUPS)])
                 for jj in range(epg)]
        i1, _ = _first_argmax(sel_c)
        i2, _ = _first_argmax([jnp.where(i1 == float(jj), -jnp.inf, sel_c[jj]) for jj in range(epg)])
        lo = jnp.minimum(i1, i2)
        hi = jnp.maximum(i1, i2)
        base = jnp.where(lo == 0.0, 0.0, jnp.where(lo == 1.0, 3.0, 5.0))
        cls = gidx * float(PAIRS_PER_GROUP) + base + (hi - lo - 1.0)
        cls_ref[0, :, r0:r0 + sb] = cls.astype(I32)


def _mixer_layer(l, x, ys_prev, pos_prev, mods, lw, ts, sb, batch):
    t, d = x.shape
    nj = t // batch // ts
    has_prev = ys_prev is not None
    tile = lambda bi, j, *_: (bi * nj + j, 0)

    args = [x]
    in_specs = [pl.BlockSpec((ts, d), tile)]
    if has_prev:
        args += [ys_prev, mods]
        in_specs += [pl.BlockSpec(memory_space=pl.ANY),
                     pl.BlockSpec((1, 1, N_MOD, d), lambda bi, j, *_: (l - 1, bi, 0, 0))]
    args += [mods]
    in_specs += [pl.BlockSpec((1, 1, N_MOD, d), lambda bi, j, *_: (l, bi, 0, 0))]
    for name in ("n1g", "n2g", "w_in", "pool_w", "pool_scale", "conv_dw", "conv_db", "conv_ln_g",
                 "conv_ln_b", "conv_pw", "conv_pb", "sg_ln_g", "sg_ln_b", "sg_w", "sg_b", "sc_w",
                 "out_norm_g", "w_out"):
        arr = lw[name]
        args.append(arr)
        in_specs.append(pl.BlockSpec((1,) + arr.shape[1:],
                                     lambda bi, j, *_, nd=arr.ndim: (l,) + (0,) * (nd - 1)))
    for name in ("router_wt", "router_b"):
        arr = lw[name]
        args.append(arr)
        in_specs.append(pl.BlockSpec(arr.shape, lambda bi, j, *_: (0, 0)))

    out_shape = (jax.ShapeDtypeStruct((t, d), F32),
                 jax.ShapeDtypeStruct((t * REC, LANES), F32),
                 jax.ShapeDtypeStruct((t // ts, 1, ts), I32))
    out_specs = (pl.BlockSpec((ts, d), tile),
                 pl.BlockSpec((ts * REC, LANES), tile),
                 pl.BlockSpec((1, 1, ts), lambda bi, j, *_: (bi * nj + j, 0, 0)))
    nslab = GROUP_W // LANES
    scratch = [pltpu.VMEM((nslab, POOL_HALO + ts, LANES), F32),
               pltpu.VMEM((nslab, CONV_HALO + ts, LANES), F32),
               pltpu.VMEM((nslab, SC_HALO + ts, LANES), F32)]
    if has_prev:
        scratch += [pltpu.VMEM((2, ts * REC, LANES), F32), pltpu.SemaphoreType.DMA((2,))]
        args = [pos_prev] + args
    grid_spec = pltpu.PrefetchScalarGridSpec(
        num_scalar_prefetch=1 if has_prev else 0,
        grid=(batch, nj),
        in_specs=in_specs,
        out_specs=out_specs,
        scratch_shapes=scratch,
    )
    return pl.pallas_call(
        functools.partial(_mixer_body, has_prev, ts, sb),
        grid_spec=grid_spec,
        out_shape=out_shape,
        compiler_params=pltpu.CompilerParams(
            dimension_semantics=("arbitrary", "arbitrary"), vmem_limit_bytes=VMEM_LIMIT),
        name="mixer_layer",
    )(*args)


def _plan_body(tm, cls_ref, pos_ref, meta_ref):
    cls = cls_ref[...]
    nrow = cls.shape[0]
    ii = lax.broadcasted_iota(I32, (LANES, LANES), 0)
    jj = lax.broadcasted_iota(I32, (LANES, LANES), 1)
    upper = jnp.where(ii <= jj, 1.0, 0.0).astype(BF16)
    ri = lax.broadcasted_iota(I32, (nrow, nrow), 0)
    rj = lax.broadcasted_iota(I32, (nrow, nrow), 1)
    lower = jnp.where(rj < ri, 1.0, 0.0).astype(BF16)
    lane_i = lax.broadcasted_iota(I32, (1, LANES), 1)
    tile_i = lane_i.astype(F32)
    start = jnp.zeros((1, 1), F32)
    cum_tiles = jnp.zeros((1, 1), F32)
    pos = jnp.zeros((nrow, LANES), F32)
    tclass = jnp.zeros((1, LANES), F32)
    seg_end = jnp.zeros((1, LANES), F32)
    for k in range(N_CLASSES):
        hit = cls == k
        oh = jnp.where(hit, 1.0, 0.0)
        ohb = oh.astype(BF16)
        incl = jnp.dot(ohb, upper, preferred_element_type=F32)
        above = jnp.sum(jnp.dot(lower, ohb, preferred_element_type=F32), axis=1, keepdims=True)
        rank = incl - oh + above
        pos = pos + jnp.where(hit, start + rank, 0.0)
        cnt = jnp.sum(jnp.sum(oh, axis=1, keepdims=True), axis=0, keepdims=True)
        n_tiles = jnp.floor((cnt + float(tm - 1)) * (1.0 / tm))
        start = start + n_tiles * float(tm)
        cum_tiles = cum_tiles + n_tiles
        tclass = tclass + jnp.where(tile_i >= cum_tiles, 1.0, 0.0)
        seg_end = jnp.where(lane_i == k, start, seg_end)
    pos_ref[...] = pos.astype(I32)

    valid = tclass < float(N_CLASSES)
    kc = jnp.minimum(tclass, float(N_CLASSES - 1))
    grp = jnp.zeros_like(kc)
    for g in range(1, N_EXPERT_GROUPS):
        grp = grp + jnp.where(kc >= float(g * PAIRS_PER_GROUP), 1.0, 0.0)
    pr = kc - grp * float(PAIRS_PER_GROUP)
    lo = jnp.where(pr >= 3.0, 1.0, 0.0) + jnp.where(pr >= 5.0, 1.0, 0.0)
    base = jnp.where(lo == 0.0, 0.0, jnp.where(lo == 1.0, 3.0, 5.0))
    hi = pr - base + lo + 1.0
    rows = {META_EA: grp * float(EXPERTS_PER_GROUP) + lo,
            META_EB: grp * float(EXPERTS_PER_GROUP) + hi,
            META_VALID: jnp.where(valid, 1.0, 0.0),
            META_BLK: jnp.minimum(tile_i, cum_tiles - 1.0),
            META_END: seg_end}
    rid = lax.broadcasted_iota(I32, (SUBLANES, LANES), 0)
    meta = jnp.zeros((SUBLANES, LANES), F32)
    for k, row in rows.items():
        meta = jnp.where(rid == k, row, meta)
    meta_ref[...] = meta.astype(I32)


def _route_plan(cls2d, tm):
    nrow = cls2d.shape[0]
    return pl.pallas_call(
        functools.partial(_plan_body, tm),
        in_specs=[pl.BlockSpec((nrow, LANES), lambda: (0, 0))],
        out_specs=(pl.BlockSpec((nrow, LANES), lambda: (0, 0)),
                   pl.BlockSpec((SUBLANES, LANES), lambda: (0, 0))),
        out_shape=(jax.ShapeDtypeStruct((nrow, LANES), I32),
                   jax.ShapeDtypeStruct((SUBLANES, LANES), I32)),
        name="route_plan",
    )(cls2d)


def _dispatch_body(tm, nt, pos_ref, meta_ref, src_ref, dst_hbm, zbuf, zsem, sem):
    g = pl.program_id(0)
    n = PERMUTE_ROWS

    @pl.when(g == 0)
    def _():
        zbuf[...] = jnp.zeros_like(zbuf)

        def zero_tile(first_slot, wait):
            cp = pltpu.make_async_copy(
                zbuf, dst_hbm.at[pl.ds(pl.multiple_of(first_slot * REC, REC), tm * REC), :], zsem)
            cp.wait() if wait else cp.start()

        for wait in (False, True):
            for k in range(N_CLASSES):
                prev_end = meta_ref[META_END, k - 1] if k > 0 else 0

                @pl.when(meta_ref[META_END, k] > prev_end)
                def _():
                    zero_tile(meta_ref[META_END, k] - tm, wait)

            for i in range(nt):
                @pl.when(meta_ref[META_VALID, i] == 0)
                def _():
                    zero_tile(i * tm, wait)

    base_row = g * (n // LANES)
    for r0 in range(0, n, PERMUTE_GROUP):
        slots = [pos_ref[base_row + (r0 + k) // LANES, (r0 + k) % LANES] for k in range(PERMUTE_GROUP)]
        for k, slot in enumerate(slots):
            pltpu.make_async_copy(
                src_ref.at[pl.ds((r0 + k) * REC, REC), :],
                dst_hbm.at[pl.ds(pl.multiple_of(slot * REC, REC), REC), :],
                sem).start(priority=k % 2)

    pltpu.make_async_copy(src_ref, dst_hbm.at[pl.ds(0, n * REC), :], sem).wait()


def _row_dispatch(rec, pos, meta, tm, nt):
    n_tok = pos.size
    grid_spec = pltpu.PrefetchScalarGridSpec(
        num_scalar_prefetch=2,
        grid=(n_tok // PERMUTE_ROWS,),
        in_specs=[pl.BlockSpec((PERMUTE_ROWS * REC, LANES), lambda g, pos, meta: (g, 0))],
        out_specs=pl.BlockSpec(memory_space=pl.ANY),
        scratch_shapes=[pltpu.VMEM((tm * REC, LANES), rec.dtype), pltpu.SemaphoreType.DMA(()),
                        pltpu.SemaphoreType.DMA(())],
    )
    return pl.pallas_call(
        functools.partial(_dispatch_body, tm, nt),
        grid_spec=grid_spec,
        out_shape=jax.ShapeDtypeStruct((nt * tm * REC, LANES), rec.dtype),
        compiler_params=pltpu.CompilerParams(
            dimension_semantics=("arbitrary",), vmem_limit_bytes=VMEM_LIMIT),
        name="row_dispatch",
    )(pos, meta, rec)


def _moe_body(tm, l, meta_ref, h_ref, rw_ref, wg_hbm, wu_hbm, wd_hbm, y_ref,
              wg_ref, wu_ref, wd_ref, sg_ref, su_ref, sd_ref, wsem):
    i = pl.program_id(0)

    @pl.when(i == 0)
    def _():
        def copies(e, buf):
            return (pltpu.make_async_copy(wg_hbm.at[l, e], sg_ref.at[buf], wsem.at[buf]),
                    pltpu.make_async_copy(wu_hbm.at[l, e], su_ref.at[buf], wsem.at[buf]),
                    pltpu.make_async_copy(wd_hbm.at[l, e], sd_ref.at[buf], wsem.at[buf]))

        for cp in copies(0, 0):
            cp.start()

        def stage(e, carry):
            buf = e % 2

            @pl.when(e + 1 < N_EXPERTS)
            def _():
                for cp in copies(e + 1, 1 - buf):
                    cp.start()

            for cp in copies(e, buf):
                cp.wait()
            wg_ref[e] = sg_ref[buf].astype(BF16)
            wu_ref[e] = su_ref[buf].astype(BF16)
            wd_ref[e] = sd_ref[buf].astype(BF16)
            return carry

        lax.fori_loop(0, N_EXPERTS, stage, 0)

    @pl.when(meta_ref[META_VALID, i] == 1)
    def _():
        ea = meta_ref[META_EA, i]
        eb = meta_ref[META_EB, i]
        xb = jnp.concatenate([_rows_of(h_ref, c, tm) for c in range(REC)], axis=1).astype(BF16)
        xf = xb.astype(F32)

        def expert(e):
            score = jax.nn.sigmoid(jnp.sum(xf * rw_ref[e].astype(F32), axis=-1, keepdims=True))
            hg = jnp.dot(xb, wg_ref[e], preferred_element_type=F32)
            hu = jnp.dot(xb, wu_ref[e], preferred_element_type=F32)
            return score, jax.nn.silu(hg) * hu

        sa, ga = expert(ea)
        sb, gb = expert(eb)
        acta = ga * (sa / (sa + sb))
        actb = gb * (sb / (sa + sb))
        y = (jnp.dot(acta.astype(BF16), wd_ref[ea], preferred_element_type=F32)
             + jnp.dot(actb.astype(BF16), wd_ref[eb], preferred_element_type=F32))
        for c in range(REC):
            y_ref[pl.ds(c, tm, stride=REC), :] = y[:, c * LANES:(c + 1) * LANES]

    @pl.when(meta_ref[META_VALID, i] == 0)
    def _():
        y_ref[...] = jnp.zeros_like(y_ref)


def _moe_tiles(l, hs, meta, router_rows, wg, wu, wd, tm, nt):
    d = D_MODEL
    whole = pl.BlockSpec(memory_space=pl.ANY)
    grid_spec = pltpu.PrefetchScalarGridSpec(
        num_scalar_prefetch=1,
        grid=(nt,),
        in_specs=[
            pl.BlockSpec((tm * REC, LANES), lambda i, meta: (meta[META_BLK, i], 0)),
            pl.BlockSpec((N_EXPERTS, 1, d), lambda i, meta: (0, 0, 0), pipeline_mode=pl.Buffered(1)),
            whole, whole, whole,
        ],
        out_specs=pl.BlockSpec((tm * REC, LANES), lambda i, meta: (i, 0)),
        scratch_shapes=[pltpu.VMEM((N_EXPERTS, d, D_EXPERT), BF16),
                        pltpu.VMEM((N_EXPERTS, d, D_EXPERT), BF16),
                        pltpu.VMEM((N_EXPERTS, D_EXPERT, d), BF16),
                        pltpu.VMEM((2, d, D_EXPERT), F32),
                        pltpu.VMEM((2, d, D_EXPERT), F32),
                        pltpu.VMEM((2, D_EXPERT, d), F32),
                        pltpu.SemaphoreType.DMA((2,))],
    )
    return pl.pallas_call(
        functools.partial(_moe_body, tm, l),
        grid_spec=grid_spec,
        out_shape=jax.ShapeDtypeStruct((nt * tm * REC, LANES), F32),
        compiler_params=pltpu.CompilerParams(
            dimension_semantics=("arbitrary",), vmem_limit_bytes=VMEM_LIMIT),
        name="moe_tiles",
    )(meta, hs, router_rows, wg, wu, wd)


def _final_body(ts, pos_ref, x_ref, ys_hbm, mod_ref, g_ref, o_ref, ybuf, gsem):
    y_ref = _gather_moe_rows(pos_ref, ys_hbm, ybuf, gsem, ts)
    y = jnp.concatenate([_rows_of(y_ref, c, ts) for c in range(REC)], axis=1)
    x = x_ref[...] + mod_ref[0, 0, 5:6, :] * y
    r = lax.rsqrt(jnp.mean(x * x, axis=-1, keepdims=True) + EPS)
    o_ref[...] = x * r * g_ref[...]


def _final_norm(x, ys, pos, mods, final_g, ts, batch):
    t, d = x.shape
    nj = t // batch // ts
    last = mods.shape[0] - 1
    tile = lambda bi, j, pos: (bi * nj + j, 0)
    grid_spec = pltpu.PrefetchScalarGridSpec(
        num_scalar_prefetch=1,
        grid=(batch, nj),
        in_specs=[pl.BlockSpec((ts, d), tile), pl.BlockSpec(memory_space=pl.ANY),
                  pl.BlockSpec((1, 1, N_MOD, d), lambda bi, j, pos: (last, bi, 0, 0)),
                  pl.BlockSpec((1, d), lambda bi, j, pos: (0, 0))],
        out_specs=pl.BlockSpec((ts, d), tile),
        scratch_shapes=[pltpu.VMEM((2, ts * REC, LANES), F32), pltpu.SemaphoreType.DMA((2,))],
    )
    return pl.pallas_call(
        functools.partial(_final_body, ts),
        grid_spec=grid_spec,
        out_shape=jax.ShapeDtypeStruct((t, d), F32),
        compiler_params=pltpu.CompilerParams(
            dimension_semantics=("arbitrary", "arbitrary"), vmem_limit_bytes=VMEM_LIMIT),
        name="final_norm",
    )(pos, x, ys, mods, final_g.reshape(1, d))


def _stacked_weights(norm1_g, norm2_g, w_in, pool_w, pool_scale, conv_dw, conv_db, conv_ln_g,
                     conv_ln_b, conv_pw, conv_pb, sg_ln_g, sg_ln_b, sg_w, sg_b, sc_w, out_norm_g, w_out):
    depth = w_in.shape[0]
    row = lambda v: v.reshape(depth, 1, -1)
    pool_bd = jnp.zeros((depth, GROUP_W, GROUP_W), pool_w.dtype)
    for i in range(len(POOL_WINDOWS)):
        pool_bd = pool_bd.at[:, i * POOL_CG:(i + 1) * POOL_CG, i * POOL_CG:(i + 1) * POOL_CG].set(
            pool_w[:, i])
    return {
        "n1g": row(norm1_g), "n2g": row(norm2_g),
        "w_in": w_in.astype(BF16),
        "pool_w": pool_bd.astype(BF16),
        "pool_scale": row(pool_scale),
        "conv_dw": jnp.pad(conv_dw, ((0, 0), (0, 1), (0, 0))),
        "conv_db": row(conv_db), "conv_ln_g": row(conv_ln_g), "conv_ln_b": row(conv_ln_b),
        "conv_pw": conv_pw.astype(BF16), "conv_pb": row(conv_pb),
        "sg_ln_g": row(sg_ln_g), "sg_ln_b": row(sg_ln_b),
        "sg_w": sg_w,
        "sg_b": jnp.repeat(jnp.swapaxes(sg_b, 1, 2), GROUP_W // SG_HEADS, axis=2),
        "sc_w": jnp.pad(sc_w, ((0, 0), (0, SUBLANES - SC_WIDTH), (0, 0))),
        "out_norm_g": row(out_norm_g),
        "w_out": w_out.astype(BF16),
    }


def kernel(x, c, norm1_g, norm2_g, w_ada, b_ada, w_in, pool_w, pool_scale, conv_dw, conv_db, conv_ln_g, conv_ln_b, conv_pw, conv_pb, sg_ln_g, sg_ln_b, sg_w, sg_b, sc_w, out_norm_g, w_out, router_w, router_bias, exp_w_gate, exp_w_up, exp_w_down, final_g):
    b, s, d = x.shape
    depth = w_in.shape[0]
    t = b * s
    ts = 512
    ts_first = 1024
    sb = 256
    tm = 256
    nt = t // tm + N_CLASSES
    assert d == D_MODEL and t % LANES == 0 and s % ts_first == 0 and t % tm == 0 and nt <= LANES
    assert t % PERMUTE_ROWS == 0 and ts_first % ts == 0 and ts % sb == 0 and sb % SG_CHUNK == 0

    mods = _ada_mod(c, w_ada, b_ada)

    perm = lambda v: v.reshape(N_EXPERT_GROUPS, EXPERTS_PER_GROUP, -1).transpose(1, 0, 2).reshape(
        N_EXPERTS, -1)
    lw = _stacked_weights(norm1_g, norm2_g, w_in, pool_w, pool_scale, conv_dw, conv_db, conv_ln_g,
                          conv_ln_b, conv_pw, conv_pb, sg_ln_g, sg_ln_b, sg_w, sg_b, sc_w, out_norm_g,
                          w_out)
    lw["router_wt"] = perm(router_w.T).astype(BF16)
    lw["router_b"] = perm(router_bias.reshape(N_EXPERTS, 1))
    router_rows = router_w.T.astype(BF16).reshape(N_EXPERTS, 1, d)

    xcur, ys, pos = x.reshape(t, d), None, None
    for l in range(depth):
        xcur, rec, cls = _mixer_layer(l, xcur, ys, pos, mods, lw, ts if l else ts_first, sb, b)
        pos, meta = _route_plan(cls.reshape(t // LANES, LANES), tm)
        hs = _row_dispatch(rec, pos, meta, tm, nt)
        ys = _moe_tiles(l, hs, meta, router_rows, exp_w_gate, exp_w_up, exp_w_down, tm, nt)
    return _final_norm(xcur, ys, pos, mods, final_g, ts, b).reshape(b, s, d)
```

```python
import functools

import jax
import jax.numpy as jnp
from jax import lax
from jax.experimental import pallas as pl
from jax.experimental.pallas import tpu as pltpu

D_MODEL = 1024
GROUP_W = 256
POOL_WINDOWS = (2, 4, 8, 16)
POOL_CG = 64
CONV_WIDTH = 31
SG_CHUNK = 128
SG_HEADS = 4
SC_WIDTH = 3
N_EXPERTS = 16
N_EXPERT_GROUPS = 4
EXPERTS_PER_GROUP = 4
PAIRS_PER_GROUP = 6
N_CLASSES = N_EXPERT_GROUPS * PAIRS_PER_GROUP
D_EXPERT = 256
N_MOD = 6
EPS = 1e-6
LN_EPS = 1e-5

LANES = 128
SUBLANES = 8
REC = SUBLANES
CONV_HALO = 32
POOL_HALO = 16
SC_HALO = 8
CONV_ROWS = 32
META_EA, META_EB, META_VALID, META_BLK, META_END = 0, 1, 2, 3, 4
PERMUTE_ROWS = 1024
PERMUTE_GROUP = 16
VMEM_LIMIT = 56 * 1024 * 1024

F32 = jnp.float32
BF16 = jnp.bfloat16
I32 = jnp.int32


def _rows_of(ref, c, n, first=0):
    return ref[pl.ds(first * REC + c, n, stride=REC), :]


def _ada_body(c_ref, w_ref, b_ref, o_ref):
    ca = jax.nn.silu(c_ref[...])
    o_ref[0] = jnp.dot(ca.astype(BF16), w_ref[0].astype(BF16), preferred_element_type=F32) + b_ref[0]


def _ada_mod(c, w_ada, b_ada):
    depth, d, n = w_ada.shape
    b = c.shape[0]
    tn = 1024
    cp = jnp.pad(c, ((0, SUBLANES - b), (0, 0)))
    out = pl.pallas_call(
        _ada_body,
        grid=(depth, n // tn),
        in_specs=[
            pl.BlockSpec((SUBLANES, d), lambda l, j: (0, 0)),
            pl.BlockSpec((1, d, tn), lambda l, j: (l, 0, j)),
            pl.BlockSpec((1, 1, tn), lambda l, j: (l, 0, j)),
        ],
        out_specs=pl.BlockSpec((1, SUBLANES, tn), lambda l, j: (l, 0, j)),
        out_shape=jax.ShapeDtypeStruct((depth, SUBLANES, n), F32),
        compiler_params=pltpu.CompilerParams(
            dimension_semantics=("arbitrary", "arbitrary"), vmem_limit_bytes=VMEM_LIMIT),
        name="ada_mod",
    )(cp, w_ada, b_ada.reshape(depth, 1, n))
    return out[:, :b].reshape(depth, b, N_MOD, d)


def _layer_norm_rows(v, g, b):
    mu = jnp.mean(v, axis=-1, keepdims=True)
    vc = v - mu
    var = jnp.mean(vc * vc, axis=-1, keepdims=True)
    return vc * lax.rsqrt(var + LN_EPS) * g + b


def _carry_tail(ext_ref, j, ts, halo):
    nslab = ext_ref.shape[0]

    @pl.when(j == 0)
    def _():
        ext_ref[:, 0:halo, :] = jnp.zeros((nslab, halo, LANES), F32)

    @pl.when(j > 0)
    def _():
        ext_ref[:, 0:halo, :] = ext_ref[:, ts:ts + halo, :]


def _append_rows(ext_ref, cur, first):
    for cb in range(ext_ref.shape[0]):
        ext_ref[cb, first:first + cur.shape[0], :] = cur[:, cb * LANES:(cb + 1) * LANES]


def _first_argmax(vals):
    best = vals[0]
    idx = jnp.zeros_like(best)
    for k in range(1, len(vals)):
        better = vals[k] > best
        idx = jnp.where(better, float(k), idx)
        best = jnp.where(better, vals[k], best)
    return idx, best


def _pick(idx, vals):
    out = vals[0]
    for k in range(1, len(vals)):
        out = jnp.where(idx == float(k), vals[k], out)
    return out


def _gather_moe_rows(pos_ref, ys_hbm, ybuf, gsem, ts):
    g = pl.program_id(0) * pl.num_programs(1) + pl.program_id(1)
    n_steps = pl.num_programs(0) * pl.num_programs(1)

    def start(step, buf):
        base_row = step * (ts // LANES)
        for r0 in range(0, ts, PERMUTE_GROUP):
            slots = [pos_ref[base_row + (r0 + k) // LANES, (r0 + k) % LANES]
                     for k in range(PERMUTE_GROUP)]
            for k, slot in enumerate(slots):
                pltpu.make_async_copy(
                    ys_hbm.at[pl.ds(pl.multiple_of(slot * REC, REC), REC), :],
                    ybuf.at[buf, pl.ds((r0 + k) * REC, REC), :],
                    gsem.at[buf]).start(priority=k % 2)

    @pl.when(g == 0)
    def _():
        start(0, 0)

    @pl.when(g + 1 < n_steps)
    def _():
        start(g + 1, (g + 1) % 2)

    cur = g % 2
    pltpu.make_async_copy(ys_hbm.at[pl.ds(0, ts * REC), :], ybuf.at[cur], gsem.at[cur]).wait()
    return ybuf.at[cur]


def _mixer_body(has_prev, ts, sb, *refs):
    refs = list(refs)
    if has_prev:
        pos_ref = refs.pop(0)
    x_ref = refs.pop(0)
    if has_prev:
        ys_hbm = refs.pop(0)
        modp_ref = refs.pop(0)
        gsem = refs.pop()
        ybuf = refs.pop()
    (mod_ref, n1g_ref, n2g_ref, win_ref, poolw_ref, pools_ref,
     cdw_ref, cdb_ref, clg_ref, clb_ref, cpw_ref, cpb_ref,
     slg_ref, slb_ref, sgw_ref, sgb_ref, scw_ref, ong_ref, wout_ref, rwt_ref, rb_ref,
     x1_ref, rec_ref, cls_ref,
     ea_ref, eg_ref, ed_ref) = refs

    j = pl.program_id(1)
    gw = GROUP_W
    nslab = gw // LANES
    if has_prev:
        yprev_ref = _gather_moe_rows(pos_ref, ys_hbm, ybuf, gsem, ts)

    sh1 = mod_ref[0, 0, 0:1, :]
    sc1 = mod_ref[0, 0, 1:2, :]
    g1 = mod_ref[0, 0, 2:3, :]
    sh2 = mod_ref[0, 0, 3:4, :]
    sc2 = mod_ref[0, 0, 4:5, :]
    gain1 = n1g_ref[0] * (1.0 + sc1)
    gain2 = n2g_ref[0] * (1.0 + sc2)

    _carry_tail(ea_ref, j, ts, POOL_HALO)
    _carry_tail(eg_ref, j, ts, CONV_HALO)
    _carry_tail(ed_ref, j, ts, SC_HALO)

    rowi = lax.broadcasted_iota(I32, (SG_CHUNK, SG_CHUNK), 0)
    coli = lax.broadcasted_iota(I32, (SG_CHUNK, SG_CHUNK), 1)
    tril = coli <= rowi
    wcat = jnp.concatenate(
        [jnp.where(tril, sgw_ref[0, hh], 0.0) for hh in range(SG_HEADS)], axis=1).astype(BF16)
    lane_g = lax.broadcasted_iota(I32, (SG_CHUNK, gw), 1) // (gw // SG_HEADS)
    zero_bf = jnp.zeros((SG_CHUNK, gw), BF16)
    lane = lax.broadcasted_iota(I32, (sb, LANES), 1)
    first = lane < POOL_CG

    for r0 in range(0, ts, sb):
        x = x_ref[r0:r0 + sb, :]
        if has_prev:
            yprev = jnp.concatenate([_rows_of(yprev_ref, c, sb, r0) for c in range(REC)], axis=1)
            x = x + modp_ref[0, 0, 5:6, :] * yprev

        r1 = lax.rsqrt(jnp.mean(x * x, axis=-1, keepdims=True) + EPS)
        h = x * r1 * gain1 + sh1
        p = jnp.dot(h.astype(BF16), win_ref[0], preferred_element_type=F32)

        a = p[:, 0:gw]
        _append_rows(ea_ref, a, POOL_HALO + r0)

        def a_shift(s, cb):
            return ea_ref[cb, POOL_HALO + r0 - s:POOL_HALO + r0 - s + sb, :]

        pos1 = (lax.broadcasted_iota(I32, (sb, LANES), 0) + (j * ts + r0 + 1)).astype(F32)
        a_lo = a[:, 0:LANES]
        s01 = a_lo + a_shift(1, 0)
        s03 = s01 + (a_shift(2, 0) + a_shift(3, 0))
        num_lo = jnp.where(first, s01, s03)
        den_lo = jnp.minimum(pos1, jnp.where(first, float(POOL_WINDOWS[0]), float(POOL_WINDOWS[1])))
        a_hi = a[:, LANES:gw]
        s07 = a_hi
        for s in range(1, 8):
            s07 = s07 + a_shift(s, 1)
        s815 = a_shift(8, 1)
        for s in range(9, 16):
            s815 = s815 + a_shift(s, 1)
        num_hi = jnp.where(first, s07, s07 + s815)
        den_hi = jnp.minimum(pos1, jnp.where(first, float(POOL_WINDOWS[2]), float(POOL_WINDOWS[3])))
        d_pool = jnp.concatenate([num_lo / den_lo - a_lo, num_hi / den_hi - a_hi], axis=1)
        y_a = jnp.dot(d_pool.astype(BF16), poolw_ref[0], preferred_element_type=F32) * pools_ref[0]

        glu = p[:, gw:2 * gw] * jax.nn.sigmoid(p[:, 2 * gw:3 * gw])
        _append_rows(eg_ref, glu, CONV_HALO + r0)
        conv_cols = []
        for cb in range(nslab):
            conv_chunks = []
            for rr in range(r0, r0 + sb, CONV_ROWS):
                acc = None
                for k in range(CONV_WIDTH):
                    off = CONV_HALO - (CONV_WIDTH - 1) + k + rr
                    term = (eg_ref[cb, off:off + CONV_ROWS, :]
                            * cdw_ref[0, k:k + 1, cb * LANES:(cb + 1) * LANES])
                    acc = term if acc is None else acc + term
                conv_chunks.append(acc)
            conv_cols.append(jnp.concatenate(conv_chunks, axis=0))
        hb = jnp.concatenate(conv_cols, axis=1) + cdb_ref[0]
        hb = jax.nn.silu(_layer_norm_rows(hb, clg_ref[0], clb_ref[0]))
        y_b = jnp.dot(hb.astype(BF16), cpw_ref[0], preferred_element_type=F32) + cpb_ref[0]

        u = p[:, 3 * gw:4 * gw]
        vln = _layer_norm_rows(p[:, 4 * gw:5 * gw], slg_ref[0], slb_ref[0]).astype(BF16)
        yc_chunks = []
        for n in range(sb // SG_CHUNK):
            vch = vln[n * SG_CHUNK:(n + 1) * SG_CHUNK, :]
            vbd = jnp.concatenate([jnp.where(lane_g == hh, vch, zero_bf) for hh in range(SG_HEADS)],
                                  axis=0)
            mixed = jnp.dot(wcat, vbd, preferred_element_type=F32) + sgb_ref[0]
            yc_chunks.append(u[n * SG_CHUNK:(n + 1) * SG_CHUNK, :] * mixed)
        y_c = jnp.concatenate(yc_chunks, axis=0)

        cx = p[:, 6 * gw:7 * gw] * p[:, 7 * gw:8 * gw]
        _append_rows(ed_ref, cx, SC_HALO + r0)
        cd_cols = []
        for cb in range(nslab):
            ls = slice(cb * LANES, (cb + 1) * LANES)
            at = SC_HALO + r0
            cd_cols.append(cx[:, ls] * scw_ref[0, 2:3, ls]
                           + ed_ref[cb, at - 1:at - 1 + sb, :] * scw_ref[0, 1:2, ls]
                           + ed_ref[cb, at - 2:at - 2 + sb, :] * scw_ref[0, 0:1, ls])
        y_d = p[:, 5 * gw:6 * gw] * jnp.concatenate(cd_cols, axis=1)

        normed = []
        for gi, yg in enumerate((y_a, y_b, y_c, y_d)):
            rg = lax.rsqrt(jnp.mean(yg * yg, axis=-1, keepdims=True) + EPS)
            normed.append((yg * rg * ong_ref[0, :, gi * gw:(gi + 1) * gw]).astype(BF16))
        yn = jnp.concatenate(normed, axis=1)
        x1 = x + g1 * jnp.dot(yn, wout_ref[0], preferred_element_type=F32)
        x1_ref[r0:r0 + sb, :] = x1

        r2 = lax.rsqrt(jnp.mean(x1 * x1, axis=-1, keepdims=True) + EPS)
        h2 = x1 * r2 * gain2 + sh2
        for c in range(REC):
            rec_ref[pl.ds(r0 * REC + c, sb, stride=REC), :] = h2[:, c * LANES:(c + 1) * LANES]
        logits = lax.dot_general(rwt_ref[...], h2.astype(BF16), (((1,), (1,)), ((), ())),
                                 preferred_element_type=F32)
        sel = jax.nn.sigmoid(logits) + rb_ref[...]
        epg = EXPERTS_PER_GROUP
        sel_s = [sel[jj * N_EXPERT_GROUPS:(jj + 1) * N_EXPERT_GROUPS, :] for jj in range(epg)]
        top2 = None
        for ja in range(epg):
            for jb in range(ja + 1, epg):
                pair = sel_s[ja] + sel_s[jb]
                top2 = pair if top2 is None else jnp.maximum(top2, pair)
        gidx, _ = _first_argmax([top2[g:g + 1, :] for g in range(N_EXPERT_GROUPS)])
        sel_c = [_pick(gidx, [sel_s[jj][g:g + 1, :] for g in range(N_EXPERT_GROUPS)])
                 for jj in range(epg)]
        i1, _ = _first_argmax(sel_c)
        i2, _ = _first_argmax([jnp.where(i1 == float(jj), -jnp.inf, sel_c[jj]) for jj in range(epg)])
        lo = jnp.minimum(i1, i2)
        hi = jnp.maximum(i1, i2)
        base = jnp.where(lo == 0.0, 0.0, jnp.where(lo == 1.0, 3.0, 5.0))
        cls = gidx * float(PAIRS_PER_GROUP) + base + (hi - lo - 1.0)
        cls_ref[0, :, r0:r0 + sb] = cls.astype(I32)


def _mixer_layer(l, x, ys_prev, pos_prev, mods, lw, ts, sb, batch):
    t, d = x.shape
    nj = t // batch // ts
    has_prev = ys_prev is not None
    tile = lambda bi, j, *_: (bi * nj + j, 0)

    args = [x]
    in_specs = [pl.BlockSpec((ts, d), tile)]
    if has_prev:
        args += [ys_prev, mods]
        in_specs += [pl.BlockSpec(memory_space=pl.ANY),
                     pl.BlockSpec((1, 1, N_MOD, d), lambda bi, j, *_: (l - 1, bi, 0, 0))]
    args += [mods]
    in_specs += [pl.BlockSpec((1, 1, N_MOD, d), lambda bi, j, *_: (l, bi, 0, 0))]
    for name in ("n1g", "n2g", "w_in", "pool_w", "pool_scale", "conv_dw", "conv_db", "conv_ln_g",
                 "conv_ln_b", "conv_pw", "conv_pb", "sg_ln_g", "sg_ln_b", "sg_w", "sg_b", "sc_w",
                 "out_norm_g", "w_out"):
        arr = lw[name]
        args.append(arr)
        in_specs.append(pl.BlockSpec((1,) + arr.shape[1:],
                                     lambda bi, j, *_, nd=arr.ndim: (l,) + (0,) * (nd - 1)))
    for name in ("router_wt", "router_b"):
        arr = lw[name]
        args.append(arr)
        in_specs.append(pl.BlockSpec(arr.shape, lambda bi, j, *_: (0, 0)))

    out_shape = (jax.ShapeDtypeStruct((t, d), F32),
                 jax.ShapeDtypeStruct((t * REC, LANES), F32),
                 jax.ShapeDtypeStruct((t // ts, 1, ts), I32))
    out_specs = (pl.BlockSpec((ts, d), tile),
                 pl.BlockSpec((ts * REC, LANES), tile),
                 pl.BlockSpec((1, 1, ts), lambda bi, j, *_: (bi * nj + j, 0, 0)))
    nslab = GROUP_W // LANES
    scratch = [pltpu.VMEM((nslab, POOL_HALO + ts, LANES), F32),
               pltpu.VMEM((nslab, CONV_HALO + ts, LANES), F32),
               pltpu.VMEM((nslab, SC_HALO + ts, LANES), F32)]
    if has_prev:
        scratch += [pltpu.VMEM((2, ts * REC, LANES), F32), pltpu.SemaphoreType.DMA((2,))]
        args = [pos_prev] + args
    grid_spec = pltpu.PrefetchScalarGridSpec(
        num_scalar_prefetch=1 if has_prev else 0,
        grid=(batch, nj),
        in_specs=in_specs,
        out_specs=out_specs,
        scratch_shapes=scratch,
    )
    return pl.pallas_call(
        functools.partial(_mixer_body, has_prev, ts, sb),
        grid_spec=grid_spec,
        out_shape=out_shape,
        compiler_params=pltpu.CompilerParams(
            dimension_semantics=("arbitrary", "arbitrary"), vmem_limit_bytes=VMEM_LIMIT),
        name="mixer_layer",
    )(*args)


def _plan_body(tm, cls_ref, pos_ref, meta_ref):
    cls = cls_ref[...]
    nrow = cls.shape[0]
    ii = lax.broadcasted_iota(I32, (LANES, LANES), 0)
    jj = lax.broadcasted_iota(I32, (LANES, LANES), 1)
    upper = jnp.where(ii <= jj, 1.0, 0.0).astype(BF16)
    ri = lax.broadcasted_iota(I32, (nrow, nrow), 0)
    rj = lax.broadcasted_iota(I32, (nrow, nrow), 1)
    lower = jnp.where(rj < ri, 1.0, 0.0).astype(BF16)
    lane_i = lax.broadcasted_iota(I32, (1, LANES), 1)
    tile_i = lane_i.astype(F32)
    start = jnp.zeros((1, 1), F32)
    cum_tiles = jnp.zeros((1, 1), F32)
    pos = jnp.zeros((nrow, LANES), F32)
    tclass = jnp.zeros((1, LANES), F32)
    seg_end = jnp.zeros((1, LANES), F32)
    for k in range(N_CLASSES):
        hit = cls == k
        oh = jnp.where(hit, 1.0, 0.0)
        ohb = oh.astype(BF16)
        incl = jnp.dot(ohb, upper, preferred_element_type=F32)
        above = jnp.sum(jnp.dot(lower, ohb, preferred_element_type=F32), axis=1, keepdims=True)
        rank = incl - oh + above
        pos = pos + jnp.where(hit, start + rank, 0.0)
        cnt = jnp.sum(jnp.sum(oh, axis=1, keepdims=True), axis=0, keepdims=True)
        n_tiles = jnp.floor((cnt + float(tm - 1)) * (1.0 / tm))
        start = start + n_tiles * float(tm)
        cum_tiles = cum_tiles + n_tiles
        tclass = tclass + jnp.where(tile_i >= cum_tiles, 1.0, 0.0)
        seg_end = jnp.where(lane_i == k, start, seg_end)
    pos_ref[...] = pos.astype(I32)

    valid = tclass < float(N_CLASSES)
    kc = jnp.minimum(tclass, float(N_CLASSES - 1))
    grp = jnp.zeros_like(kc)
    for g in range(1, N_EXPERT_GROUPS):
        grp = grp + jnp.where(kc >= float(g * PAIRS_PER_GROUP), 1.0, 0.0)
    pr = kc - grp * float(PAIRS_PER_GROUP)
    lo = jnp.where(pr >= 3.0, 1.0, 0.0) + jnp.where(pr >= 5.0, 1.0, 0.0)
    base = jnp.where(lo == 0.0, 0.0, jnp.where(lo == 1.0, 3.0, 5.0))
    hi = pr - base + lo + 1.0
    rows = {META_EA: grp * float(EXPERTS_PER_GROUP) + lo,
            META_EB: grp * float(EXPERTS_PER_GROUP) + hi,
            META_VALID: jnp.where(valid, 1.0, 0.0),
            META_BLK: jnp.minimum(tile_i, cum_tiles - 1.0),
            META_END: seg_end}
    rid = lax.broadcasted_iota(I32, (SUBLANES, LANES), 0)
    meta = jnp.zeros((SUBLANES, LANES), F32)
    for k, row in rows.items():
        meta = jnp.where(rid == k, row, meta)
    meta_ref[...] = meta.astype(I32)


def _route_plan(cls2d, tm):
    nrow = cls2d.shape[0]
    return pl.pallas_call(
        functools.partial(_plan_body, tm),
        in_specs=[pl.BlockSpec((nrow, LANES), lambda: (0, 0))],
        out_specs=(pl.BlockSpec((nrow, LANES), lambda: (0, 0)),
                   pl.BlockSpec((SUBLANES, LANES), lambda: (0, 0))),
        out_shape=(jax.ShapeDtypeStruct((nrow, LANES), I32),
                   jax.ShapeDtypeStruct((SUBLANES, LANES), I32)),
        name="route_plan",
    )(cls2d)


def _dispatch_body(tm, nt, pos_ref, meta_ref, src_ref, dst_hbm, zbuf, zsem, sem):
    g = pl.program_id(0)
    n = PERMUTE_ROWS

    @pl.when(g == 0)
    def _():
        zbuf[...] = jnp.zeros_like(zbuf)

        def zero_tile(first_slot, wait):
            cp = pltpu.make_async_copy(
                zbuf, dst_hbm.at[pl.ds(pl.multiple_of(first_slot * REC, REC), tm * REC), :], zsem)
            cp.wait() if wait else cp.start()

        for wait in (False, True):
            for k in range(N_CLASSES):
                prev_end = meta_ref[META_END, k - 1] if k > 0 else 0

                @pl.when(meta_ref[META_END, k] > prev_end)
                def _():
                    zero_tile(meta_ref[META_END, k] - tm, wait)

            for i in range(nt):
                @pl.when(meta_ref[META_VALID, i] == 0)
                def _():
                    zero_tile(i * tm, wait)

    base_row = g * (n // LANES)
    for r0 in range(0, n, PERMUTE_GROUP):
        slots = [pos_ref[base_row + (r0 + k) // LANES, (r0 + k) % LANES] for k in range(PERMUTE_GROUP)]
        for k, slot in enumerate(slots):
            pltpu.make_async_copy(
                src_ref.at[pl.ds((r0 + k) * REC, REC), :],
                dst_hbm.at[pl.ds(pl.multiple_of(slot * REC, REC), REC), :],
                sem).start(priority=k % 2)

    pltpu.make_async_copy(src_ref, dst_hbm.at[pl.ds(0, n * REC), :], sem).wait()


def _row_dispatch(rec, pos, meta, tm, nt):
    n_tok = pos.size
    grid_spec = pltpu.PrefetchScalarGridSpec(
        num_scalar_prefetch=2,
        grid=(n_tok // PERMUTE_ROWS,),
        in_specs=[pl.BlockSpec((PERMUTE_ROWS * REC, LANES), lambda g, pos, meta: (g, 0))],
        out_specs=pl.BlockSpec(memory_space=pl.ANY),
        scratch_shapes=[pltpu.VMEM((tm * REC, LANES), rec.dtype), pltpu.SemaphoreType.DMA(()),
                        pltpu.SemaphoreType.DMA(())],
    )
    return pl.pallas_call(
        functools.partial(_dispatch_body, tm, nt),
        grid_spec=grid_spec,
        out_shape=jax.ShapeDtypeStruct((nt * tm * REC, LANES), rec.dtype),
        compiler_params=pltpu.CompilerParams(
            dimension_semantics=("arbitrary",), vmem_limit_bytes=VMEM_LIMIT),
        name="row_dispatch",
    )(pos, meta, rec)


def _moe_body(tm, l, meta_ref, h_ref, rw_ref, wg_hbm, wu_hbm, wd_hbm, y_ref,
              wg_ref, wu_ref, wd_ref, sg_ref, su_ref, sd_ref, wsem, st_ref):
    i = pl.program_id(0)

    def copies(e, buf):
        return (pltpu.make_async_copy(wg_hbm.at[l, e], sg_ref.at[buf], wsem.at[buf]),
                pltpu.make_async_copy(wu_hbm.at[l, e], su_ref.at[buf], wsem.at[buf]),
                pltpu.make_async_copy(wd_hbm.at[l, e], sd_ref.at[buf], wsem.at[buf]))

    def request(e, buf):
        @pl.when(st_ref[e] == 0)
        def _():
            for cp in copies(e, buf):
                cp.start()
            st_ref[N_EXPERTS + buf] = e

    def finish(buf):
        e = st_ref[N_EXPERTS + buf]

        @pl.when(e >= 0)
        def _():
            for cp in copies(e, buf):
                cp.wait()
            wg_ref[e] = sg_ref[buf].astype(BF16)
            wu_ref[e] = su_ref[buf].astype(BF16)
            wd_ref[e] = sd_ref[buf].astype(BF16)
            st_ref[e] = 1
            st_ref[N_EXPERTS + buf] = -1

    @pl.when(i == 0)
    def _():
        for e in range(N_EXPERTS):
            st_ref[e] = 0
        st_ref[N_EXPERTS] = -1
        st_ref[N_EXPERTS + 1] = -1
        request(meta_ref[META_EA, 0], 0)
        request(meta_ref[META_EB, 0], 1)

    finish(0)
    finish(1)

    @pl.when(meta_ref[META_VALID, i + 1] == 1)
    def _():
        request(meta_ref[META_EA, i + 1], 0)
        request(meta_ref[META_EB, i + 1], 1)

    @pl.when(meta_ref[META_VALID, i] == 1)
    def _():
        ea = meta_ref[META_EA, i]
        eb = meta_ref[META_EB, i]
        xb = jnp.concatenate([_rows_of(h_ref, c, tm) for c in range(REC)], axis=1).astype(BF16)
        xf = xb.astype(F32)

        def expert(e):
            score = jax.nn.sigmoid(jnp.sum(xf * rw_ref[e].astype(F32), axis=-1, keepdims=True))
            hg = jnp.dot(xb, wg_ref[e], preferred_element_type=F32)
            hu = jnp.dot(xb, wu_ref[e], preferred_element_type=F32)
            return score, jax.nn.silu(hg) * hu

        sa, ga = expert(ea)
        sb, gb = expert(eb)
        acta = ga * (sa / (sa + sb))
        actb = gb * (sb / (sa + sb))
        y = (jnp.dot(acta.astype(BF16), wd_ref[ea], preferred_element_type=F32)
             + jnp.dot(actb.astype(BF16), wd_ref[eb], preferred_element_type=F32))
        for c in range(REC):
            y_ref[pl.ds(c, tm, stride=REC), :] = y[:, c * LANES:(c + 1) * LANES]

    @pl.when(meta_ref[META_VALID, i] == 0)
    def _():
        y_ref[...] = jnp.zeros_like(y_ref)


def _moe_tiles(l, hs, meta, router_rows, wg, wu, wd, tm, nt):
    d = D_MODEL
    whole = pl.BlockSpec(memory_space=pl.ANY)
    grid_spec = pltpu.PrefetchScalarGridSpec(
        num_scalar_prefetch=1,
        grid=(nt,),
        in_specs=[
            pl.BlockSpec((tm * REC, LANES), lambda i, meta: (meta[META_BLK, i], 0)),
            pl.BlockSpec((N_EXPERTS, 1, d), lambda i, meta: (0, 0, 0), pipeline_mode=pl.Buffered(1)),
            whole, whole, whole,
        ],
        out_specs=pl.BlockSpec((tm * REC, LANES), lambda i, meta: (i, 0)),
        scratch_shapes=[pltpu.VMEM((N_EXPERTS, d, D_EXPERT), BF16),
                        pltpu.VMEM((N_EXPERTS, d, D_EXPERT), BF16),
                        pltpu.VMEM((N_EXPERTS, D_EXPERT, d), BF16),
                        pltpu.VMEM((2, d, D_EXPERT), F32),
                        pltpu.VMEM((2, d, D_EXPERT), F32),
                        pltpu.VMEM((2, D_EXPERT, d), F32),
                        pltpu.SemaphoreType.DMA((2,)),
                        pltpu.SMEM((N_EXPERTS + 2,), I32)],
    )
    return pl.pallas_call(
        functools.partial(_moe_body, tm, l),
        grid_spec=grid_spec,
        out_shape=jax.ShapeDtypeStruct((nt * tm * REC, LANES), F32),
        compiler_params=pltpu.CompilerParams(
            dimension_semantics=("arbitrary",), vmem_limit_bytes=VMEM_LIMIT),
        name="moe_tiles",
    )(meta, hs, router_rows, wg, wu, wd)


def _final_body(ts, pos_ref, x_ref, ys_hbm, mod_ref, g_ref, o_ref, ybuf, gsem):
    y_ref = _gather_moe_rows(pos_ref, ys_hbm, ybuf, gsem, ts)
    y = jnp.concatenate([_rows_of(y_ref, c, ts) for c in range(REC)], axis=1)
    x = x_ref[...] + mod_ref[0, 0, 5:6, :] * y
    r = lax.rsqrt(jnp.mean(x * x, axis=-1, keepdims=True) + EPS)
    o_ref[...] = x * r * g_ref[...]


def _final_norm(x, ys, pos, mods, final_g, ts, batch):
    t, d = x.shape
    nj = t // batch // ts
    last = mods.shape[0] - 1
    tile = lambda bi, j, pos: (bi * nj + j, 0)
    grid_spec = pltpu.PrefetchScalarGridSpec(
        num_scalar_prefetch=1,
        grid=(batch, nj),
        in_specs=[pl.BlockSpec((ts, d), tile), pl.BlockSpec(memory_space=pl.ANY),
                  pl.BlockSpec((1, 1, N_MOD, d), lambda bi, j, pos: (last, bi, 0, 0)),
                  pl.BlockSpec((1, d), lambda bi, j, pos: (0, 0))],
        out_specs=pl.BlockSpec((ts, d), tile),
        scratch_shapes=[pltpu.VMEM((2, ts * REC, LANES), F32), pltpu.SemaphoreType.DMA((2,))],
    )
    return pl.pallas_call(
        functools.partial(_final_body, ts),
        grid_spec=grid_spec,
        out_shape=jax.ShapeDtypeStruct((t, d), F32),
        compiler_params=pltpu.CompilerParams(
            dimension_semantics=("arbitrary", "arbitrary"), vmem_limit_bytes=VMEM_LIMIT),
        name="final_norm",
    )(pos, x, ys, mods, final_g.reshape(1, d))


def _stacked_weights(norm1_g, norm2_g, w_in, pool_w, pool_scale, conv_dw, conv_db, conv_ln_g,
                     conv_ln_b, conv_pw, conv_pb, sg_ln_g, sg_ln_b, sg_w, sg_b, sc_w, out_norm_g, w_out):
    depth = w_in.shape[0]
    row = lambda v: v.reshape(depth, 1, -1)
    pool_bd = jnp.zeros((depth, GROUP_W, GROUP_W), pool_w.dtype)
    for i in range(len(POOL_WINDOWS)):
        pool_bd = pool_bd.at[:, i * POOL_CG:(i + 1) * POOL_CG, i * POOL_CG:(i + 1) * POOL_CG].set(
            pool_w[:, i])
    return {
        "n1g": row(norm1_g), "n2g": row(norm2_g),
        "w_in": w_in.astype(BF16),
        "pool_w": pool_bd.astype(BF16),
        "pool_scale": row(pool_scale),
        "conv_dw": jnp.pad(conv_dw, ((0, 0), (0, 1), (0, 0))),
        "conv_db": row(conv_db), "conv_ln_g": row(conv_ln_g), "conv_ln_b": row(conv_ln_b),
        "conv_pw": conv_pw.astype(BF16), "conv_pb": row(conv_pb),
        "sg_ln_g": row(sg_ln_g), "sg_ln_b": row(sg_ln_b),
        "sg_w": sg_w,
        "sg_b": jnp.repeat(jnp.swapaxes(sg_b, 1, 2), GROUP_W // SG_HEADS, axis=2),
        "sc_w": jnp.pad(sc_w, ((0, 0), (0, SUBLANES - SC_WIDTH), (0, 0))),
        "out_norm_g": row(out_norm_g),
        "w_out": w_out.astype(BF16),
    }


def kernel(x, c, norm1_g, norm2_g, w_ada, b_ada, w_in, pool_w, pool_scale, conv_dw, conv_db, conv_ln_g, conv_ln_b, conv_pw, conv_pb, sg_ln_g, sg_ln_b, sg_w, sg_b, sc_w, out_norm_g, w_out, router_w, router_bias, exp_w_gate, exp_w_up, exp_w_down, final_g):
    b, s, d = x.shape
    depth = w_in.shape[0]
    t = b * s
    ts = 512
    ts_first = 1024
    sb = 256
    tm = 256
    nt = t // tm + N_CLASSES
    assert d == D_MODEL and t % LANES == 0 and s % ts_first == 0 and t % tm == 0 and nt <= LANES
    assert t % PERMUTE_ROWS == 0 and ts_first % ts == 0 and ts % sb == 0 and sb % SG_CHUNK == 0

    mods = _ada_mod(c, w_ada, b_ada)

    perm = lambda v: v.reshape(N_EXPERT_GROUPS, EXPERTS_PER_GROUP, -1).transpose(1, 0, 2).reshape(
        N_EXPERTS, -1)
    lw = _stacked_weights(norm1_g, norm2_g, w_in, pool_w, pool_scale, conv_dw, conv_db, conv_ln_g,
                          conv_ln_b, conv_pw, conv_pb, sg_ln_g, sg_ln_b, sg_w, sg_b, sc_w, out_norm_g,
                          w_out)
    lw["router_wt"] = perm(router_w.T).astype(BF16)
    lw["router_b"] = perm(router_bias.reshape(N_EXPERTS, 1))
    router_rows = router_w.T.astype(BF16).reshape(N_EXPERTS, 1, d)

    xcur, ys, pos = x.reshape(t, d), None, None
    for l in range(depth):
        xcur, rec, cls = _mixer_layer(l, xcur, ys, pos, mods, lw, ts if l else ts_first, sb, b)
        pos, meta = _route_plan(cls.reshape(t // LANES, LANES), tm)
        hs = _row_dispatch(rec, pos, meta, tm, nt)
        ys = _moe_tiles(l, hs, meta, router_rows, exp_w_gate, exp_w_up, exp_w_down, tm, nt)
    return _final_norm(xcur, ys, pos, mods, final_g, ts, b).reshape(b, s, d)
```

```python
import functools

import jax
import jax.numpy as jnp
from jax import lax
from jax.experimental import pallas as pl
from jax.experimental.pallas import tpu as pltpu

D_MODEL = 1024
GROUP_W = 256
POOL_WINDOWS = (2, 4, 8, 16)
POOL_CG = 64
CONV_WIDTH = 31
SG_CHUNK = 128
SG_HEADS = 4
SC_WIDTH = 3
N_EXPERTS = 16
N_EXPERT_GROUPS = 4
EXPERTS_PER_GROUP = 4
PAIRS_PER_GROUP = 6
N_CLASSES = N_EXPERT_GROUPS * PAIRS_PER_GROUP
D_EXPERT = 256
N_MOD = 6
EPS = 1e-6
LN_EPS = 1e-5

LANES = 128
SUBLANES = 8
REC = SUBLANES
CONV_HALO = 32
POOL_HALO = 16
SC_HALO = 8
CONV_ROWS = 32
META_EA, META_EB, META_VALID, META_BLK, META_END = 0, 1, 2, 3, 4
PERMUTE_ROWS = 2048
PERMUTE_GROUP = 16
VMEM_LIMIT = 56 * 1024 * 1024

F32 = jnp.float32
BF16 = jnp.bfloat16
I32 = jnp.int32


def _rows_of(ref, c, n, first=0):
    return ref[pl.ds(first * REC + c, n, stride=REC), :]


def _ada_body(c_ref, w_ref, b_ref, o_ref):
    ca = jax.nn.silu(c_ref[...])
    o_ref[0] = jnp.dot(ca.astype(BF16), w_ref[0].astype(BF16), preferred_element_type=F32) + b_ref[0]


def _ada_mod(c, w_ada, b_ada):
    depth, d, n = w_ada.shape
    b = c.shape[0]
    tn = 2048
    cp = jnp.pad(c, ((0, SUBLANES - b), (0, 0)))
    out = pl.pallas_call(
        _ada_body,
        grid=(depth, n // tn),
        in_specs=[
            pl.BlockSpec((SUBLANES, d), lambda l, j: (0, 0)),
            pl.BlockSpec((1, d, tn), lambda l, j: (l, 0, j)),
            pl.BlockSpec((1, 1, tn), lambda l, j: (l, 0, j)),
        ],
        out_specs=pl.BlockSpec((1, SUBLANES, tn), lambda l, j: (l, 0, j)),
        out_shape=jax.ShapeDtypeStruct((depth, SUBLANES, n), F32),
        compiler_params=pltpu.CompilerParams(
            dimension_semantics=("arbitrary", "arbitrary"), vmem_limit_bytes=VMEM_LIMIT),
        name="ada_mod",
    )(cp, w_ada, b_ada.reshape(depth, 1, n))
    return out[:, :b].reshape(depth, b, N_MOD, d)


def _layer_norm_rows(v, g, b):
    mu = jnp.mean(v, axis=-1, keepdims=True)
    vc = v - mu
    var = jnp.mean(vc * vc, axis=-1, keepdims=True)
    return vc * lax.rsqrt(var + LN_EPS) * g + b


def _carry_tail(ext_ref, j, ts, halo):
    nslab = ext_ref.shape[0]

    @pl.when(j == 0)
    def _():
        ext_ref[:, 0:halo, :] = jnp.zeros((nslab, halo, LANES), F32)

    @pl.when(j > 0)
    def _():
        ext_ref[:, 0:halo, :] = ext_ref[:, ts:ts + halo, :]


def _append_rows(ext_ref, cur, first):
    for cb in range(ext_ref.shape[0]):
        ext_ref[cb, first:first + cur.shape[0], :] = cur[:, cb * LANES:(cb + 1) * LANES]


def _first_argmax(vals):
    best = vals[0]
    idx = jnp.zeros_like(best)
    for k in range(1, len(vals)):
        better = vals[k] > best
        idx = jnp.where(better, float(k), idx)
        best = jnp.where(better, vals[k], best)
    return idx, best


def _pick(idx, vals):
    out = vals[0]
    for k in range(1, len(vals)):
        out = jnp.where(idx == float(k), vals[k], out)
    return out


def _gather_moe_rows(pos_ref, ys_hbm, ybuf, gsem, ts):
    g = pl.program_id(0) * pl.num_programs(1) + pl.program_id(1)
    n_steps = pl.num_programs(0) * pl.num_programs(1)

    def start(step, buf):
        base_row = step * (ts // LANES)
        for r0 in range(0, ts, PERMUTE_GROUP):
            slots = [pos_ref[base_row + (r0 + k) // LANES, (r0 + k) % LANES]
                     for k in range(PERMUTE_GROUP)]
            for k, slot in enumerate(slots):
                pltpu.make_async_copy(
                    ys_hbm.at[pl.ds(pl.multiple_of(slot * REC, REC), REC), :],
                    ybuf.at[buf, pl.ds((r0 + k) * REC, REC), :],
                    gsem.at[buf]).start(priority=k % 2)

    @pl.when(g == 0)
    def _():
        start(0, 0)

    @pl.when(g + 1 < n_steps)
    def _():
        start(g + 1, (g + 1) % 2)

    cur = g % 2
    pltpu.make_async_copy(ys_hbm.at[pl.ds(0, ts * REC), :], ybuf.at[cur], gsem.at[cur]).wait()
    return ybuf.at[cur]


def _mixer_body(has_prev, ts, sb, *refs):
    refs = list(refs)
    if has_prev:
        pos_ref = refs.pop(0)
    x_ref = refs.pop(0)
    if has_prev:
        ys_hbm = refs.pop(0)
        modp_ref = refs.pop(0)
        gsem = refs.pop()
        ybuf = refs.pop()
    (mod_ref, n1g_ref, n2g_ref, win_ref, poolw_ref, pools_ref,
     cdw_ref, cdb_ref, clg_ref, clb_ref, cpw_ref, cpb_ref,
     slg_ref, slb_ref, sgw_ref, sgb_ref, scw_ref, ong_ref, wout_ref, rwt_ref, rb_ref,
     x1_ref, rec_ref, cls_ref,
     ea_ref, eg_ref, ed_ref) = refs

    j = pl.program_id(1)
    gw = GROUP_W
    nslab = gw // LANES
    if has_prev:
        yprev_ref = _gather_moe_rows(pos_ref, ys_hbm, ybuf, gsem, ts)

    sh1 = mod_ref[0, 0, 0:1, :]
    sc1 = mod_ref[0, 0, 1:2, :]
    g1 = mod_ref[0, 0, 2:3, :]
    sh2 = mod_ref[0, 0, 3:4, :]
    sc2 = mod_ref[0, 0, 4:5, :]
    gain1 = n1g_ref[0] * (1.0 + sc1)
    gain2 = n2g_ref[0] * (1.0 + sc2)

    _carry_tail(ea_ref, j, ts, POOL_HALO)
    _carry_tail(eg_ref, j, ts, CONV_HALO)
    _carry_tail(ed_ref, j, ts, SC_HALO)

    rowi = lax.broadcasted_iota(I32, (SG_CHUNK, SG_CHUNK), 0)
    coli = lax.broadcasted_iota(I32, (SG_CHUNK, SG_CHUNK), 1)
    tril = coli <= rowi
    wcat = jnp.concatenate(
        [jnp.where(tril, sgw_ref[0, hh], 0.0) for hh in range(SG_HEADS)], axis=1).astype(BF16)
    lane_g = lax.broadcasted_iota(I32, (SG_CHUNK, gw), 1) // (gw // SG_HEADS)
    zero_bf = jnp.zeros((SG_CHUNK, gw), BF16)
    lane = lax.broadcasted_iota(I32, (sb, LANES), 1)
    first = lane < POOL_CG

    for r0 in range(0, ts, sb):
        x = x_ref[r0:r0 + sb, :]
        if has_prev:
            yprev = jnp.concatenate([_rows_of(yprev_ref, c, sb, r0) for c in range(REC)], axis=1)
            x = x + modp_ref[0, 0, 5:6, :] * yprev

        r1 = lax.rsqrt(jnp.mean(x * x, axis=-1, keepdims=True) + EPS)
        h = x * r1 * gain1 + sh1
        p = jnp.dot(h.astype(BF16), win_ref[0], preferred_element_type=F32)

        a = p[:, 0:gw]
        _append_rows(ea_ref, a, POOL_HALO + r0)

        def a_shift(s, cb):
            return ea_ref[cb, POOL_HALO + r0 - s:POOL_HALO + r0 - s + sb, :]

        pos1 = (lax.broadcasted_iota(I32, (sb, LANES), 0) + (j * ts + r0 + 1)).astype(F32)
        a_lo = a[:, 0:LANES]
        s01 = a_lo + a_shift(1, 0)
        s03 = s01 + (a_shift(2, 0) + a_shift(3, 0))
        num_lo = jnp.where(first, s01, s03)
        den_lo = jnp.minimum(pos1, jnp.where(first, float(POOL_WINDOWS[0]), float(POOL_WINDOWS[1])))
        a_hi = a[:, LANES:gw]
        s07 = a_hi
        for s in range(1, 8):
            s07 = s07 + a_shift(s, 1)
        s815 = a_shift(8, 1)
        for s in range(9, 16):
            s815 = s815 + a_shift(s, 1)
        num_hi = jnp.where(first, s07, s07 + s815)
        den_hi = jnp.minimum(pos1, jnp.where(first, float(POOL_WINDOWS[2]), float(POOL_WINDOWS[3])))
        d_pool = jnp.concatenate([num_lo / den_lo - a_lo, num_hi / den_hi - a_hi], axis=1)
        y_a = jnp.dot(d_pool.astype(BF16), poolw_ref[0], preferred_element_type=F32) * pools_ref[0]

        glu = p[:, gw:2 * gw] * jax.nn.sigmoid(p[:, 2 * gw:3 * gw])
        _append_rows(eg_ref, glu, CONV_HALO + r0)
        conv_cols = []
        for cb in range(nslab):
            conv_chunks = []
            for rr in range(r0, r0 + sb, CONV_ROWS):
                acc = None
                for k in range(CONV_WIDTH):
                    off = CONV_HALO - (CONV_WIDTH - 1) + k + rr
                    term = (eg_ref[cb, off:off + CONV_ROWS, :]
                            * cdw_ref[0, k:k + 1, cb * LANES:(cb + 1) * LANES])
                    acc = term if acc is None else acc + term
                conv_chunks.append(acc)
            conv_cols.append(jnp.concatenate(conv_chunks, axis=0))
        hb = jnp.concatenate(conv_cols, axis=1) + cdb_ref[0]
        hb = jax.nn.silu(_layer_norm_rows(hb, clg_ref[0], clb_ref[0]))
        y_b = jnp.dot(hb.astype(BF16), cpw_ref[0], preferred_element_type=F32) + cpb_ref[0]

        u = p[:, 3 * gw:4 * gw]
        vln = _layer_norm_rows(p[:, 4 * gw:5 * gw], slg_ref[0], slb_ref[0]).astype(BF16)
        yc_chunks = []
        for n in range(sb // SG_CHUNK):
            vch = vln[n * SG_CHUNK:(n + 1) * SG_CHUNK, :]
            vbd = jnp.concatenate([jnp.where(lane_g == hh, vch, zero_bf) for hh in range(SG_HEADS)],
                                  axis=0)
            mixed = jnp.dot(wcat, vbd, preferred_element_type=F32) + sgb_ref[0]
            yc_chunks.append(u[n * SG_CHUNK:(n + 1) * SG_CHUNK, :] * mixed)
        y_c = jnp.concatenate(yc_chunks, axis=0)

        cx = p[:, 6 * gw:7 * gw] * p[:, 7 * gw:8 * gw]
        _append_rows(ed_ref, cx, SC_HALO + r0)
        cd_cols = []
        for cb in range(nslab):
            ls = slice(cb * LANES, (cb + 1) * LANES)
            at = SC_HALO + r0
            cd_cols.append(cx[:, ls] * scw_ref[0, 2:3, ls]
                           + ed_ref[cb, at - 1:at - 1 + sb, :] * scw_ref[0, 1:2, ls]
                           + ed_ref[cb, at - 2:at - 2 + sb, :] * scw_ref[0, 0:1, ls])
        y_d = p[:, 5 * gw:6 * gw] * jnp.concatenate(cd_cols, axis=1)

        normed = []
        for gi, yg in enumerate((y_a, y_b, y_c, y_d)):
            rg = lax.rsqrt(jnp.mean(yg * yg, axis=-1, keepdims=True) + EPS)
            normed.append((yg * rg * ong_ref[0, :, gi * gw:(gi + 1) * gw]).astype(BF16))
        yn = jnp.concatenate(normed, axis=1)
        x1 = x + g1 * jnp.dot(yn, wout_ref[0], preferred_element_type=F32)
        x1_ref[r0:r0 + sb, :] = x1

        r2 = lax.rsqrt(jnp.mean(x1 * x1, axis=-1, keepdims=True) + EPS)
        h2 = x1 * r2 * gain2 + sh2
        for c in range(REC):
            rec_ref[pl.ds(r0 * REC + c, sb, stride=REC), :] = h2[:, c * LANES:(c + 1) * LANES]
        logits = lax.dot_general(rwt_ref[...], h2.astype(BF16), (((1,), (1,)), ((), ())),
                                 preferred_element_type=F32)
        sel = jax.nn.sigmoid(logits) + rb_ref[...]
        epg = EXPERTS_PER_GROUP
        sel_s = [sel[jj * N_EXPERT_GROUPS:(jj + 1) * N_EXPERT_GROUPS, :] for jj in range(epg)]
        top2 = None
        for ja in range(epg):
            for jb in range(ja + 1, epg):
                pair = sel_s[ja] + sel_s[jb]
                top2 = pair if top2 is None else jnp.maximum(top2, pair)
        gidx, _ = _first_argmax([top2[g:g + 1, :] for g in range(N_EXPERT_GROUPS)])
        sel_c = [_pick(gidx, [sel_s[jj][g:g + 1, :] for g in range(N_EXPERT_GROUPS)])
                 for jj in range(epg)]
        i1, _ = _first_argmax(sel_c)
        i2, _ = _first_argmax([jnp.where(i1 == float(jj), -jnp.inf, sel_c[jj]) for jj in range(epg)])
        lo = jnp.minimum(i1, i2)
        hi = jnp.maximum(i1, i2)
        base = jnp.where(lo == 0.0, 0.0, jnp.where(lo == 1.0, 3.0, 5.0))
        cls = gidx * float(PAIRS_PER_GROUP) + base + (hi - lo - 1.0)
        cls_ref[0, :, r0:r0 + sb] = cls.astype(I32)


def _mixer_layer(l, x, ys_prev, pos_prev, mods, lw, ts, sb, batch):
    t, d = x.shape
    nj = t // batch // ts
    has_prev = ys_prev is not None
    tile = lambda bi, j, *_: (bi * nj + j, 0)

    args = [x]
    in_specs = [pl.BlockSpec((ts, d), tile)]
    if has_prev:
        args += [ys_prev, mods]
        in_specs += [pl.BlockSpec(memory_space=pl.ANY),
                     pl.BlockSpec((1, 1, N_MOD, d), lambda bi, j, *_: (l - 1, bi, 0, 0))]
    args += [mods]
    in_specs += [pl.BlockSpec((1, 1, N_MOD, d), lambda bi, j, *_: (l, bi, 0, 0))]
    for name in ("n1g", "n2g", "w_in", "pool_w", "pool_scale", "conv_dw", "conv_db", "conv_ln_g",
                 "conv_ln_b", "conv_pw", "conv_pb", "sg_ln_g", "sg_ln_b", "sg_w", "sg_b", "sc_w",
                 "out_norm_g", "w_out"):
        arr = lw[name]
        args.append(arr)
        in_specs.append(pl.BlockSpec((1,) + arr.shape[1:],
                                     lambda bi, j, *_, nd=arr.ndim: (l,) + (0,) * (nd - 1)))
    for name in ("router_wt", "router_b"):
        arr = lw[name]
        args.append(arr)
        in_specs.append(pl.BlockSpec(arr.shape, lambda bi, j, *_: (0, 0)))

    out_shape = (jax.ShapeDtypeStruct((t, d), F32),
                 jax.ShapeDtypeStruct((t * REC, LANES), F32),
                 jax.ShapeDtypeStruct((t // ts, 1, ts), I32))
    out_specs = (pl.BlockSpec((ts, d), tile),
                 pl.BlockSpec((ts * REC, LANES), tile),
                 pl.BlockSpec((1, 1, ts), lambda bi, j, *_: (bi * nj + j, 0, 0)))
    nslab = GROUP_W // LANES
    scratch = [pltpu.VMEM((nslab, POOL_HALO + ts, LANES), F32),
               pltpu.VMEM((nslab, CONV_HALO + ts, LANES), F32),
               pltpu.VMEM((nslab, SC_HALO + ts, LANES), F32)]
    if has_prev:
        scratch += [pltpu.VMEM((2, ts * REC, LANES), F32), pltpu.SemaphoreType.DMA((2,))]
        args = [pos_prev] + args
    grid_spec = pltpu.PrefetchScalarGridSpec(
        num_scalar_prefetch=1 if has_prev else 0,
        grid=(batch, nj),
        in_specs=in_specs,
        out_specs=out_specs,
        scratch_shapes=scratch,
    )
    return pl.pallas_call(
        functools.partial(_mixer_body, has_prev, ts, sb),
        grid_spec=grid_spec,
        out_shape=out_shape,
        compiler_params=pltpu.CompilerParams(
            dimension_semantics=("arbitrary", "arbitrary"), vmem_limit_bytes=VMEM_LIMIT),
        name="mixer_layer",
    )(*args)


def _plan_body(tm, cls_ref, pos_ref, meta_ref):
    cls = cls_ref[...]
    nrow = cls.shape[0]
    ii = lax.broadcasted_iota(I32, (LANES, LANES), 0)
    jj = lax.broadcasted_iota(I32, (LANES, LANES), 1)
    upper = jnp.where(ii <= jj, 1.0, 0.0).astype(BF16)
    ri = lax.broadcasted_iota(I32, (nrow, nrow), 0)
    rj = lax.broadcasted_iota(I32, (nrow, nrow), 1)
    lower = jnp.where(rj < ri, 1.0, 0.0).astype(BF16)
    lane_i = lax.broadcasted_iota(I32, (1, LANES), 1)
    tile_i = lane_i.astype(F32)
    start = jnp.zeros((1, 1), F32)
    cum_tiles = jnp.zeros((1, 1), F32)
    pos = jnp.zeros((nrow, LANES), F32)
    tclass = jnp.zeros((1, LANES), F32)
    seg_end = jnp.zeros((1, LANES), F32)
    for k in range(N_CLASSES):
        hit = cls == k
        oh = jnp.where(hit, 1.0, 0.0)
        ohb = oh.astype(BF16)
        incl = jnp.dot(ohb, upper, preferred_element_type=F32)
        above = jnp.sum(jnp.dot(lower, ohb, preferred_element_type=F32), axis=1, keepdims=True)
        rank = incl - oh + above
        pos = pos + jnp.where(hit, start + rank, 0.0)
        cnt = jnp.sum(jnp.sum(oh, axis=1, keepdims=True), axis=0, keepdims=True)
        n_tiles = jnp.floor((cnt + float(tm - 1)) * (1.0 / tm))
        start = start + n_tiles * float(tm)
        cum_tiles = cum_tiles + n_tiles
        tclass = tclass + jnp.where(tile_i >= cum_tiles, 1.0, 0.0)
        seg_end = jnp.where(lane_i == k, start, seg_end)
    pos_ref[...] = pos.astype(I32)

    valid = tclass < float(N_CLASSES)
    kc = jnp.minimum(tclass, float(N_CLASSES - 1))
    grp = jnp.zeros_like(kc)
    for g in range(1, N_EXPERT_GROUPS):
        grp = grp + jnp.where(kc >= float(g * PAIRS_PER_GROUP), 1.0, 0.0)
    pr = kc - grp * float(PAIRS_PER_GROUP)
    lo = jnp.where(pr >= 3.0, 1.0, 0.0) + jnp.where(pr >= 5.0, 1.0, 0.0)
    base = jnp.where(lo == 0.0, 0.0, jnp.where(lo == 1.0, 3.0, 5.0))
    hi = pr - base + lo + 1.0
    rows = {META_EA: grp * float(EXPERTS_PER_GROUP) + lo,
            META_EB: grp * float(EXPERTS_PER_GROUP) + hi,
            META_VALID: jnp.where(valid, 1.0, 0.0),
            META_BLK: jnp.minimum(tile_i, cum_tiles - 1.0),
            META_END: seg_end}
    rid = lax.broadcasted_iota(I32, (SUBLANES, LANES), 0)
    meta = jnp.zeros((SUBLANES, LANES), F32)
    for k, row in rows.items():
        meta = jnp.where(rid == k, row, meta)
    meta_ref[...] = meta.astype(I32)


def _route_plan(cls2d, tm):
    nrow = cls2d.shape[0]
    return pl.pallas_call(
        functools.partial(_plan_body, tm),
        in_specs=[pl.BlockSpec((nrow, LANES), lambda: (0, 0))],
        out_specs=(pl.BlockSpec((nrow, LANES), lambda: (0, 0)),
                   pl.BlockSpec((SUBLANES, LANES), lambda: (0, 0))),
        out_shape=(jax.ShapeDtypeStruct((nrow, LANES), I32),
                   jax.ShapeDtypeStruct((SUBLANES, LANES), I32)),
        name="route_plan",
    )(cls2d)


def _dispatch_body(tm, nt, pos_ref, meta_ref, src_ref, dst_hbm, zbuf, zsem, sem):
    g = pl.program_id(0)
    n = PERMUTE_ROWS

    @pl.when(g == 0)
    def _():
        zbuf[...] = jnp.zeros_like(zbuf)

        def zero_tile(first_slot, wait):
            cp = pltpu.make_async_copy(
                zbuf, dst_hbm.at[pl.ds(pl.multiple_of(first_slot * REC, REC), tm * REC), :], zsem)
            cp.wait() if wait else cp.start()

        for wait in (False, True):
            for k in range(N_CLASSES):
                prev_end = meta_ref[META_END, k - 1] if k > 0 else 0

                @pl.when(meta_ref[META_END, k] > prev_end)
                def _():
                    zero_tile(meta_ref[META_END, k] - tm, wait)

            for i in range(nt):
                @pl.when(meta_ref[META_VALID, i] == 0)
                def _():
                    zero_tile(i * tm, wait)

    base_row = g * (n // LANES)
    for r0 in range(0, n, PERMUTE_GROUP):
        slots = [pos_ref[base_row + (r0 + k) // LANES, (r0 + k) % LANES] for k in range(PERMUTE_GROUP)]
        for k, slot in enumerate(slots):
            pltpu.make_async_copy(
                src_ref.at[pl.ds((r0 + k) * REC, REC), :],
                dst_hbm.at[pl.ds(pl.multiple_of(slot * REC, REC), REC), :],
                sem).start(priority=k % 2)

    pltpu.make_async_copy(src_ref, dst_hbm.at[pl.ds(0, n * REC), :], sem).wait()


def _row_dispatch(rec, pos, meta, tm, nt):
    n_tok = pos.size
    grid_spec = pltpu.PrefetchScalarGridSpec(
        num_scalar_prefetch=2,
        grid=(n_tok // PERMUTE_ROWS,),
        in_specs=[pl.BlockSpec((PERMUTE_ROWS * REC, LANES), lambda g, pos, meta: (g, 0))],
        out_specs=pl.BlockSpec(memory_space=pl.ANY),
        scratch_shapes=[pltpu.VMEM((tm * REC, LANES), rec.dtype), pltpu.SemaphoreType.DMA(()),
                        pltpu.SemaphoreType.DMA(())],
    )
    return pl.pallas_call(
        functools.partial(_dispatch_body, tm, nt),
        grid_spec=grid_spec,
        out_shape=jax.ShapeDtypeStruct((nt * tm * REC, LANES), rec.dtype),
        compiler_params=pltpu.CompilerParams(
            dimension_semantics=("arbitrary",), vmem_limit_bytes=VMEM_LIMIT),
        name="row_dispatch",
    )(pos, meta, rec)


def _moe_body(tm, l, meta_ref, h_ref, rw_ref, wg_hbm, wu_hbm, wd_hbm, y_ref,
              wg_ref, wu_ref, wd_ref, sg_ref, su_ref, sd_ref, wsem):
    i = pl.program_id(0)

    @pl.when(i == 0)
    def _():
        def copies(e, buf):
            return (pltpu.make_async_copy(wg_hbm.at[l, e], sg_ref.at[buf], wsem.at[buf]),
                    pltpu.make_async_copy(wu_hbm.at[l, e], su_ref.at[buf], wsem.at[buf]),
                    pltpu.make_async_copy(wd_hbm.at[l, e], sd_ref.at[buf], wsem.at[buf]))

        for cp in copies(0, 0):
            cp.start()

        def stage(e, carry):
            buf = e % 2

            @pl.when(e + 1 < N_EXPERTS)
            def _():
                for cp in copies(e + 1, 1 - buf):
                    cp.start()

            for cp in copies(e, buf):
                cp.wait()
            wg_ref[e] = sg_ref[buf].astype(BF16)
            wu_ref[e] = su_ref[buf].astype(BF16)
            wd_ref[e] = sd_ref[buf].astype(BF16)
            return carry

        lax.fori_loop(0, N_EXPERTS, stage, 0)

    @pl.when(meta_ref[META_VALID, i] == 1)
    def _():
        ea = meta_ref[META_EA, i]
        eb = meta_ref[META_EB, i]
        xb = jnp.concatenate([_rows_of(h_ref, c, tm) for c in range(REC)], axis=1).astype(BF16)
        xf = xb.astype(F32)

        def expert(e):
            score = jax.nn.sigmoid(jnp.sum(xf * rw_ref[e].astype(F32), axis=-1, keepdims=True))
            hg = jnp.dot(xb, wg_ref[e], preferred_element_type=F32)
            hu = jnp.dot(xb, wu_ref[e], preferred_element_type=F32)
            return score, jax.nn.silu(hg) * hu

        sa, ga = expert(ea)
        sb, gb = expert(eb)
        acta = ga * (sa / (sa + sb))
        actb = gb * (sb / (sa + sb))
        y = (jnp.dot(acta.astype(BF16), wd_ref[ea], preferred_element_type=F32)
             + jnp.dot(actb.astype(BF16), wd_ref[eb], preferred_element_type=F32))
        for c in range(REC):
            y_ref[pl.ds(c, tm, stride=REC), :] = y[:, c * LANES:(c + 1) * LANES]

    @pl.when(meta_ref[META_VALID, i] == 0)
    def _():
        y_ref[...] = jnp.zeros_like(y_ref)


def _moe_tiles(l, hs, meta, router_rows, wg, wu, wd, tm, nt):
    d = D_MODEL
    whole = pl.BlockSpec(memory_space=pl.ANY)
    grid_spec = pltpu.PrefetchScalarGridSpec(
        num_scalar_prefetch=1,
        grid=(nt,),
        in_specs=[
            pl.BlockSpec((tm * REC, LANES), lambda i, meta: (meta[META_BLK, i], 0)),
            pl.BlockSpec((N_EXPERTS, 1, d), lambda i, meta: (0, 0, 0), pipeline_mode=pl.Buffered(1)),
            whole, whole, whole,
        ],
        out_specs=pl.BlockSpec((tm * REC, LANES), lambda i, meta: (i, 0)),
        scratch_shapes=[pltpu.VMEM((N_EXPERTS, d, D_EXPERT), BF16),
                        pltpu.VMEM((N_EXPERTS, d, D_EXPERT), BF16),
                        pltpu.VMEM((N_EXPERTS, D_EXPERT, d), BF16),
                        pltpu.VMEM((2, d, D_EXPERT), F32),
                        pltpu.VMEM((2, d, D_EXPERT), F32),
                        pltpu.VMEM((2, D_EXPERT, d), F32),
                        pltpu.SemaphoreType.DMA((2,))],
    )
    return pl.pallas_call(
        functools.partial(_moe_body, tm, l),
        grid_spec=grid_spec,
        out_shape=jax.ShapeDtypeStruct((nt * tm * REC, LANES), F32),
        compiler_params=pltpu.CompilerParams(
            dimension_semantics=("arbitrary",), vmem_limit_bytes=VMEM_LIMIT),
        name="moe_tiles",
    )(meta, hs, router_rows, wg, wu, wd)


def _final_body(ts, pos_ref, x_ref, ys_hbm, mod_ref, g_ref, o_ref, ybuf, gsem):
    y_ref = _gather_moe_rows(pos_ref, ys_hbm, ybuf, gsem, ts)
    y = jnp.concatenate([_rows_of(y_ref, c, ts) for c in range(REC)], axis=1)
    x = x_ref[...] + mod_ref[0, 0, 5:6, :] * y
    r = lax.rsqrt(jnp.mean(x * x, axis=-1, keepdims=True) + EPS)
    o_ref[...] = x * r * g_ref[...]


def _final_norm(x, ys, pos, mods, final_g, ts, batch):
    t, d = x.shape
    nj = t // batch // ts
    last = mods.shape[0] - 1
    tile = lambda bi, j, pos: (bi * nj + j, 0)
    grid_spec = pltpu.PrefetchScalarGridSpec(
        num_scalar_prefetch=1,
        grid=(batch, nj),
        in_specs=[pl.BlockSpec((ts, d), tile), pl.BlockSpec(memory_space=pl.ANY),
                  pl.BlockSpec((1, 1, N_MOD, d), lambda bi, j, pos: (last, bi, 0, 0)),
                  pl.BlockSpec((1, d), lambda bi, j, pos: (0, 0))],
        out_specs=pl.BlockSpec((ts, d), tile),
        scratch_shapes=[pltpu.VMEM((2, ts * REC, LANES), F32), pltpu.SemaphoreType.DMA((2,))],
    )
    return pl.pallas_call(
        functools.partial(_final_body, ts),
        grid_spec=grid_spec,
        out_shape=jax.ShapeDtypeStruct((t, d), F32),
        compiler_params=pltpu.CompilerParams(
            dimension_semantics=("arbitrary", "arbitrary"), vmem_limit_bytes=VMEM_LIMIT),
        name="final_norm",
    )(pos, x, ys, mods, final_g.reshape(1, d))


def _stacked_weights(norm1_g, norm2_g, w_in, pool_w, pool_scale, conv_dw, conv_db, conv_ln_g,
                     conv_ln_b, conv_pw, conv_pb, sg_ln_g, sg_ln_b, sg_w, sg_b, sc_w, out_norm_g, w_out):
    depth = w_in.shape[0]
    row = lambda v: v.reshape(depth, 1, -1)
    pool_bd = jnp.zeros((depth, GROUP_W, GROUP_W), pool_w.dtype)
    for i in range(len(POOL_WINDOWS)):
        pool_bd = pool_bd.at[:, i * POOL_CG:(i + 1) * POOL_CG, i * POOL_CG:(i + 1) * POOL_CG].set(
            pool_w[:, i])
    return {
        "n1g": row(norm1_g), "n2g": row(norm2_g),
        "w_in": w_in.astype(BF16),
        "pool_w": pool_bd.astype(BF16),
        "pool_scale": row(pool_scale),
        "conv_dw": jnp.pad(conv_dw, ((0, 0), (0, 1), (0, 0))),
        "conv_db": row(conv_db), "conv_ln_g": row(conv_ln_g), "conv_ln_b": row(conv_ln_b),
        "conv_pw": conv_pw.astype(BF16), "conv_pb": row(conv_pb),
        "sg_ln_g": row(sg_ln_g), "sg_ln_b": row(sg_ln_b),
        "sg_w": sg_w,
        "sg_b": jnp.repeat(jnp.swapaxes(sg_b, 1, 2), GROUP_W // SG_HEADS, axis=2),
        "sc_w": jnp.pad(sc_w, ((0, 0), (0, SUBLANES - SC_WIDTH), (0, 0))),
        "out_norm_g": row(out_norm_g),
        "w_out": w_out.astype(BF16),
    }


def kernel(x, c, norm1_g, norm2_g, w_ada, b_ada, w_in, pool_w, pool_scale, conv_dw, conv_db, conv_ln_g, conv_ln_b, conv_pw, conv_pb, sg_ln_g, sg_ln_b, sg_w, sg_b, sc_w, out_norm_g, w_out, router_w, router_bias, exp_w_gate, exp_w_up, exp_w_down, final_g):
    b, s, d = x.shape
    depth = w_in.shape[0]
    t = b * s
    ts = 512
    ts_first = 1024
    sb = 256
    tm = 256
    nt = t // tm + N_CLASSES
    assert d == D_MODEL and t % LANES == 0 and s % ts_first == 0 and t % tm == 0 and nt <= LANES
    assert t % PERMUTE_ROWS == 0 and ts_first % ts == 0 and ts % sb == 0 and sb % SG_CHUNK == 0

    mods = _ada_mod(c, w_ada, b_ada)

    perm = lambda v: v.reshape(N_EXPERT_GROUPS, EXPERTS_PER_GROUP, -1).transpose(1, 0, 2).reshape(
        N_EXPERTS, -1)
    lw = _stacked_weights(norm1_g, norm2_g, w_in, pool_w, pool_scale, conv_dw, conv_db, conv_ln_g,
                          conv_ln_b, conv_pw, conv_pb, sg_ln_g, sg_ln_b, sg_w, sg_b, sc_w, out_norm_g,
                          w_out)
    lw["router_wt"] = perm(router_w.T).astype(BF16)
    lw["router_b"] = perm(router_bias.reshape(N_EXPERTS, 1))
    router_rows = router_w.T.astype(BF16).reshape(N_EXPERTS, 1, d)

    xcur, ys, pos = x.reshape(t, d), None, None
    for l in range(depth):
        xcur, rec, cls = _mixer_layer(l, xcur, ys, pos, mods, lw, ts if l else ts_first, sb, b)
        pos, meta = _route_plan(cls.reshape(t // LANES, LANES), tm)
        hs = _row_dispatch(rec, pos, meta, tm, nt)
        ys = _moe_tiles(l, hs, meta, router_rows, exp_w_gate, exp_w_up, exp_w_down, tm, nt)
    return _final_norm(xcur, ys, pos, mods, final_g, ts, b).reshape(b, s, d)
```

```python
import functools

import jax
import jax.numpy as jnp
from jax import lax
from jax.experimental import pallas as pl
from jax.experimental.pallas import tpu as pltpu

D_MODEL = 1024
GROUP_W = 256
POOL_WINDOWS = (2, 4, 8, 16)
POOL_CG = 64
CONV_WIDTH = 31
SG_CHUNK = 128
SG_HEADS = 4
SC_WIDTH = 3
N_EXPERTS = 16
N_EXPERT_GROUPS = 4
EXPERTS_PER_GROUP = 4
PAIRS_PER_GROUP = 6
N_CLASSES = N_EXPERT_GROUPS * PAIRS_PER_GROUP
D_EXPERT = 256
N_MOD = 6
EPS = 1e-6
LN_EPS = 1e-5

LANES = 128
SUBLANES = 8
REC = SUBLANES
CONV_HALO = 32
POOL_HALO = 16
SC_HALO = 8
CONV_ROWS = 32
META_EA, META_EB, META_VALID, META_BLK, META_END = 0, 1, 2, 3, 4
PERMUTE_ROWS = 2048
PERMUTE_GROUP = 16
VMEM_LIMIT = 56 * 1024 * 1024

F32 = jnp.float32
BF16 = jnp.bfloat16
I32 = jnp.int32


def _rows_of(ref, c, n, first=0):
    return ref[pl.ds(first * REC + c, n, stride=REC), :]


def _ada_body(c_ref, w_ref, b_ref, o_ref):
    ca = jax.nn.silu(c_ref[...])
    o_ref[0] = jnp.dot(ca.astype(BF16), w_ref[0].astype(BF16), preferred_element_type=F32) + b_ref[0]


def _ada_mod(c, w_ada, b_ada):
    depth, d, n = w_ada.shape
    b = c.shape[0]
    tn = 2048
    cp = jnp.pad(c, ((0, SUBLANES - b), (0, 0)))
    out = pl.pallas_call(
        _ada_body,
        grid=(depth, n // tn),
        in_specs=[
            pl.BlockSpec((SUBLANES, d), lambda l, j: (0, 0)),
            pl.BlockSpec((1, d, tn), lambda l, j: (l, 0, j)),
            pl.BlockSpec((1, 1, tn), lambda l, j: (l, 0, j)),
        ],
        out_specs=pl.BlockSpec((1, SUBLANES, tn), lambda l, j: (l, 0, j)),
        out_shape=jax.ShapeDtypeStruct((depth, SUBLANES, n), F32),
        compiler_params=pltpu.CompilerParams(
            dimension_semantics=("arbitrary", "arbitrary"), vmem_limit_bytes=VMEM_LIMIT),
        name="ada_mod",
    )(cp, w_ada, b_ada.reshape(depth, 1, n))
    return out[:, :b].reshape(depth, b, N_MOD, d)


def _layer_norm_rows(v, g, b):
    mu = jnp.mean(v, axis=-1, keepdims=True)
    vc = v - mu
    var = jnp.mean(vc * vc, axis=-1, keepdims=True)
    return vc * lax.rsqrt(var + LN_EPS) * g + b


def _carry_tail(ext_ref, j, ts, halo):
    nslab = ext_ref.shape[0]

    @pl.when(j == 0)
    def _():
        ext_ref[:, 0:halo, :] = jnp.zeros((nslab, halo, LANES), F32)

    @pl.when(j > 0)
    def _():
        ext_ref[:, 0:halo, :] = ext_ref[:, ts:ts + halo, :]


def _append_rows(ext_ref, cur, first):
    for cb in range(ext_ref.shape[0]):
        ext_ref[cb, first:first + cur.shape[0], :] = cur[:, cb * LANES:(cb + 1) * LANES]


def _first_argmax(vals):
    best = vals[0]
    idx = jnp.zeros_like(best)
    for k in range(1, len(vals)):
        better = vals[k] > best
        idx = jnp.where(better, float(k), idx)
        best = jnp.where(better, vals[k], best)
    return idx, best


def _pick(idx, vals):
    out = vals[0]
    for k in range(1, len(vals)):
        out = jnp.where(idx == float(k), vals[k], out)
    return out


def _gather_moe_rows(pos_ref, ys_hbm, ybuf, gsem, ts):
    g = pl.program_id(0) * pl.num_programs(1) + pl.program_id(1)
    n_steps = pl.num_programs(0) * pl.num_programs(1)

    def start(step, buf):
        base_row = step * (ts // LANES)
        for r0 in range(0, ts, PERMUTE_GROUP):
            slots = [pos_ref[base_row + (r0 + k) // LANES, (r0 + k) % LANES]
                     for k in range(PERMUTE_GROUP)]
            for k, slot in enumerate(slots):
                pltpu.make_async_copy(
                    ys_hbm.at[pl.ds(pl.multiple_of(slot * REC, REC), REC), :],
                    ybuf.at[buf, pl.ds((r0 + k) * REC, REC), :],
                    gsem.at[buf]).start(priority=1)

    @pl.when(g == 0)
    def _():
        start(0, 0)

    @pl.when(g + 1 < n_steps)
    def _():
        start(g + 1, (g + 1) % 2)

    cur = g % 2
    pltpu.make_async_copy(ys_hbm.at[pl.ds(0, ts * REC), :], ybuf.at[cur], gsem.at[cur]).wait()
    return ybuf.at[cur]


def _mixer_body(has_prev, ts, sb, *refs):
    refs = list(refs)
    if has_prev:
        pos_ref = refs.pop(0)
    x_ref = refs.pop(0)
    if has_prev:
        ys_hbm = refs.pop(0)
        modp_ref = refs.pop(0)
        gsem = refs.pop()
        ybuf = refs.pop()
    (mod_ref, n1g_ref, n2g_ref, win_ref, poolw_ref, pools_ref,
     cdw_ref, cdb_ref, clg_ref, clb_ref, cpw_ref, cpb_ref,
     slg_ref, slb_ref, sgw_ref, sgb_ref, scw_ref, ong_ref, wout_ref, rwt_ref, rb_ref,
     x1_ref, rec_ref, cls_ref,
     ea_ref, eg_ref, ed_ref) = refs

    j = pl.program_id(1)
    gw = GROUP_W
    nslab = gw // LANES
    if has_prev:
        yprev_ref = _gather_moe_rows(pos_ref, ys_hbm, ybuf, gsem, ts)

    sh1 = mod_ref[0, 0, 0:1, :]
    sc1 = mod_ref[0, 0, 1:2, :]
    g1 = mod_ref[0, 0, 2:3, :]
    sh2 = mod_ref[0, 0, 3:4, :]
    sc2 = mod_ref[0, 0, 4:5, :]
    gain1 = n1g_ref[0] * (1.0 + sc1)
    gain2 = n2g_ref[0] * (1.0 + sc2)

    _carry_tail(ea_ref, j, ts, POOL_HALO)
    _carry_tail(eg_ref, j, ts, CONV_HALO)
    _carry_tail(ed_ref, j, ts, SC_HALO)

    rowi = lax.broadcasted_iota(I32, (SG_CHUNK, SG_CHUNK), 0)
    coli = lax.broadcasted_iota(I32, (SG_CHUNK, SG_CHUNK), 1)
    tril = coli <= rowi
    wcat = jnp.concatenate(
        [jnp.where(tril, sgw_ref[0, hh], 0.0) for hh in range(SG_HEADS)], axis=1).astype(BF16)
    lane_g = lax.broadcasted_iota(I32, (SG_CHUNK, gw), 1) // (gw // SG_HEADS)
    zero_bf = jnp.zeros((SG_CHUNK, gw), BF16)
    lane = lax.broadcasted_iota(I32, (sb, LANES), 1)
    first = lane < POOL_CG

    for r0 in range(0, ts, sb):
        x = x_ref[r0:r0 + sb, :]
        if has_prev:
            yprev = jnp.concatenate([_rows_of(yprev_ref, c, sb, r0) for c in range(REC)], axis=1)
            x = x + modp_ref[0, 0, 5:6, :] * yprev

        r1 = lax.rsqrt(jnp.mean(x * x, axis=-1, keepdims=True) + EPS)
        h = x * r1 * gain1 + sh1
        p = jnp.dot(h.astype(BF16), win_ref[0], preferred_element_type=F32)

        a = p[:, 0:gw]
        _append_rows(ea_ref, a, POOL_HALO + r0)

        def a_shift(s, cb):
            return ea_ref[cb, POOL_HALO + r0 - s:POOL_HALO + r0 - s + sb, :]

        pos1 = (lax.broadcasted_iota(I32, (sb, LANES), 0) + (j * ts + r0 + 1)).astype(F32)
        a_lo = a[:, 0:LANES]
        s01 = a_lo + a_shift(1, 0)
        s03 = s01 + (a_shift(2, 0) + a_shift(3, 0))
        num_lo = jnp.where(first, s01, s03)
        den_lo = jnp.minimum(pos1, jnp.where(first, float(POOL_WINDOWS[0]), float(POOL_WINDOWS[1])))
        a_hi = a[:, LANES:gw]
        s07 = a_hi
        for s in range(1, 8):
            s07 = s07 + a_shift(s, 1)
        s815 = a_shift(8, 1)
        for s in range(9, 16):
            s815 = s815 + a_shift(s, 1)
        num_hi = jnp.where(first, s07, s07 + s815)
        den_hi = jnp.minimum(pos1, jnp.where(first, float(POOL_WINDOWS[2]), float(POOL_WINDOWS[3])))
        d_pool = jnp.concatenate([num_lo / den_lo - a_lo, num_hi / den_hi - a_hi], axis=1)
        y_a = jnp.dot(d_pool.astype(BF16), poolw_ref[0], preferred_element_type=F32) * pools_ref[0]

        glu = p[:, gw:2 * gw] * jax.nn.sigmoid(p[:, 2 * gw:3 * gw])
        _append_rows(eg_ref, glu, CONV_HALO + r0)
        conv_cols = []
        for cb in range(nslab):
            conv_chunks = []
            for rr in range(r0, r0 + sb, CONV_ROWS):
                acc = None
                for k in range(CONV_WIDTH):
                    off = CONV_HALO - (CONV_WIDTH - 1) + k + rr
                    term = (eg_ref[cb, off:off + CONV_ROWS, :]
                            * cdw_ref[0, k:k + 1, cb * LANES:(cb + 1) * LANES])
                    acc = term if acc is None else acc + term
                conv_chunks.append(acc)
            conv_cols.append(jnp.concatenate(conv_chunks, axis=0))
        hb = jnp.concatenate(conv_cols, axis=1) + cdb_ref[0]
        hb = jax.nn.silu(_layer_norm_rows(hb, clg_ref[0], clb_ref[0]))
        y_b = jnp.dot(hb.astype(BF16), cpw_ref[0], preferred_element_type=F32) + cpb_ref[0]

        u = p[:, 3 * gw:4 * gw]
        vln = _layer_norm_rows(p[:, 4 * gw:5 * gw], slg_ref[0], slb_ref[0]).astype(BF16)
        yc_chunks = []
        for n in range(sb // SG_CHUNK):
            vch = vln[n * SG_CHUNK:(n + 1) * SG_CHUNK, :]
            vbd = jnp.concatenate([jnp.where(lane_g == hh, vch, zero_bf) for hh in range(SG_HEADS)],
                                  axis=0)
            mixed = jnp.dot(wcat, vbd, preferred_element_type=F32) + sgb_ref[0]
            yc_chunks.append(u[n * SG_CHUNK:(n + 1) * SG_CHUNK, :] * mixed)
        y_c = jnp.concatenate(yc_chunks, axis=0)

        cx = p[:, 6 * gw:7 * gw] * p[:, 7 * gw:8 * gw]
        _append_rows(ed_ref, cx, SC_HALO + r0)
        cd_cols = []
        for cb in range(nslab):
            ls = slice(cb * LANES, (cb + 1) * LANES)
            at = SC_HALO + r0
            cd_cols.append(cx[:, ls] * scw_ref[0, 2:3, ls]
                           + ed_ref[cb, at - 1:at - 1 + sb, :] * scw_ref[0, 1:2, ls]
                           + ed_ref[cb, at - 2:at - 2 + sb, :] * scw_ref[0, 0:1, ls])
        y_d = p[:, 5 * gw:6 * gw] * jnp.concatenate(cd_cols, axis=1)

        normed = []
        for gi, yg in enumerate((y_a, y_b, y_c, y_d)):
            rg = lax.rsqrt(jnp.mean(yg * yg, axis=-1, keepdims=True) + EPS)
            normed.append((yg * rg * ong_ref[0, :, gi * gw:(gi + 1) * gw]).astype(BF16))
        yn = jnp.concatenate(normed, axis=1)
        x1 = x + g1 * jnp.dot(yn, wout_ref[0], preferred_element_type=F32)
        x1_ref[r0:r0 + sb, :] = x1

        r2 = lax.rsqrt(jnp.mean(x1 * x1, axis=-1, keepdims=True) + EPS)
        h2 = x1 * r2 * gain2 + sh2
        for c in range(REC):
            rec_ref[pl.ds(r0 * REC + c, sb, stride=REC), :] = h2[:, c * LANES:(c + 1) * LANES]
        logits = lax.dot_general(rwt_ref[...], h2.astype(BF16), (((1,), (1,)), ((), ())),
                                 preferred_element_type=F32)
        sel = jax.nn.sigmoid(logits) + rb_ref[...]
        epg = EXPERTS_PER_GROUP
        sel_s = [sel[jj * N_EXPERT_GROUPS:(jj + 1) * N_EXPERT_GROUPS, :] for jj in range(epg)]
        top2 = None
        for ja in range(epg):
            for jb in range(ja + 1, epg):
                pair = sel_s[ja] + sel_s[jb]
                top2 = pair if top2 is None else jnp.maximum(top2, pair)
        gidx, _ = _first_argmax([top2[g:g + 1, :] for g in range(N_EXPERT_GROUPS)])
        sel_c = [_pick(gidx, [sel_s[jj][g:g + 1, :] for g in range(N_EXPERT_GROUPS)])
                 for jj in range(epg)]
        i1, _ = _first_argmax(sel_c)
        i2, _ = _first_argmax([jnp.where(i1 == float(jj), -jnp.inf, sel_c[jj]) for jj in range(epg)])
        lo = jnp.minimum(i1, i2)
        hi = jnp.maximum(i1, i2)
        base = jnp.where(lo == 0.0, 0.0, jnp.where(lo == 1.0, 3.0, 5.0))
        cls = gidx * float(PAIRS_PER_GROUP) + base + (hi - lo - 1.0)
        cls_ref[0, :, r0:r0 + sb] = cls.astype(I32)


def _mixer_layer(l, x, ys_prev, pos_prev, mods, lw, ts, sb, batch):
    t, d = x.shape
    nj = t // batch // ts
    has_prev = ys_prev is not None
    tile = lambda bi, j, *_: (bi * nj + j, 0)

    args = [x]
    in_specs = [pl.BlockSpec((ts, d), tile)]
    if has_prev:
        args += [ys_prev, mods]
        in_specs += [pl.BlockSpec(memory_space=pl.ANY),
                     pl.BlockSpec((1, 1, N_MOD, d), lambda bi, j, *_: (l - 1, bi, 0, 0))]
    args += [mods]
    in_specs += [pl.BlockSpec((1, 1, N_MOD, d), lambda bi, j, *_: (l, bi, 0, 0))]
    for name in ("n1g", "n2g", "w_in", "pool_w", "pool_scale", "conv_dw", "conv_db", "conv_ln_g",
                 "conv_ln_b", "conv_pw", "conv_pb", "sg_ln_g", "sg_ln_b", "sg_w", "sg_b", "sc_w",
                 "out_norm_g", "w_out"):
        arr = lw[name]
        args.append(arr)
        in_specs.append(pl.BlockSpec((1,) + arr.shape[1:],
                                     lambda bi, j, *_, nd=arr.ndim: (l,) + (0,) * (nd - 1)))
    for name in ("router_wt", "router_b"):
        arr = lw[name]
        args.append(arr)
        in_specs.append(pl.BlockSpec(arr.shape, lambda bi, j, *_: (0, 0)))

    out_shape = (jax.ShapeDtypeStruct((t, d), F32),
                 jax.ShapeDtypeStruct((t * REC, LANES), F32),
                 jax.ShapeDtypeStruct((t // ts, 1, ts), I32))
    out_specs = (pl.BlockSpec((ts, d), tile),
                 pl.BlockSpec((ts * REC, LANES), tile),
                 pl.BlockSpec((1, 1, ts), lambda bi, j, *_: (bi * nj + j, 0, 0)))
    nslab = GROUP_W // LANES
    scratch = [pltpu.VMEM((nslab, POOL_HALO + ts, LANES), F32),
               pltpu.VMEM((nslab, CONV_HALO + ts, LANES), F32),
               pltpu.VMEM((nslab, SC_HALO + ts, LANES), F32)]
    if has_prev:
        scratch += [pltpu.VMEM((2, ts * REC, LANES), F32), pltpu.SemaphoreType.DMA((2,))]
        args = [pos_prev] + args
    grid_spec = pltpu.PrefetchScalarGridSpec(
        num_scalar_prefetch=1 if has_prev else 0,
        grid=(batch, nj),
        in_specs=in_specs,
        out_specs=out_specs,
        scratch_shapes=scratch,
    )
    return pl.pallas_call(
        functools.partial(_mixer_body, has_prev, ts, sb),
        grid_spec=grid_spec,
        out_shape=out_shape,
        compiler_params=pltpu.CompilerParams(
            dimension_semantics=("arbitrary", "arbitrary"), vmem_limit_bytes=VMEM_LIMIT),
        name="mixer_layer",
    )(*args)


def _plan_body(tm, cls_ref, pos_ref, meta_ref):
    cls = cls_ref[...]
    nrow = cls.shape[0]
    ii = lax.broadcasted_iota(I32, (LANES, LANES), 0)
    jj = lax.broadcasted_iota(I32, (LANES, LANES), 1)
    upper = jnp.where(ii <= jj, 1.0, 0.0).astype(BF16)
    ri = lax.broadcasted_iota(I32, (nrow, nrow), 0)
    rj = lax.broadcasted_iota(I32, (nrow, nrow), 1)
    lower = jnp.where(rj < ri, 1.0, 0.0).astype(BF16)
    lane_i = lax.broadcasted_iota(I32, (1, LANES), 1)
    tile_i = lane_i.astype(F32)
    start = jnp.zeros((1, 1), F32)
    cum_tiles = jnp.zeros((1, 1), F32)
    pos = jnp.zeros((nrow, LANES), F32)
    tclass = jnp.zeros((1, LANES), F32)
    seg_end = jnp.zeros((1, LANES), F32)
    for k in range(N_CLASSES):
        hit = cls == k
        oh = jnp.where(hit, 1.0, 0.0)
        ohb = oh.astype(BF16)
        incl = jnp.dot(ohb, upper, preferred_element_type=F32)
        above = jnp.sum(jnp.dot(lower, ohb, preferred_element_type=F32), axis=1, keepdims=True)
        rank = incl - oh + above
        pos = pos + jnp.where(hit, start + rank, 0.0)
        cnt = jnp.sum(jnp.sum(oh, axis=1, keepdims=True), axis=0, keepdims=True)
        n_tiles = jnp.floor((cnt + float(tm - 1)) * (1.0 / tm))
        start = start + n_tiles * float(tm)
        cum_tiles = cum_tiles + n_tiles
        tclass = tclass + jnp.where(tile_i >= cum_tiles, 1.0, 0.0)
        seg_end = jnp.where(lane_i == k, start, seg_end)
    pos_ref[...] = pos.astype(I32)

    valid = tclass < float(N_CLASSES)
    kc = jnp.minimum(tclass, float(N_CLASSES - 1))
    grp = jnp.zeros_like(kc)
    for g in range(1, N_EXPERT_GROUPS):
        grp = grp + jnp.where(kc >= float(g * PAIRS_PER_GROUP), 1.0, 0.0)
    pr = kc - grp * float(PAIRS_PER_GROUP)
    lo = jnp.where(pr >= 3.0, 1.0, 0.0) + jnp.where(pr >= 5.0, 1.0, 0.0)
    base = jnp.where(lo == 0.0, 0.0, jnp.where(lo == 1.0, 3.0, 5.0))
    hi = pr - base + lo + 1.0
    rows = {META_EA: grp * float(EXPERTS_PER_GROUP) + lo,
            META_EB: grp * float(EXPERTS_PER_GROUP) + hi,
            META_VALID: jnp.where(valid, 1.0, 0.0),
            META_BLK: jnp.minimum(tile_i, cum_tiles - 1.0),
            META_END: seg_end}
    rid = lax.broadcasted_iota(I32, (SUBLANES, LANES), 0)
    meta = jnp.zeros((SUBLANES, LANES), F32)
    for k, row in rows.items():
        meta = jnp.where(rid == k, row, meta)
    meta_ref[...] = meta.astype(I32)


def _route_plan(cls2d, tm):
    nrow = cls2d.shape[0]
    return pl.pallas_call(
        functools.partial(_plan_body, tm),
        in_specs=[pl.BlockSpec((nrow, LANES), lambda: (0, 0))],
        out_specs=(pl.BlockSpec((nrow, LANES), lambda: (0, 0)),
                   pl.BlockSpec((SUBLANES, LANES), lambda: (0, 0))),
        out_shape=(jax.ShapeDtypeStruct((nrow, LANES), I32),
                   jax.ShapeDtypeStruct((SUBLANES, LANES), I32)),
        name="route_plan",
    )(cls2d)


def _dispatch_body(tm, nt, pos_ref, meta_ref, src_ref, dst_hbm, zbuf, zsem, sem):
    g = pl.program_id(0)
    n = PERMUTE_ROWS

    @pl.when(g == 0)
    def _():
        zbuf[...] = jnp.zeros_like(zbuf)

        def zero_tile(first_slot, wait):
            cp = pltpu.make_async_copy(
                zbuf, dst_hbm.at[pl.ds(pl.multiple_of(first_slot * REC, REC), tm * REC), :], zsem)
            cp.wait() if wait else cp.start()

        for wait in (False, True):
            for k in range(N_CLASSES):
                prev_end = meta_ref[META_END, k - 1] if k > 0 else 0

                @pl.when(meta_ref[META_END, k] > prev_end)
                def _():
                    zero_tile(meta_ref[META_END, k] - tm, wait)

            for i in range(nt):
                @pl.when(meta_ref[META_VALID, i] == 0)
                def _():
                    zero_tile(i * tm, wait)

    base_row = g * (n // LANES)
    for r0 in range(0, n, PERMUTE_GROUP):
        slots = [pos_ref[base_row + (r0 + k) // LANES, (r0 + k) % LANES] for k in range(PERMUTE_GROUP)]
        for k, slot in enumerate(slots):
            pltpu.make_async_copy(
                src_ref.at[pl.ds((r0 + k) * REC, REC), :],
                dst_hbm.at[pl.ds(pl.multiple_of(slot * REC, REC), REC), :],
                sem).start(priority=k % 2)

    pltpu.make_async_copy(src_ref, dst_hbm.at[pl.ds(0, n * REC), :], sem).wait()


def _row_dispatch(rec, pos, meta, tm, nt):
    n_tok = pos.size
    grid_spec = pltpu.PrefetchScalarGridSpec(
        num_scalar_prefetch=2,
        grid=(n_tok // PERMUTE_ROWS,),
        in_specs=[pl.BlockSpec((PERMUTE_ROWS * REC, LANES), lambda g, pos, meta: (g, 0))],
        out_specs=pl.BlockSpec(memory_space=pl.ANY),
        scratch_shapes=[pltpu.VMEM((tm * REC, LANES), rec.dtype), pltpu.SemaphoreType.DMA(()),
                        pltpu.SemaphoreType.DMA(())],
    )
    return pl.pallas_call(
        functools.partial(_dispatch_body, tm, nt),
        grid_spec=grid_spec,
        out_shape=jax.ShapeDtypeStruct((nt * tm * REC, LANES), rec.dtype),
        compiler_params=pltpu.CompilerParams(
            dimension_semantics=("arbitrary",), vmem_limit_bytes=VMEM_LIMIT),
        name="row_dispatch",
    )(pos, meta, rec)


def _moe_body(tm, l, meta_ref, h_ref, rw_ref, wg_hbm, wu_hbm, wd_hbm, y_ref,
              wg_ref, wu_ref, wd_ref, sg_ref, su_ref, sd_ref, wsem):
    i = pl.program_id(0)

    @pl.when(i == 0)
    def _():
        def copies(e, buf):
            return (pltpu.make_async_copy(wg_hbm.at[l, e], sg_ref.at[buf], wsem.at[buf]),
                    pltpu.make_async_copy(wu_hbm.at[l, e], su_ref.at[buf], wsem.at[buf]),
                    pltpu.make_async_copy(wd_hbm.at[l, e], sd_ref.at[buf], wsem.at[buf]))

        for cp in copies(0, 0):
            cp.start()

        def stage(e, carry):
            buf = e % 2

            @pl.when(e + 1 < N_EXPERTS)
            def _():
                for cp in copies(e + 1, 1 - buf):
                    cp.start()

            for cp in copies(e, buf):
                cp.wait()
            wg_ref[e] = sg_ref[buf].astype(BF16)
            wu_ref[e] = su_ref[buf].astype(BF16)
            wd_ref[e] = sd_ref[buf].astype(BF16)
            return carry

        lax.fori_loop(0, N_EXPERTS, stage, 0)

    @pl.when(meta_ref[META_VALID, i] == 1)
    def _():
        ea = meta_ref[META_EA, i]
        eb = meta_ref[META_EB, i]
        xb = jnp.concatenate([_rows_of(h_ref, c, tm) for c in range(REC)], axis=1).astype(BF16)
        xf = xb.astype(F32)

        def expert(e):
            score = jax.nn.sigmoid(jnp.sum(xf * rw_ref[e].astype(F32), axis=-1, keepdims=True))
            hg = jnp.dot(xb, wg_ref[e], preferred_element_type=F32)
            hu = jnp.dot(xb, wu_ref[e], preferred_element_type=F32)
            return score, jax.nn.silu(hg) * hu

        sa, ga = expert(ea)
        sb, gb = expert(eb)
        acta = ga * (sa / (sa + sb))
        actb = gb * (sb / (sa + sb))
        y = (jnp.dot(acta.astype(BF16), wd_ref[ea], preferred_element_type=F32)
             + jnp.dot(actb.astype(BF16), wd_ref[eb], preferred_element_type=F32))
        for c in range(REC):
            y_ref[pl.ds(c, tm, stride=REC), :] = y[:, c * LANES:(c + 1) * LANES]

    @pl.when(meta_ref[META_VALID, i] == 0)
    def _():
        y_ref[...] = jnp.zeros_like(y_ref)


def _moe_tiles(l, hs, meta, router_rows, wg, wu, wd, tm, nt):
    d = D_MODEL
    whole = pl.BlockSpec(memory_space=pl.ANY)
    grid_spec = pltpu.PrefetchScalarGridSpec(
        num_scalar_prefetch=1,
        grid=(nt,),
        in_specs=[
            pl.BlockSpec((tm * REC, LANES), lambda i, meta: (meta[META_BLK, i], 0)),
            pl.BlockSpec((N_EXPERTS, 1, d), lambda i, meta: (0, 0, 0), pipeline_mode=pl.Buffered(1)),
            whole, whole, whole,
        ],
        out_specs=pl.BlockSpec((tm * REC, LANES), lambda i, meta: (i, 0)),
        scratch_shapes=[pltpu.VMEM((N_EXPERTS, d, D_EXPERT), BF16),
                        pltpu.VMEM((N_EXPERTS, d, D_EXPERT), BF16),
                        pltpu.VMEM((N_EXPERTS, D_EXPERT, d), BF16),
                        pltpu.VMEM((2, d, D_EXPERT), F32),
                        pltpu.VMEM((2, d, D_EXPERT), F32),
                        pltpu.VMEM((2, D_EXPERT, d), F32),
                        pltpu.SemaphoreType.DMA((2,))],
    )
    return pl.pallas_call(
        functools.partial(_moe_body, tm, l),
        grid_spec=grid_spec,
        out_shape=jax.ShapeDtypeStruct((nt * tm * REC, LANES), F32),
        compiler_params=pltpu.CompilerParams(
            dimension_semantics=("arbitrary",), vmem_limit_bytes=VMEM_LIMIT),
        name="moe_tiles",
    )(meta, hs, router_rows, wg, wu, wd)


def _final_body(ts, pos_ref, x_ref, ys_hbm, mod_ref, g_ref, o_ref, ybuf, gsem):
    y_ref = _gather_moe_rows(pos_ref, ys_hbm, ybuf, gsem, ts)
    y = jnp.concatenate([_rows_of(y_ref, c, ts) for c in range(REC)], axis=1)
    x = x_ref[...] + mod_ref[0, 0, 5:6, :] * y
    r = lax.rsqrt(jnp.mean(x * x, axis=-1, keepdims=True) + EPS)
    o_ref[...] = x * r * g_ref[...]


def _final_norm(x, ys, pos, mods, final_g, ts, batch):
    t, d = x.shape
    nj = t // batch // ts
    last = mods.shape[0] - 1
    tile = lambda bi, j, pos: (bi * nj + j, 0)
    grid_spec = pltpu.PrefetchScalarGridSpec(
        num_scalar_prefetch=1,
        grid=(batch, nj),
        in_specs=[pl.BlockSpec((ts, d), tile), pl.BlockSpec(memory_space=pl.ANY),
                  pl.BlockSpec((1, 1, N_MOD, d), lambda bi, j, pos: (last, bi, 0, 0)),
                  pl.BlockSpec((1, d), lambda bi, j, pos: (0, 0))],
        out_specs=pl.BlockSpec((ts, d), tile),
        scratch_shapes=[pltpu.VMEM((2, ts * REC, LANES), F32), pltpu.SemaphoreType.DMA((2,))],
    )
    return pl.pallas_call(
        functools.partial(_final_body, ts),
        grid_spec=grid_spec,
        out_shape=jax.ShapeDtypeStruct((t, d), F32),
        compiler_params=pltpu.CompilerParams(
            dimension_semantics=("arbitrary", "arbitrary"), vmem_limit_bytes=VMEM_LIMIT),
        name="final_norm",
    )(pos, x, ys, mods, final_g.reshape(1, d))


def _stacked_weights(norm1_g, norm2_g, w_in, pool_w, pool_scale, conv_dw, conv_db, conv_ln_g,
                     conv_ln_b, conv_pw, conv_pb, sg_ln_g, sg_ln_b, sg_w, sg_b, sc_w, out_norm_g, w_out):
    depth = w_in.shape[0]
    row = lambda v: v.reshape(depth, 1, -1)
    pool_bd = jnp.zeros((depth, GROUP_W, GROUP_W), pool_w.dtype)
    for i in range(len(POOL_WINDOWS)):
        pool_bd = pool_bd.at[:, i * POOL_CG:(i + 1) * POOL_CG, i * POOL_CG:(i + 1) * POOL_CG].set(
            pool_w[:, i])
    return {
        "n1g": row(norm1_g), "n2g": row(norm2_g),
        "w_in": w_in.astype(BF16),
        "pool_w": pool_bd.astype(BF16),
        "pool_scale": row(pool_scale),
        "conv_dw": jnp.pad(conv_dw, ((0, 0), (0, 1), (0, 0))),
        "conv_db": row(conv_db), "conv_ln_g": row(conv_ln_g), "conv_ln_b": row(conv_ln_b),
        "conv_pw": conv_pw.astype(BF16), "conv_pb": row(conv_pb),
        "sg_ln_g": row(sg_ln_g), "sg_ln_b": row(sg_ln_b),
        "sg_w": sg_w,
        "sg_b": jnp.repeat(jnp.swapaxes(sg_b, 1, 2), GROUP_W // SG_HEADS, axis=2),
        "sc_w": jnp.pad(sc_w, ((0, 0), (0, SUBLANES - SC_WIDTH), (0, 0))),
        "out_norm_g": row(out_norm_g),
        "w_out": w_out.astype(BF16),
    }


def kernel(x, c, norm1_g, norm2_g, w_ada, b_ada, w_in, pool_w, pool_scale, conv_dw, conv_db, conv_ln_g, conv_ln_b, conv_pw, conv_pb, sg_ln_g, sg_ln_b, sg_w, sg_b, sc_w, out_norm_g, w_out, router_w, router_bias, exp_w_gate, exp_w_up, exp_w_down, final_g):
    b, s, d = x.shape
    depth = w_in.shape[0]
    t = b * s
    ts = 512
    ts_first = 1024
    sb = 256
    tm = 256
    nt = t // tm + N_CLASSES
    assert d == D_MODEL and t % LANES == 0 and s % ts_first == 0 and t % tm == 0 and nt <= LANES
    assert t % PERMUTE_ROWS == 0 and ts_first % ts == 0 and ts % sb == 0 and sb % SG_CHUNK == 0

    mods = _ada_mod(c, w_ada, b_ada)

    perm = lambda v: v.reshape(N_EXPERT_GROUPS, EXPERTS_PER_GROUP, -1).transpose(1, 0, 2).reshape(
        N_EXPERTS, -1)
    lw = _stacked_weights(norm1_g, norm2_g, w_in, pool_w, pool_scale, conv_dw, conv_db, conv_ln_g,
                          conv_ln_b, conv_pw, conv_pb, sg_ln_g, sg_ln_b, sg_w, sg_b, sc_w, out_norm_g,
                          w_out)
    lw["router_wt"] = perm(router_w.T).astype(BF16)
    lw["router_b"] = perm(router_bias.reshape(N_EXPERTS, 1))
    router_rows = router_w.T.astype(BF16).reshape(N_EXPERTS, 1, d)

    xcur, ys, pos = x.reshape(t, d), None, None
    for l in range(depth):
        xcur, rec, cls = _mixer_layer(l, xcur, ys, pos, mods, lw, ts if l else ts_first, sb, b)
        pos, meta = _route_plan(cls.reshape(t // LANES, LANES), tm)
        hs = _row_dispatch(rec, pos, meta, tm, nt)
        ys = _moe_tiles(l, hs, meta, router_rows, exp_w_gate, exp_w_up, exp_w_down, tm, nt)
    return _final_norm(xcur, ys, pos, mods, final_g, ts, b).reshape(b, s, d)
```

```python
import functools

import jax
import jax.numpy as jnp
from jax import lax
from jax.experimental import pallas as pl
from jax.experimental.pallas import tpu as pltpu

D_MODEL = 1024
GROUP_W = 256
POOL_WINDOWS = (2, 4, 8, 16)
POOL_CG = 64
CONV_WIDTH = 31
SG_CHUNK = 128
SG_HEADS = 4
SC_WIDTH = 3
N_EXPERTS = 16
N_EXPERT_GROUPS = 4
EXPERTS_PER_GROUP = 4
PAIRS_PER_GROUP = 6
N_CLASSES = N_EXPERT_GROUPS * PAIRS_PER_GROUP
D_EXPERT = 256
N_MOD = 6
EPS = 1e-6
LN_EPS = 1e-5

LANES = 128
SUBLANES = 8
REC = SUBLANES
CONV_HALO = 32
POOL_HALO = 16
SC_HALO = 8
CONV_ROWS = 32
META_EA, META_EB, META_VALID, META_BLK, META_END = 0, 1, 2, 3, 4
PERMUTE_ROWS = 2048
PERMUTE_GROUP = 16
MOE_TILES_PER_STEP = 2
VMEM_LIMIT = 56 * 1024 * 1024

F32 = jnp.float32
BF16 = jnp.bfloat16
I32 = jnp.int32


def _rows_of(ref, c, n, first=0):
    return ref[pl.ds(first * REC + c, n, stride=REC), :]


def _ada_body(c_ref, w_ref, b_ref, o_ref):
    ca = jax.nn.silu(c_ref[...])
    o_ref[0] = jnp.dot(ca.astype(BF16), w_ref[0].astype(BF16), preferred_element_type=F32) + b_ref[0]


def _ada_mod(c, w_ada, b_ada):
    depth, d, n = w_ada.shape
    b = c.shape[0]
    tn = 2048
    cp = jnp.pad(c, ((0, SUBLANES - b), (0, 0)))
    out = pl.pallas_call(
        _ada_body,
        grid=(depth, n // tn),
        in_specs=[
            pl.BlockSpec((SUBLANES, d), lambda l, j: (0, 0)),
            pl.BlockSpec((1, d, tn), lambda l, j: (l, 0, j)),
            pl.BlockSpec((1, 1, tn), lambda l, j: (l, 0, j)),
        ],
        out_specs=pl.BlockSpec((1, SUBLANES, tn), lambda l, j: (l, 0, j)),
        out_shape=jax.ShapeDtypeStruct((depth, SUBLANES, n), F32),
        compiler_params=pltpu.CompilerParams(
            dimension_semantics=("arbitrary", "arbitrary"), vmem_limit_bytes=VMEM_LIMIT),
        name="ada_mod",
    )(cp, w_ada, b_ada.reshape(depth, 1, n))
    return out[:, :b].reshape(depth, b, N_MOD, d)


def _layer_norm_rows(v, g, b):
    mu = jnp.mean(v, axis=-1, keepdims=True)
    vc = v - mu
    var = jnp.mean(vc * vc, axis=-1, keepdims=True)
    return vc * lax.rsqrt(var + LN_EPS) * g + b


def _carry_tail(ext_ref, j, ts, halo):
    nslab = ext_ref.shape[0]

    @pl.when(j == 0)
    def _():
        ext_ref[:, 0:halo, :] = jnp.zeros((nslab, halo, LANES), F32)

    @pl.when(j > 0)
    def _():
        ext_ref[:, 0:halo, :] = ext_ref[:, ts:ts + halo, :]


def _append_rows(ext_ref, cur, first):
    for cb in range(ext_ref.shape[0]):
        ext_ref[cb, first:first + cur.shape[0], :] = cur[:, cb * LANES:(cb + 1) * LANES]


def _first_argmax(vals):
    best = vals[0]
    idx = jnp.zeros_like(best)
    for k in range(1, len(vals)):
        better = vals[k] > best
        idx = jnp.where(better, float(k), idx)
        best = jnp.where(better, vals[k], best)
    return idx, best


def _pick(idx, vals):
    out = vals[0]
    for k in range(1, len(vals)):
        out = jnp.where(idx == float(k), vals[k], out)
    return out


def _gather_moe_rows(pos_ref, ys_hbm, ybuf, gsem, ts):
    g = pl.program_id(0) * pl.num_programs(1) + pl.program_id(1)
    n_steps = pl.num_programs(0) * pl.num_programs(1)

    def start(step, buf):
        base_row = step * (ts // LANES)
        for r0 in range(0, ts, PERMUTE_GROUP):
            slots = [pos_ref[base_row + (r0 + k) // LANES, (r0 + k) % LANES]
                     for k in range(PERMUTE_GROUP)]
            for k, slot in enumerate(slots):
                pltpu.make_async_copy(
                    ys_hbm.at[pl.ds(pl.multiple_of(slot * REC, REC), REC), :],
                    ybuf.at[buf, pl.ds((r0 + k) * REC, REC), :],
                    gsem.at[buf]).start(priority=k % 2)

    @pl.when(g == 0)
    def _():
        start(0, 0)

    @pl.when(g + 1 < n_steps)
    def _():
        start(g + 1, (g + 1) % 2)

    cur = g % 2
    pltpu.make_async_copy(ys_hbm.at[pl.ds(0, ts * REC), :], ybuf.at[cur], gsem.at[cur]).wait()
    return ybuf.at[cur]


def _mixer_body(has_prev, ts, sb, *refs):
    refs = list(refs)
    if has_prev:
        pos_ref = refs.pop(0)
    x_ref = refs.pop(0)
    if has_prev:
        ys_hbm = refs.pop(0)
        modp_ref = refs.pop(0)
        gsem = refs.pop()
        ybuf = refs.pop()
    (mod_ref, n1g_ref, n2g_ref, win_ref, poolw_ref, pools_ref,
     cdw_ref, cdb_ref, clg_ref, clb_ref, cpw_ref, cpb_ref,
     slg_ref, slb_ref, sgw_ref, sgb_ref, scw_ref, ong_ref, wout_ref, rwt_ref, rb_ref,
     x1_ref, rec_ref, cls_ref,
     ea_ref, eg_ref, ed_ref) = refs

    j = pl.program_id(1)
    gw = GROUP_W
    nslab = gw // LANES
    if has_prev:
        yprev_ref = _gather_moe_rows(pos_ref, ys_hbm, ybuf, gsem, ts)

    sh1 = mod_ref[0, 0, 0:1, :]
    sc1 = mod_ref[0, 0, 1:2, :]
    g1 = mod_ref[0, 0, 2:3, :]
    sh2 = mod_ref[0, 0, 3:4, :]
    sc2 = mod_ref[0, 0, 4:5, :]
    gain1 = n1g_ref[0] * (1.0 + sc1)
    gain2 = n2g_ref[0] * (1.0 + sc2)

    _carry_tail(ea_ref, j, ts, POOL_HALO)
    _carry_tail(eg_ref, j, ts, CONV_HALO)
    _carry_tail(ed_ref, j, ts, SC_HALO)

    rowi = lax.broadcasted_iota(I32, (SG_CHUNK, SG_CHUNK), 0)
    coli = lax.broadcasted_iota(I32, (SG_CHUNK, SG_CHUNK), 1)
    tril = coli <= rowi
    wcat = jnp.concatenate(
        [jnp.where(tril, sgw_ref[0, hh], 0.0) for hh in range(SG_HEADS)], axis=1).astype(BF16)
    lane_g = lax.broadcasted_iota(I32, (SG_CHUNK, gw), 1) // (gw // SG_HEADS)
    zero_bf = jnp.zeros((SG_CHUNK, gw), BF16)
    lane = lax.broadcasted_iota(I32, (sb, LANES), 1)
    first = lane < POOL_CG

    for r0 in range(0, ts, sb):
        x = x_ref[r0:r0 + sb, :]
        if has_prev:
            yprev = jnp.concatenate([_rows_of(yprev_ref, c, sb, r0) for c in range(REC)], axis=1)
            x = x + modp_ref[0, 0, 5:6, :] * yprev

        r1 = lax.rsqrt(jnp.mean(x * x, axis=-1, keepdims=True) + EPS)
        h = x * r1 * gain1 + sh1
        p = jnp.dot(h.astype(BF16), win_ref[0], preferred_element_type=F32)

        a = p[:, 0:gw]
        _append_rows(ea_ref, a, POOL_HALO + r0)

        def a_shift(s, cb):
            return ea_ref[cb, POOL_HALO + r0 - s:POOL_HALO + r0 - s + sb, :]

        pos1 = (lax.broadcasted_iota(I32, (sb, LANES), 0) + (j * ts + r0 + 1)).astype(F32)
        a_lo = a[:, 0:LANES]
        s01 = a_lo + a_shift(1, 0)
        s03 = s01 + (a_shift(2, 0) + a_shift(3, 0))
        num_lo = jnp.where(first, s01, s03)
        den_lo = jnp.minimum(pos1, jnp.where(first, float(POOL_WINDOWS[0]), float(POOL_WINDOWS[1])))
        a_hi = a[:, LANES:gw]
        s07 = a_hi
        for s in range(1, 8):
            s07 = s07 + a_shift(s, 1)
        s815 = a_shift(8, 1)
        for s in range(9, 16):
            s815 = s815 + a_shift(s, 1)
        num_hi = jnp.where(first, s07, s07 + s815)
        den_hi = jnp.minimum(pos1, jnp.where(first, float(POOL_WINDOWS[2]), float(POOL_WINDOWS[3])))
        d_pool = jnp.concatenate([num_lo / den_lo - a_lo, num_hi / den_hi - a_hi], axis=1)
        y_a = jnp.dot(d_pool.astype(BF16), poolw_ref[0], preferred_element_type=F32) * pools_ref[0]

        glu = p[:, gw:2 * gw] * jax.nn.sigmoid(p[:, 2 * gw:3 * gw])
        _append_rows(eg_ref, glu, CONV_HALO + r0)
        conv_cols = []
        for cb in range(nslab):
            conv_chunks = []
            for rr in range(r0, r0 + sb, CONV_ROWS):
                acc = None
                for k in range(CONV_WIDTH):
                    off = CONV_HALO - (CONV_WIDTH - 1) + k + rr
                    term = (eg_ref[cb, off:off + CONV_ROWS, :]
                            * cdw_ref[0, k:k + 1, cb * LANES:(cb + 1) * LANES])
                    acc = term if acc is None else acc + term
                conv_chunks.append(acc)
            conv_cols.append(jnp.concatenate(conv_chunks, axis=0))
        hb = jnp.concatenate(conv_cols, axis=1) + cdb_ref[0]
        hb = jax.nn.silu(_layer_norm_rows(hb, clg_ref[0], clb_ref[0]))
        y_b = jnp.dot(hb.astype(BF16), cpw_ref[0], preferred_element_type=F32) + cpb_ref[0]

        u = p[:, 3 * gw:4 * gw]
        vln = _layer_norm_rows(p[:, 4 * gw:5 * gw], slg_ref[0], slb_ref[0]).astype(BF16)
        yc_chunks = []
        for n in range(sb // SG_CHUNK):
            vch = vln[n * SG_CHUNK:(n + 1) * SG_CHUNK, :]
            vbd = jnp.concatenate([jnp.where(lane_g == hh, vch, zero_bf) for hh in range(SG_HEADS)],
                                  axis=0)
            mixed = jnp.dot(wcat, vbd, preferred_element_type=F32) + sgb_ref[0]
            yc_chunks.append(u[n * SG_CHUNK:(n + 1) * SG_CHUNK, :] * mixed)
        y_c = jnp.concatenate(yc_chunks, axis=0)

        cx = p[:, 6 * gw:7 * gw] * p[:, 7 * gw:8 * gw]
        _append_rows(ed_ref, cx, SC_HALO + r0)
        cd_cols = []
        for cb in range(nslab):
            ls = slice(cb * LANES, (cb + 1) * LANES)
            at = SC_HALO + r0
            cd_cols.append(cx[:, ls] * scw_ref[0, 2:3, ls]
                           + ed_ref[cb, at - 1:at - 1 + sb, :] * scw_ref[0, 1:2, ls]
                           + ed_ref[cb, at - 2:at - 2 + sb, :] * scw_ref[0, 0:1, ls])
        y_d = p[:, 5 * gw:6 * gw] * jnp.concatenate(cd_cols, axis=1)

        normed = []
        for gi, yg in enumerate((y_a, y_b, y_c, y_d)):
            rg = lax.rsqrt(jnp.mean(yg * yg, axis=-1, keepdims=True) + EPS)
            normed.append((yg * rg * ong_ref[0, :, gi * gw:(gi + 1) * gw]).astype(BF16))
        yn = jnp.concatenate(normed, axis=1)
        x1 = x + g1 * jnp.dot(yn, wout_ref[0], preferred_element_type=F32)
        x1_ref[r0:r0 + sb, :] = x1

        r2 = lax.rsqrt(jnp.mean(x1 * x1, axis=-1, keepdims=True) + EPS)
        h2 = x1 * r2 * gain2 + sh2
        for c in range(REC):
            rec_ref[pl.ds(r0 * REC + c, sb, stride=REC), :] = h2[:, c * LANES:(c + 1) * LANES]
        logits = lax.dot_general(rwt_ref[...], h2.astype(BF16), (((1,), (1,)), ((), ())),
                                 preferred_element_type=F32)
        sel = jax.nn.sigmoid(logits) + rb_ref[...]
        epg = EXPERTS_PER_GROUP
        sel_s = [sel[jj * N_EXPERT_GROUPS:(jj + 1) * N_EXPERT_GROUPS, :] for jj in range(epg)]
        top2 = None
        for ja in range(epg):
            for jb in range(ja + 1, epg):
                pair = sel_s[ja] + sel_s[jb]
                top2 = pair if top2 is None else jnp.maximum(top2, pair)
        gidx, _ = _first_argmax([top2[g:g + 1, :] for g in range(N_EXPERT_GROUPS)])
        sel_c = [_pick(gidx, [sel_s[jj][g:g + 1, :] for g in range(N_EXPERT_GROUPS)])
                 for jj in range(epg)]
        i1, _ = _first_argmax(sel_c)
        i2, _ = _first_argmax([jnp.where(i1 == float(jj), -jnp.inf, sel_c[jj]) for jj in range(epg)])
        lo = jnp.minimum(i1, i2)
        hi = jnp.maximum(i1, i2)
        base = jnp.where(lo == 0.0, 0.0, jnp.where(lo == 1.0, 3.0, 5.0))
        cls = gidx * float(PAIRS_PER_GROUP) + base + (hi - lo - 1.0)
        cls_ref[0, :, r0:r0 + sb] = cls.astype(I32)


def _mixer_layer(l, x, ys_prev, pos_prev, mods, lw, ts, sb, batch):
    t, d = x.shape
    nj = t // batch // ts
    has_prev = ys_prev is not None
    tile = lambda bi, j, *_: (bi * nj + j, 0)

    args = [x]
    in_specs = [pl.BlockSpec((ts, d), tile)]
    if has_prev:
        args += [ys_prev, mods]
        in_specs += [pl.BlockSpec(memory_space=pl.ANY),
                     pl.BlockSpec((1, 1, N_MOD, d), lambda bi, j, *_: (l - 1, bi, 0, 0))]
    args += [mods]
    in_specs += [pl.BlockSpec((1, 1, N_MOD, d), lambda bi, j, *_: (l, bi, 0, 0))]
    for name in ("n1g", "n2g", "w_in", "pool_w", "pool_scale", "conv_dw", "conv_db", "conv_ln_g",
                 "conv_ln_b", "conv_pw", "conv_pb", "sg_ln_g", "sg_ln_b", "sg_w", "sg_b", "sc_w",
                 "out_norm_g", "w_out"):
        arr = lw[name]
        args.append(arr)
        in_specs.append(pl.BlockSpec((1,) + arr.shape[1:],
                                     lambda bi, j, *_, nd=arr.ndim: (l,) + (0,) * (nd - 1)))
    for name in ("router_wt", "router_b"):
        arr = lw[name]
        args.append(arr)
        in_specs.append(pl.BlockSpec(arr.shape, lambda bi, j, *_: (0, 0)))

    out_shape = (jax.ShapeDtypeStruct((t, d), F32),
                 jax.ShapeDtypeStruct((t * REC, LANES), F32),
                 jax.ShapeDtypeStruct((t // ts, 1, ts), I32))
    out_specs = (pl.BlockSpec((ts, d), tile),
                 pl.BlockSpec((ts * REC, LANES), tile),
                 pl.BlockSpec((1, 1, ts), lambda bi, j, *_: (bi * nj + j, 0, 0)))
    nslab = GROUP_W // LANES
    scratch = [pltpu.VMEM((nslab, POOL_HALO + ts, LANES), F32),
               pltpu.VMEM((nslab, CONV_HALO + ts, LANES), F32),
               pltpu.VMEM((nslab, SC_HALO + ts, LANES), F32)]
    if has_prev:
        scratch += [pltpu.VMEM((2, ts * REC, LANES), F32), pltpu.SemaphoreType.DMA((2,))]
        args = [pos_prev] + args
    grid_spec = pltpu.PrefetchScalarGridSpec(
        num_scalar_prefetch=1 if has_prev else 0,
        grid=(batch, nj),
        in_specs=in_specs,
        out_specs=out_specs,
        scratch_shapes=scratch,
    )
    return pl.pallas_call(
        functools.partial(_mixer_body, has_prev, ts, sb),
        grid_spec=grid_spec,
        out_shape=out_shape,
        compiler_params=pltpu.CompilerParams(
            dimension_semantics=("arbitrary", "arbitrary"), vmem_limit_bytes=VMEM_LIMIT),
        name="mixer_layer",
    )(*args)


def _plan_body(tm, cls_ref, pos_ref, meta_ref):
    cls = cls_ref[...]
    nrow = cls.shape[0]
    ii = lax.broadcasted_iota(I32, (LANES, LANES), 0)
    jj = lax.broadcasted_iota(I32, (LANES, LANES), 1)
    upper = jnp.where(ii <= jj, 1.0, 0.0).astype(BF16)
    ri = lax.broadcasted_iota(I32, (nrow, nrow), 0)
    rj = lax.broadcasted_iota(I32, (nrow, nrow), 1)
    lower = jnp.where(rj < ri, 1.0, 0.0).astype(BF16)
    lane_i = lax.broadcasted_iota(I32, (1, LANES), 1)
    tile_i = lane_i.astype(F32)
    start = jnp.zeros((1, 1), F32)
    cum_tiles = jnp.zeros((1, 1), F32)
    pos = jnp.zeros((nrow, LANES), F32)
    tclass = jnp.zeros((1, LANES), F32)
    seg_end = jnp.zeros((1, LANES), F32)
    for k in range(N_CLASSES):
        hit = cls == k
        oh = jnp.where(hit, 1.0, 0.0)
        ohb = oh.astype(BF16)
        incl = jnp.dot(ohb, upper, preferred_element_type=F32)
        above = jnp.sum(jnp.dot(lower, ohb, preferred_element_type=F32), axis=1, keepdims=True)
        rank = incl - oh + above
        pos = pos + jnp.where(hit, start + rank, 0.0)
        cnt = jnp.sum(jnp.sum(oh, axis=1, keepdims=True), axis=0, keepdims=True)
        n_tiles = jnp.floor((cnt + float(tm - 1)) * (1.0 / tm))
        start = start + n_tiles * float(tm)
        cum_tiles = cum_tiles + n_tiles
        tclass = tclass + jnp.where(tile_i >= cum_tiles, 1.0, 0.0)
        seg_end = jnp.where(lane_i == k, start, seg_end)
    pos_ref[...] = pos.astype(I32)

    valid = tclass < float(N_CLASSES)
    kc = jnp.minimum(tclass, float(N_CLASSES - 1))
    grp = jnp.zeros_like(kc)
    for g in range(1, N_EXPERT_GROUPS):
        grp = grp + jnp.where(kc >= float(g * PAIRS_PER_GROUP), 1.0, 0.0)
    pr = kc - grp * float(PAIRS_PER_GROUP)
    lo = jnp.where(pr >= 3.0, 1.0, 0.0) + jnp.where(pr >= 5.0, 1.0, 0.0)
    base = jnp.where(lo == 0.0, 0.0, jnp.where(lo == 1.0, 3.0, 5.0))
    hi = pr - base + lo + 1.0
    rows = {META_EA: grp * float(EXPERTS_PER_GROUP) + lo,
            META_EB: grp * float(EXPERTS_PER_GROUP) + hi,
            META_VALID: jnp.where(valid, 1.0, 0.0),
            META_BLK: jnp.minimum(tile_i, cum_tiles - 1.0),
            META_END: seg_end}
    rid = lax.broadcasted_iota(I32, (SUBLANES, LANES), 0)
    meta = jnp.zeros((SUBLANES, LANES), F32)
    for k, row in rows.items():
        meta = jnp.where(rid == k, row, meta)
    meta_ref[...] = meta.astype(I32)


def _route_plan(cls2d, tm):
    nrow = cls2d.shape[0]
    return pl.pallas_call(
        functools.partial(_plan_body, tm),
        in_specs=[pl.BlockSpec((nrow, LANES), lambda: (0, 0))],
        out_specs=(pl.BlockSpec((nrow, LANES), lambda: (0, 0)),
                   pl.BlockSpec((SUBLANES, LANES), lambda: (0, 0))),
        out_shape=(jax.ShapeDtypeStruct((nrow, LANES), I32),
                   jax.ShapeDtypeStruct((SUBLANES, LANES), I32)),
        name="route_plan",
    )(cls2d)


def _dispatch_body(tm, nt, pos_ref, meta_ref, src_ref, dst_hbm, zbuf, zsem, sem):
    g = pl.program_id(0)
    n = PERMUTE_ROWS

    @pl.when(g == 0)
    def _():
        zbuf[...] = jnp.zeros_like(zbuf)

        def zero_tile(first_slot, wait):
            cp = pltpu.make_async_copy(
                zbuf, dst_hbm.at[pl.ds(pl.multiple_of(first_slot * REC, REC), tm * REC), :], zsem)
            cp.wait() if wait else cp.start()

        for wait in (False, True):
            for k in range(N_CLASSES):
                prev_end = meta_ref[META_END, k - 1] if k > 0 else 0

                @pl.when(meta_ref[META_END, k] > prev_end)
                def _():
                    zero_tile(meta_ref[META_END, k] - tm, wait)

            for i in range(nt):
                @pl.when(meta_ref[META_VALID, i] == 0)
                def _():
                    zero_tile(i * tm, wait)

    base_row = g * (n // LANES)
    for r0 in range(0, n, PERMUTE_GROUP):
        slots = [pos_ref[base_row + (r0 + k) // LANES, (r0 + k) % LANES] for k in range(PERMUTE_GROUP)]
        for k, slot in enumerate(slots):
            pltpu.make_async_copy(
                src_ref.at[pl.ds((r0 + k) * REC, REC), :],
                dst_hbm.at[pl.ds(pl.multiple_of(slot * REC, REC), REC), :],
                sem).start(priority=k % 2)

    pltpu.make_async_copy(src_ref, dst_hbm.at[pl.ds(0, n * REC), :], sem).wait()


def _row_dispatch(rec, pos, meta, tm, nt):
    n_tok = pos.size
    grid_spec = pltpu.PrefetchScalarGridSpec(
        num_scalar_prefetch=2,
        grid=(n_tok // PERMUTE_ROWS,),
        in_specs=[pl.BlockSpec((PERMUTE_ROWS * REC, LANES), lambda g, pos, meta: (g, 0))],
        out_specs=pl.BlockSpec(memory_space=pl.ANY),
        scratch_shapes=[pltpu.VMEM((tm * REC, LANES), rec.dtype), pltpu.SemaphoreType.DMA(()),
                        pltpu.SemaphoreType.DMA(())],
    )
    return pl.pallas_call(
        functools.partial(_dispatch_body, tm, nt),
        grid_spec=grid_spec,
        out_shape=jax.ShapeDtypeStruct((nt * tm * REC, LANES), rec.dtype),
        compiler_params=pltpu.CompilerParams(
            dimension_semantics=("arbitrary",), vmem_limit_bytes=VMEM_LIMIT),
        name="row_dispatch",
    )(pos, meta, rec)


def _moe_body(tm, l, meta_ref, h_ref, rw_ref, wg_hbm, wu_hbm, wd_hbm, y_ref,
              wg_ref, wu_ref, wd_ref, sg_ref, su_ref, sd_ref, wsem):
    i = pl.program_id(0)

    @pl.when(i == 0)
    def _():
        def copies(e, buf):
            return (pltpu.make_async_copy(wg_hbm.at[l, e], sg_ref.at[buf], wsem.at[buf]),
                    pltpu.make_async_copy(wu_hbm.at[l, e], su_ref.at[buf], wsem.at[buf]),
                    pltpu.make_async_copy(wd_hbm.at[l, e], sd_ref.at[buf], wsem.at[buf]))

        for cp in copies(0, 0):
            cp.start()

        def stage(e, carry):
            buf = e % 2

            @pl.when(e + 1 < N_EXPERTS)
            def _():
                for cp in copies(e + 1, 1 - buf):
                    cp.start()

            for cp in copies(e, buf):
                cp.wait()
            wg_ref[e] = sg_ref[buf].astype(BF16)
            wu_ref[e] = su_ref[buf].astype(BF16)
            wd_ref[e] = sd_ref[buf].astype(BF16)
            return carry

        lax.fori_loop(0, N_EXPERTS, stage, 0)

    def one_tile(tile, first):
        @pl.when(meta_ref[META_VALID, tile] == 1)
        def _():
            ea = meta_ref[META_EA, tile]
            eb = meta_ref[META_EB, tile]
            xb = jnp.concatenate([_rows_of(h_ref, c, tm, first) for c in range(REC)], axis=1).astype(BF16)
            xf = xb.astype(F32)

            def expert(e):
                score = jax.nn.sigmoid(jnp.sum(xf * rw_ref[e].astype(F32), axis=-1, keepdims=True))
                hg = jnp.dot(xb, wg_ref[e], preferred_element_type=F32)
                hu = jnp.dot(xb, wu_ref[e], preferred_element_type=F32)
                return score, jax.nn.silu(hg) * hu

            sa, ga = expert(ea)
            sb, gb = expert(eb)
            acta = ga * (sa / (sa + sb))
            actb = gb * (sb / (sa + sb))
            y = (jnp.dot(acta.astype(BF16), wd_ref[ea], preferred_element_type=F32)
                 + jnp.dot(actb.astype(BF16), wd_ref[eb], preferred_element_type=F32))
            for c in range(REC):
                y_ref[pl.ds(first * REC + c, tm, stride=REC), :] = y[:, c * LANES:(c + 1) * LANES]

        @pl.when(meta_ref[META_VALID, tile] == 0)
        def _():
            y_ref[first * REC:(first + tm) * REC, :] = jnp.zeros((tm * REC, LANES), F32)

    for k in range(MOE_TILES_PER_STEP):
        one_tile(i * MOE_TILES_PER_STEP + k, k * tm)


def _moe_tiles(l, hs, meta, router_rows, wg, wu, wd, tm, nt):
    d = D_MODEL
    per = MOE_TILES_PER_STEP
    whole = pl.BlockSpec(memory_space=pl.ANY)
    grid_spec = pltpu.PrefetchScalarGridSpec(
        num_scalar_prefetch=1,
        grid=(nt // per,),
        in_specs=[
            pl.BlockSpec((per * tm * REC, LANES), lambda i, meta: (meta[META_BLK, i * per] // per, 0)),
            pl.BlockSpec((N_EXPERTS, 1, d), lambda i, meta: (0, 0, 0), pipeline_mode=pl.Buffered(1)),
            whole, whole, whole,
        ],
        out_specs=pl.BlockSpec((per * tm * REC, LANES), lambda i, meta: (i, 0)),
        scratch_shapes=[pltpu.VMEM((N_EXPERTS, d, D_EXPERT), BF16),
                        pltpu.VMEM((N_EXPERTS, d, D_EXPERT), BF16),
                        pltpu.VMEM((N_EXPERTS, D_EXPERT, d), BF16),
                        pltpu.VMEM((2, d, D_EXPERT), F32),
                        pltpu.VMEM((2, d, D_EXPERT), F32),
                        pltpu.VMEM((2, D_EXPERT, d), F32),
                        pltpu.SemaphoreType.DMA((2,))],
    )
    return pl.pallas_call(
        functools.partial(_moe_body, tm, l),
        grid_spec=grid_spec,
        out_shape=jax.ShapeDtypeStruct((nt * tm * REC, LANES), F32),
        compiler_params=pltpu.CompilerParams(
            dimension_semantics=("arbitrary",), vmem_limit_bytes=VMEM_LIMIT),
        name="moe_tiles",
    )(meta, hs, router_rows, wg, wu, wd)


def _final_body(ts, pos_ref, x_ref, ys_hbm, mod_ref, g_ref, o_ref, ybuf, gsem):
    y_ref = _gather_moe_rows(pos_ref, ys_hbm, ybuf, gsem, ts)
    y = jnp.concatenate([_rows_of(y_ref, c, ts) for c in range(REC)], axis=1)
    x = x_ref[...] + mod_ref[0, 0, 5:6, :] * y
    r = lax.rsqrt(jnp.mean(x * x, axis=-1, keepdims=True) + EPS)
    o_ref[...] = x * r * g_ref[...]


def _final_norm(x, ys, pos, mods, final_g, ts, batch):
    t, d = x.shape
    nj = t // batch // ts
    last = mods.shape[0] - 1
    tile = lambda bi, j, pos: (bi * nj + j, 0)
    grid_spec = pltpu.PrefetchScalarGridSpec(
        num_scalar_prefetch=1,
        grid=(batch, nj),
        in_specs=[pl.BlockSpec((ts, d), tile), pl.BlockSpec(memory_space=pl.ANY),
                  pl.BlockSpec((1, 1, N_MOD, d), lambda bi, j, pos: (last, bi, 0, 0)),
                  pl.BlockSpec((1, d), lambda bi, j, pos: (0, 0))],
        out_specs=pl.BlockSpec((ts, d), tile),
        scratch_shapes=[pltpu.VMEM((2, ts * REC, LANES), F32), pltpu.SemaphoreType.DMA((2,))],
    )
    return pl.pallas_call(
        functools.partial(_final_body, ts),
        grid_spec=grid_spec,
        out_shape=jax.ShapeDtypeStruct((t, d), F32),
        compiler_params=pltpu.CompilerParams(
            dimension_semantics=("arbitrary", "arbitrary"), vmem_limit_bytes=VMEM_LIMIT),
        name="final_norm",
    )(pos, x, ys, mods, final_g.reshape(1, d))


def _stacked_weights(norm1_g, norm2_g, w_in, pool_w, pool_scale, conv_dw, conv_db, conv_ln_g,
                     conv_ln_b, conv_pw, conv_pb, sg_ln_g, sg_ln_b, sg_w, sg_b, sc_w, out_norm_g, w_out):
    depth = w_in.shape[0]
    row = lambda v: v.reshape(depth, 1, -1)
    pool_bd = jnp.zeros((depth, GROUP_W, GROUP_W), pool_w.dtype)
    for i in range(len(POOL_WINDOWS)):
        pool_bd = pool_bd.at[:, i * POOL_CG:(i + 1) * POOL_CG, i * POOL_CG:(i + 1) * POOL_CG].set(
            pool_w[:, i])
    return {
        "n1g": row(norm1_g), "n2g": row(norm2_g),
        "w_in": w_in.astype(BF16),
        "pool_w": pool_bd.astype(BF16),
        "pool_scale": row(pool_scale),
        "conv_dw": jnp.pad(conv_dw, ((0, 0), (0, 1), (0, 0))),
        "conv_db": row(conv_db), "conv_ln_g": row(conv_ln_g), "conv_ln_b": row(conv_ln_b),
        "conv_pw": conv_pw.astype(BF16), "conv_pb": row(conv_pb),
        "sg_ln_g": row(sg_ln_g), "sg_ln_b": row(sg_ln_b),
        "sg_w": sg_w,
        "sg_b": jnp.repeat(jnp.swapaxes(sg_b, 1, 2), GROUP_W // SG_HEADS, axis=2),
        "sc_w": jnp.pad(sc_w, ((0, 0), (0, SUBLANES - SC_WIDTH), (0, 0))),
        "out_norm_g": row(out_norm_g),
        "w_out": w_out.astype(BF16),
    }


def kernel(x, c, norm1_g, norm2_g, w_ada, b_ada, w_in, pool_w, pool_scale, conv_dw, conv_db, conv_ln_g, conv_ln_b, conv_pw, conv_pb, sg_ln_g, sg_ln_b, sg_w, sg_b, sc_w, out_norm_g, w_out, router_w, router_bias, exp_w_gate, exp_w_up, exp_w_down, final_g):
    b, s, d = x.shape
    depth = w_in.shape[0]
    t = b * s
    ts = 512
    ts_first = 1024
    sb = 256
    tm = 256
    nt = t // tm + N_CLASSES
    assert d == D_MODEL and t % LANES == 0 and s % ts_first == 0 and t % tm == 0 and nt <= LANES
    assert t % PERMUTE_ROWS == 0 and ts_first % ts == 0 and ts % sb == 0 and sb % SG_CHUNK == 0
    assert nt % MOE_TILES_PER_STEP == 0

    mods = _ada_mod(c, w_ada, b_ada)

    perm = lambda v: v.reshape(N_EXPERT_GROUPS, EXPERTS_PER_GROUP, -1).transpose(1, 0, 2).reshape(
        N_EXPERTS, -1)
    lw = _stacked_weights(norm1_g, norm2_g, w_in, pool_w, pool_scale, conv_dw, conv_db, conv_ln_g,
                          conv_ln_b, conv_pw, conv_pb, sg_ln_g, sg_ln_b, sg_w, sg_b, sc_w, out_norm_g,
                          w_out)
    lw["router_wt"] = perm(router_w.T).astype(BF16)
    lw["router_b"] = perm(router_bias.reshape(N_EXPERTS, 1))
    router_rows = router_w.T.astype(BF16).reshape(N_EXPERTS, 1, d)

    xcur, ys, pos = x.reshape(t, d), None, None
    for l in range(depth):
        xcur, rec, cls = _mixer_layer(l, xcur, ys, pos, mods, lw, ts if l else ts_first, sb, b)
        pos, meta = _route_plan(cls.reshape(t // LANES, LANES), tm)
        hs = _row_dispatch(rec, pos, meta, tm, nt)
        ys = _moe_tiles(l, hs, meta, router_rows, exp_w_gate, exp_w_up, exp_w_down, tm, nt)
    return _final_norm(xcur, ys, pos, mods, final_g, ts, b).reshape(b, s, d)
```

```python
import functools

import jax
import jax.numpy as jnp
from jax import lax
from jax.experimental import pallas as pl
from jax.experimental.pallas import tpu as pltpu

D_MODEL = 1024
GROUP_W = 256
POOL_WINDOWS = (2, 4, 8, 16)
POOL_CG = 64
CONV_WIDTH = 31
SG_CHUNK = 128
SG_HEADS = 4
SC_WIDTH = 3
N_EXPERTS = 16
N_EXPERT_GROUPS = 4
EXPERTS_PER_GROUP = 4
PAIRS_PER_GROUP = 6
N_CLASSES = N_EXPERT_GROUPS * PAIRS_PER_GROUP
D_EXPERT = 256
N_MOD = 6
EPS = 1e-6
LN_EPS = 1e-5

LANES = 128
SUBLANES = 8
REC = SUBLANES
CONV_HALO = 32
POOL_HALO = 16
SC_HALO = 8
CONV_ROWS = 32
META_EA, META_EB, META_VALID, META_BLK, META_END = 0, 1, 2, 3, 4
PERMUTE_ROWS = 2048
PERMUTE_GROUP = 16
MOE_TILES_PER_STEP = 4
VMEM_LIMIT = 56 * 1024 * 1024

F32 = jnp.float32
BF16 = jnp.bfloat16
I32 = jnp.int32


def _rows_of(ref, c, n, first=0):
    return ref[pl.ds(first * REC + c, n, stride=REC), :]


def _ada_body(c_ref, w_ref, b_ref, o_ref):
    ca = jax.nn.silu(c_ref[...])
    o_ref[0] = jnp.dot(ca.astype(BF16), w_ref[0].astype(BF16), preferred_element_type=F32) + b_ref[0]


def _ada_mod(c, w_ada, b_ada):
    depth, d, n = w_ada.shape
    b = c.shape[0]
    tn = 2048
    cp = jnp.pad(c, ((0, SUBLANES - b), (0, 0)))
    out = pl.pallas_call(
        _ada_body,
        grid=(depth, n // tn),
        in_specs=[
            pl.BlockSpec((SUBLANES, d), lambda l, j: (0, 0)),
            pl.BlockSpec((1, d, tn), lambda l, j: (l, 0, j)),
            pl.BlockSpec((1, 1, tn), lambda l, j: (l, 0, j)),
        ],
        out_specs=pl.BlockSpec((1, SUBLANES, tn), lambda l, j: (l, 0, j)),
        out_shape=jax.ShapeDtypeStruct((depth, SUBLANES, n), F32),
        compiler_params=pltpu.CompilerParams(
            dimension_semantics=("arbitrary", "arbitrary"), vmem_limit_bytes=VMEM_LIMIT),
        name="ada_mod",
    )(cp, w_ada, b_ada.reshape(depth, 1, n))
    return out[:, :b].reshape(depth, b, N_MOD, d)


def _layer_norm_rows(v, g, b):
    mu = jnp.mean(v, axis=-1, keepdims=True)
    vc = v - mu
    var = jnp.mean(vc * vc, axis=-1, keepdims=True)
    return vc * lax.rsqrt(var + LN_EPS) * g + b


def _carry_tail(ext_ref, j, ts, halo):
    nslab = ext_ref.shape[0]

    @pl.when(j == 0)
    def _():
        ext_ref[:, 0:halo, :] = jnp.zeros((nslab, halo, LANES), F32)

    @pl.when(j > 0)
    def _():
        ext_ref[:, 0:halo, :] = ext_ref[:, ts:ts + halo, :]


def _append_rows(ext_ref, cur, first):
    for cb in range(ext_ref.shape[0]):
        ext_ref[cb, first:first + cur.shape[0], :] = cur[:, cb * LANES:(cb + 1) * LANES]


def _first_argmax(vals):
    best = vals[0]
    idx = jnp.zeros_like(best)
    for k in range(1, len(vals)):
        better = vals[k] > best
        idx = jnp.where(better, float(k), idx)
        best = jnp.where(better, vals[k], best)
    return idx, best


def _pick(idx, vals):
    out = vals[0]
    for k in range(1, len(vals)):
        out = jnp.where(idx == float(k), vals[k], out)
    return out


def _gather_moe_rows(pos_ref, ys_hbm, ybuf, gsem, ts):
    g = pl.program_id(0) * pl.num_programs(1) + pl.program_id(1)
    n_steps = pl.num_programs(0) * pl.num_programs(1)

    def start(step, buf):
        base_row = step * (ts // LANES)
        for r0 in range(0, ts, PERMUTE_GROUP):
            slots = [pos_ref[base_row + (r0 + k) // LANES, (r0 + k) % LANES]
                     for k in range(PERMUTE_GROUP)]
            for k, slot in enumerate(slots):
                pltpu.make_async_copy(
                    ys_hbm.at[pl.ds(pl.multiple_of(slot * REC, REC), REC), :],
                    ybuf.at[buf, pl.ds((r0 + k) * REC, REC), :],
                    gsem.at[buf]).start(priority=k % 2)

    @pl.when(g == 0)
    def _():
        start(0, 0)

    @pl.when(g + 1 < n_steps)
    def _():
        start(g + 1, (g + 1) % 2)

    cur = g % 2
    pltpu.make_async_copy(ys_hbm.at[pl.ds(0, ts * REC), :], ybuf.at[cur], gsem.at[cur]).wait()
    return ybuf.at[cur]


def _mixer_body(has_prev, ts, sb, *refs):
    refs = list(refs)
    if has_prev:
        pos_ref = refs.pop(0)
    x_ref = refs.pop(0)
    if has_prev:
        ys_hbm = refs.pop(0)
        modp_ref = refs.pop(0)
        gsem = refs.pop()
        ybuf = refs.pop()
    (mod_ref, n1g_ref, n2g_ref, win_ref, poolw_ref, pools_ref,
     cdw_ref, cdb_ref, clg_ref, clb_ref, cpw_ref, cpb_ref,
     slg_ref, slb_ref, sgw_ref, sgb_ref, scw_ref, ong_ref, wout_ref, rwt_ref, rb_ref,
     x1_ref, rec_ref, cls_ref,
     ea_ref, eg_ref, ed_ref) = refs

    j = pl.program_id(1)
    gw = GROUP_W
    nslab = gw // LANES
    if has_prev:
        yprev_ref = _gather_moe_rows(pos_ref, ys_hbm, ybuf, gsem, ts)

    sh1 = mod_ref[0, 0, 0:1, :]
    sc1 = mod_ref[0, 0, 1:2, :]
    g1 = mod_ref[0, 0, 2:3, :]
    sh2 = mod_ref[0, 0, 3:4, :]
    sc2 = mod_ref[0, 0, 4:5, :]
    gain1 = n1g_ref[0] * (1.0 + sc1)
    gain2 = n2g_ref[0] * (1.0 + sc2)

    _carry_tail(ea_ref, j, ts, POOL_HALO)
    _carry_tail(eg_ref, j, ts, CONV_HALO)
    _carry_tail(ed_ref, j, ts, SC_HALO)

    rowi = lax.broadcasted_iota(I32, (SG_CHUNK, SG_CHUNK), 0)
    coli = lax.broadcasted_iota(I32, (SG_CHUNK, SG_CHUNK), 1)
    tril = coli <= rowi
    wcat = jnp.concatenate(
        [jnp.where(tril, sgw_ref[0, hh], 0.0) for hh in range(SG_HEADS)], axis=1).astype(BF16)
    lane_g = lax.broadcasted_iota(I32, (SG_CHUNK, gw), 1) // (gw // SG_HEADS)
    zero_bf = jnp.zeros((SG_CHUNK, gw), BF16)
    lane = lax.broadcasted_iota(I32, (sb, LANES), 1)
    first = lane < POOL_CG

    for r0 in range(0, ts, sb):
        x = x_ref[r0:r0 + sb, :]
        if has_prev:
            yprev = jnp.concatenate([_rows_of(yprev_ref, c, sb, r0) for c in range(REC)], axis=1)
            x = x + modp_ref[0, 0, 5:6, :] * yprev

        r1 = lax.rsqrt(jnp.mean(x * x, axis=-1, keepdims=True) + EPS)
        h = x * r1 * gain1 + sh1
        p = jnp.dot(h.astype(BF16), win_ref[0], preferred_element_type=F32)

        a = p[:, 0:gw]
        _append_rows(ea_ref, a, POOL_HALO + r0)

        def a_shift(s, cb):
            return ea_ref[cb, POOL_HALO + r0 - s:POOL_HALO + r0 - s + sb, :]

        pos1 = (lax.broadcasted_iota(I32, (sb, LANES), 0) + (j * ts + r0 + 1)).astype(F32)
        a_lo = a[:, 0:LANES]
        s01 = a_lo + a_shift(1, 0)
        s03 = s01 + (a_shift(2, 0) + a_shift(3, 0))
        num_lo = jnp.where(first, s01, s03)
        den_lo = jnp.minimum(pos1, jnp.where(first, float(POOL_WINDOWS[0]), float(POOL_WINDOWS[1])))
        a_hi = a[:, LANES:gw]
        s07 = a_hi
        for s in range(1, 8):
            s07 = s07 + a_shift(s, 1)
        s815 = a_shift(8, 1)
        for s in range(9, 16):
            s815 = s815 + a_shift(s, 1)
        num_hi = jnp.where(first, s07, s07 + s815)
        den_hi = jnp.minimum(pos1, jnp.where(first, float(POOL_WINDOWS[2]), float(POOL_WINDOWS[3])))
        d_pool = jnp.concatenate([num_lo / den_lo - a_lo, num_hi / den_hi - a_hi], axis=1)
        y_a = jnp.dot(d_pool.astype(BF16), poolw_ref[0], preferred_element_type=F32) * pools_ref[0]

        glu = p[:, gw:2 * gw] * jax.nn.sigmoid(p[:, 2 * gw:3 * gw])
        _append_rows(eg_ref, glu, CONV_HALO + r0)
        conv_cols = []
        for cb in range(nslab):
            conv_chunks = []
            for rr in range(r0, r0 + sb, CONV_ROWS):
                acc = None
                for k in range(CONV_WIDTH):
                    off = CONV_HALO - (CONV_WIDTH - 1) + k + rr
                    term = (eg_ref[cb, off:off + CONV_ROWS, :]
                            * cdw_ref[0, k:k + 1, cb * LANES:(cb + 1) * LANES])
                    acc = term if acc is None else acc + term
                conv_chunks.append(acc)
            conv_cols.append(jnp.concatenate(conv_chunks, axis=0))
        hb = jnp.concatenate(conv_cols, axis=1) + cdb_ref[0]
        hb = jax.nn.silu(_layer_norm_rows(hb, clg_ref[0], clb_ref[0]))
        y_b = jnp.dot(hb.astype(BF16), cpw_ref[0], preferred_element_type=F32) + cpb_ref[0]

        u = p[:, 3 * gw:4 * gw]
        vln = _layer_norm_rows(p[:, 4 * gw:5 * gw], slg_ref[0], slb_ref[0]).astype(BF16)
        yc_chunks = []
        for n in range(sb // SG_CHUNK):
            vch = vln[n * SG_CHUNK:(n + 1) * SG_CHUNK, :]
            vbd = jnp.concatenate([jnp.where(lane_g == hh, vch, zero_bf) for hh in range(SG_HEADS)],
                                  axis=0)
            mixed = jnp.dot(wcat, vbd, preferred_element_type=F32) + sgb_ref[0]
            yc_chunks.append(u[n * SG_CHUNK:(n + 1) * SG_CHUNK, :] * mixed)
        y_c = jnp.concatenate(yc_chunks, axis=0)

        cx = p[:, 6 * gw:7 * gw] * p[:, 7 * gw:8 * gw]
        _append_rows(ed_ref, cx, SC_HALO + r0)
        cd_cols = []
        for cb in range(nslab):
            ls = slice(cb * LANES, (cb + 1) * LANES)
            at = SC_HALO + r0
            cd_cols.append(cx[:, ls] * scw_ref[0, 2:3, ls]
                           + ed_ref[cb, at - 1:at - 1 + sb, :] * scw_ref[0, 1:2, ls]
                           + ed_ref[cb, at - 2:at - 2 + sb, :] * scw_ref[0, 0:1, ls])
        y_d = p[:, 5 * gw:6 * gw] * jnp.concatenate(cd_cols, axis=1)

        normed = []
        for gi, yg in enumerate((y_a, y_b, y_c, y_d)):
            rg = lax.rsqrt(jnp.mean(yg * yg, axis=-1, keepdims=True) + EPS)
            normed.append((yg * rg * ong_ref[0, :, gi * gw:(gi + 1) * gw]).astype(BF16))
        yn = jnp.concatenate(normed, axis=1)
        x1 = x + g1 * jnp.dot(yn, wout_ref[0], preferred_element_type=F32)
        x1_ref[r0:r0 + sb, :] = x1

        r2 = lax.rsqrt(jnp.mean(x1 * x1, axis=-1, keepdims=True) + EPS)
        h2 = x1 * r2 * gain2 + sh2
        for c in range(REC):
            rec_ref[pl.ds(r0 * REC + c, sb, stride=REC), :] = h2[:, c * LANES:(c + 1) * LANES]
        logits = lax.dot_general(rwt_ref[...], h2.astype(BF16), (((1,), (1,)), ((), ())),
                                 preferred_element_type=F32)
        sel = jax.nn.sigmoid(logits) + rb_ref[...]
        epg = EXPERTS_PER_GROUP
        sel_s = [sel[jj * N_EXPERT_GROUPS:(jj + 1) * N_EXPERT_GROUPS, :] for jj in range(epg)]
        top2 = None
        for ja in range(epg):
            for jb in range(ja + 1, epg):
                pair = sel_s[ja] + sel_s[jb]
                top2 = pair if top2 is None else jnp.maximum(top2, pair)
        gidx, _ = _first_argmax([top2[g:g + 1, :] for g in range(N_EXPERT_GROUPS)])
        sel_c = [_pick(gidx, [sel_s[jj][g:g + 1, :] for g in range(N_EXPERT_GROUPS)])
                 for jj in range(epg)]
        i1, _ = _first_argmax(sel_c)
        i2, _ = _first_argmax([jnp.where(i1 == float(jj), -jnp.inf, sel_c[jj]) for jj in range(epg)])
        lo = jnp.minimum(i1, i2)
        hi = jnp.maximum(i1, i2)
        base = jnp.where(lo == 0.0, 0.0, jnp.where(lo == 1.0, 3.0, 5.0))
        cls = gidx * float(PAIRS_PER_GROUP) + base + (hi - lo - 1.0)
        cls_ref[0, :, r0:r0 + sb] = cls.astype(I32)


def _mixer_layer(l, x, ys_prev, pos_prev, mods, lw, ts, sb, batch):
    t, d = x.shape
    nj = t // batch // ts
    has_prev = ys_prev is not None
    tile = lambda bi, j, *_: (bi * nj + j, 0)

    args = [x]
    in_specs = [pl.BlockSpec((ts, d), tile)]
    if has_prev:
        args += [ys_prev, mods]
        in_specs += [pl.BlockSpec(memory_space=pl.ANY),
                     pl.BlockSpec((1, 1, N_MOD, d), lambda bi, j, *_: (l - 1, bi, 0, 0))]
    args += [mods]
    in_specs += [pl.BlockSpec((1, 1, N_MOD, d), lambda bi, j, *_: (l, bi, 0, 0))]
    for name in ("n1g", "n2g", "w_in", "pool_w", "pool_scale", "conv_dw", "conv_db", "conv_ln_g",
                 "conv_ln_b", "conv_pw", "conv_pb", "sg_ln_g", "sg_ln_b", "sg_w", "sg_b", "sc_w",
                 "out_norm_g", "w_out"):
        arr = lw[name]
        args.append(arr)
        in_specs.append(pl.BlockSpec((1,) + arr.shape[1:],
                                     lambda bi, j, *_, nd=arr.ndim: (l,) + (0,) * (nd - 1)))
    for name in ("router_wt", "router_b"):
        arr = lw[name]
        args.append(arr)
        in_specs.append(pl.BlockSpec(arr.shape, lambda bi, j, *_: (0, 0)))

    out_shape = (jax.ShapeDtypeStruct((t, d), F32),
                 jax.ShapeDtypeStruct((t * REC, LANES), F32),
                 jax.ShapeDtypeStruct((t // ts, 1, ts), I32))
    out_specs = (pl.BlockSpec((ts, d), tile),
                 pl.BlockSpec((ts * REC, LANES), tile),
                 pl.BlockSpec((1, 1, ts), lambda bi, j, *_: (bi * nj + j, 0, 0)))
    nslab = GROUP_W // LANES
    scratch = [pltpu.VMEM((nslab, POOL_HALO + ts, LANES), F32),
               pltpu.VMEM((nslab, CONV_HALO + ts, LANES), F32),
               pltpu.VMEM((nslab, SC_HALO + ts, LANES), F32)]
    if has_prev:
        scratch += [pltpu.VMEM((2, ts * REC, LANES), F32), pltpu.SemaphoreType.DMA((2,))]
        args = [pos_prev] + args
    grid_spec = pltpu.PrefetchScalarGridSpec(
        num_scalar_prefetch=1 if has_prev else 0,
        grid=(batch, nj),
        in_specs=in_specs,
        out_specs=out_specs,
        scratch_shapes=scratch,
    )
    return pl.pallas_call(
        functools.partial(_mixer_body, has_prev, ts, sb),
        grid_spec=grid_spec,
        out_shape=out_shape,
        compiler_params=pltpu.CompilerParams(
            dimension_semantics=("arbitrary", "arbitrary"), vmem_limit_bytes=VMEM_LIMIT),
        name="mixer_layer",
    )(*args)


def _plan_body(tm, cls_ref, pos_ref, meta_ref):
    cls = cls_ref[...]
    nrow = cls.shape[0]
    ii = lax.broadcasted_iota(I32, (LANES, LANES), 0)
    jj = lax.broadcasted_iota(I32, (LANES, LANES), 1)
    upper = jnp.where(ii <= jj, 1.0, 0.0).astype(BF16)
    ri = lax.broadcasted_iota(I32, (nrow, nrow), 0)
    rj = lax.broadcasted_iota(I32, (nrow, nrow), 1)
    lower = jnp.where(rj < ri, 1.0, 0.0).astype(BF16)
    lane_i = lax.broadcasted_iota(I32, (1, LANES), 1)
    tile_i = lane_i.astype(F32)
    start = jnp.zeros((1, 1), F32)
    cum_tiles = jnp.zeros((1, 1), F32)
    pos = jnp.zeros((nrow, LANES), F32)
    tclass = jnp.zeros((1, LANES), F32)
    seg_end = jnp.zeros((1, LANES), F32)
    for k in range(N_CLASSES):
        hit = cls == k
        oh = jnp.where(hit, 1.0, 0.0)
        ohb = oh.astype(BF16)
        incl = jnp.dot(ohb, upper, preferred_element_type=F32)
        above = jnp.sum(jnp.dot(lower, ohb, preferred_element_type=F32), axis=1, keepdims=True)
        rank = incl - oh + above
        pos = pos + jnp.where(hit, start + rank, 0.0)
        cnt = jnp.sum(jnp.sum(oh, axis=1, keepdims=True), axis=0, keepdims=True)
        n_tiles = jnp.floor((cnt + float(tm - 1)) * (1.0 / tm))
        start = start + n_tiles * float(tm)
        cum_tiles = cum_tiles + n_tiles
        tclass = tclass + jnp.where(tile_i >= cum_tiles, 1.0, 0.0)
        seg_end = jnp.where(lane_i == k, start, seg_end)
    pos_ref[...] = pos.astype(I32)

    valid = tclass < float(N_CLASSES)
    kc = jnp.minimum(tclass, float(N_CLASSES - 1))
    grp = jnp.zeros_like(kc)
    for g in range(1, N_EXPERT_GROUPS):
        grp = grp + jnp.where(kc >= float(g * PAIRS_PER_GROUP), 1.0, 0.0)
    pr = kc - grp * float(PAIRS_PER_GROUP)
    lo = jnp.where(pr >= 3.0, 1.0, 0.0) + jnp.where(pr >= 5.0, 1.0, 0.0)
    base = jnp.where(lo == 0.0, 0.0, jnp.where(lo == 1.0, 3.0, 5.0))
    hi = pr - base + lo + 1.0
    rows = {META_EA: grp * float(EXPERTS_PER_GROUP) + lo,
            META_EB: grp * float(EXPERTS_PER_GROUP) + hi,
            META_VALID: jnp.where(valid, 1.0, 0.0),
            META_BLK: jnp.minimum(tile_i, cum_tiles - 1.0),
            META_END: seg_end}
    rid = lax.broadcasted_iota(I32, (SUBLANES, LANES), 0)
    meta = jnp.zeros((SUBLANES, LANES), F32)
    for k, row in rows.items():
        meta = jnp.where(rid == k, row, meta)
    meta_ref[...] = meta.astype(I32)


def _route_plan(cls2d, tm):
    nrow = cls2d.shape[0]
    return pl.pallas_call(
        functools.partial(_plan_body, tm),
        in_specs=[pl.BlockSpec((nrow, LANES), lambda: (0, 0))],
        out_specs=(pl.BlockSpec((nrow, LANES), lambda: (0, 0)),
                   pl.BlockSpec((SUBLANES, LANES), lambda: (0, 0))),
        out_shape=(jax.ShapeDtypeStruct((nrow, LANES), I32),
                   jax.ShapeDtypeStruct((SUBLANES, LANES), I32)),
        name="route_plan",
    )(cls2d)


def _dispatch_body(tm, nt, pos_ref, meta_ref, src_ref, dst_hbm, zbuf, zsem, sem):
    g = pl.program_id(0)
    n = PERMUTE_ROWS

    @pl.when(g == 0)
    def _():
        zbuf[...] = jnp.zeros_like(zbuf)

        def zero_tile(first_slot, wait):
            cp = pltpu.make_async_copy(
                zbuf, dst_hbm.at[pl.ds(pl.multiple_of(first_slot * REC, REC), tm * REC), :], zsem)
            cp.wait() if wait else cp.start()

        for wait in (False, True):
            for k in range(N_CLASSES):
                prev_end = meta_ref[META_END, k - 1] if k > 0 else 0

                @pl.when(meta_ref[META_END, k] > prev_end)
                def _():
                    zero_tile(meta_ref[META_END, k] - tm, wait)

            for i in range(nt):
                @pl.when(meta_ref[META_VALID, i] == 0)
                def _():
                    zero_tile(i * tm, wait)

    base_row = g * (n // LANES)
    for r0 in range(0, n, PERMUTE_GROUP):
        slots = [pos_ref[base_row + (r0 + k) // LANES, (r0 + k) % LANES] for k in range(PERMUTE_GROUP)]
        for k, slot in enumerate(slots):
            pltpu.make_async_copy(
                src_ref.at[pl.ds((r0 + k) * REC, REC), :],
                dst_hbm.at[pl.ds(pl.multiple_of(slot * REC, REC), REC), :],
                sem).start(priority=k % 2)

    pltpu.make_async_copy(src_ref, dst_hbm.at[pl.ds(0, n * REC), :], sem).wait()


def _row_dispatch(rec, pos, meta, tm, nt):
    n_tok = pos.size
    grid_spec = pltpu.PrefetchScalarGridSpec(
        num_scalar_prefetch=2,
        grid=(n_tok // PERMUTE_ROWS,),
        in_specs=[pl.BlockSpec((PERMUTE_ROWS * REC, LANES), lambda g, pos, meta: (g, 0))],
        out_specs=pl.BlockSpec(memory_space=pl.ANY),
        scratch_shapes=[pltpu.VMEM((tm * REC, LANES), rec.dtype), pltpu.SemaphoreType.DMA(()),
                        pltpu.SemaphoreType.DMA(())],
    )
    return pl.pallas_call(
        functools.partial(_dispatch_body, tm, nt),
        grid_spec=grid_spec,
        out_shape=jax.ShapeDtypeStruct((nt * tm * REC, LANES), rec.dtype),
        compiler_params=pltpu.CompilerParams(
            dimension_semantics=("arbitrary",), vmem_limit_bytes=VMEM_LIMIT),
        name="row_dispatch",
    )(pos, meta, rec)


def _moe_body(tm, l, meta_ref, h_ref, rw_ref, wg_hbm, wu_hbm, wd_hbm, y_ref,
              wg_ref, wu_ref, wd_ref, sg_ref, su_ref, sd_ref, wsem):
    i = pl.program_id(0)

    @pl.when(i == 0)
    def _():
        def copies(e, buf):
            return (pltpu.make_async_copy(wg_hbm.at[l, e], sg_ref.at[buf], wsem.at[buf]),
                    pltpu.make_async_copy(wu_hbm.at[l, e], su_ref.at[buf], wsem.at[buf]),
                    pltpu.make_async_copy(wd_hbm.at[l, e], sd_ref.at[buf], wsem.at[buf]))

        for cp in copies(0, 0):
            cp.start()

        def stage(e, carry):
            buf = e % 2

            @pl.when(e + 1 < N_EXPERTS)
            def _():
                for cp in copies(e + 1, 1 - buf):
                    cp.start()

            for cp in copies(e, buf):
                cp.wait()
            wg_ref[e] = sg_ref[buf].astype(BF16)
            wu_ref[e] = su_ref[buf].astype(BF16)
            wd_ref[e] = sd_ref[buf].astype(BF16)
            return carry

        lax.fori_loop(0, N_EXPERTS, stage, 0)

    def one_tile(tile, first):
        @pl.when(meta_ref[META_VALID, tile] == 1)
        def _():
            ea = meta_ref[META_EA, tile]
            eb = meta_ref[META_EB, tile]
            xb = jnp.concatenate([_rows_of(h_ref, c, tm, first) for c in range(REC)], axis=1).astype(BF16)
            xf = xb.astype(F32)

            def expert(e):
                score = jax.nn.sigmoid(jnp.sum(xf * rw_ref[e].astype(F32), axis=-1, keepdims=True))
                hg = jnp.dot(xb, wg_ref[e], preferred_element_type=F32)
                hu = jnp.dot(xb, wu_ref[e], preferred_element_type=F32)
                return score, jax.nn.silu(hg) * hu

            sa, ga = expert(ea)
            sb, gb = expert(eb)
            acta = ga * (sa / (sa + sb))
            actb = gb * (sb / (sa + sb))
            y = (jnp.dot(acta.astype(BF16), wd_ref[ea], preferred_element_type=F32)
                 + jnp.dot(actb.astype(BF16), wd_ref[eb], preferred_element_type=F32))
            for c in range(REC):
                y_ref[pl.ds(first * REC + c, tm, stride=REC), :] = y[:, c * LANES:(c + 1) * LANES]

        @pl.when(meta_ref[META_VALID, tile] == 0)
        def _():
            y_ref[first * REC:(first + tm) * REC, :] = jnp.zeros((tm * REC, LANES), F32)

    for k in range(MOE_TILES_PER_STEP):
        one_tile(i * MOE_TILES_PER_STEP + k, k * tm)


def _moe_tiles(l, hs, meta, router_rows, wg, wu, wd, tm, nt):
    d = D_MODEL
    per = MOE_TILES_PER_STEP
    whole = pl.BlockSpec(memory_space=pl.ANY)
    grid_spec = pltpu.PrefetchScalarGridSpec(
        num_scalar_prefetch=1,
        grid=(nt // per,),
        in_specs=[
            pl.BlockSpec((per * tm * REC, LANES), lambda i, meta: (meta[META_BLK, i * per] // per, 0)),
            pl.BlockSpec((N_EXPERTS, 1, d), lambda i, meta: (0, 0, 0), pipeline_mode=pl.Buffered(1)),
            whole, whole, whole,
        ],
        out_specs=pl.BlockSpec((per * tm * REC, LANES), lambda i, meta: (i, 0)),
        scratch_shapes=[pltpu.VMEM((N_EXPERTS, d, D_EXPERT), BF16),
                        pltpu.VMEM((N_EXPERTS, d, D_EXPERT), BF16),
                        pltpu.VMEM((N_EXPERTS, D_EXPERT, d), BF16),
                        pltpu.VMEM((2, d, D_EXPERT), F32),
                        pltpu.VMEM((2, d, D_EXPERT), F32),
                        pltpu.VMEM((2, D_EXPERT, d), F32),
                        pltpu.SemaphoreType.DMA((2,))],
    )
    return pl.pallas_call(
        functools.partial(_moe_body, tm, l),
        grid_spec=grid_spec,
        out_shape=jax.ShapeDtypeStruct((nt * tm * REC, LANES), F32),
        compiler_params=pltpu.CompilerParams(
            dimension_semantics=("arbitrary",), vmem_limit_bytes=VMEM_LIMIT),
        name="moe_tiles",
    )(meta, hs, router_rows, wg, wu, wd)


def _final_body(ts, pos_ref, x_ref, ys_hbm, mod_ref, g_ref, o_ref, ybuf, gsem):
    y_ref = _gather_moe_rows(pos_ref, ys_hbm, ybuf, gsem, ts)
    y = jnp.concatenate([_rows_of(y_ref, c, ts) for c in range(REC)], axis=1)
    x = x_ref[...] + mod_ref[0, 0, 5:6, :] * y
    r = lax.rsqrt(jnp.mean(x * x, axis=-1, keepdims=True) + EPS)
    o_ref[...] = x * r * g_ref[...]


def _final_norm(x, ys, pos, mods, final_g, ts, batch):
    t, d = x.shape
    nj = t // batch // ts
    last = mods.shape[0] - 1
    tile = lambda bi, j, pos: (bi * nj + j, 0)
    grid_spec = pltpu.PrefetchScalarGridSpec(
        num_scalar_prefetch=1,
        grid=(batch, nj),
        in_specs=[pl.BlockSpec((ts, d), tile), pl.BlockSpec(memory_space=pl.ANY),
                  pl.BlockSpec((1, 1, N_MOD, d), lambda bi, j, pos: (last, bi, 0, 0)),
                  pl.BlockSpec((1, d), lambda bi, j, pos: (0, 0))],
        out_specs=pl.BlockSpec((ts, d), tile),
        scratch_shapes=[pltpu.VMEM((2, ts * REC, LANES), F32), pltpu.SemaphoreType.DMA((2,))],
    )
    return pl.pallas_call(
        functools.partial(_final_body, ts),
        grid_spec=grid_spec,
        out_shape=jax.ShapeDtypeStruct((t, d), F32),
        compiler_params=pltpu.CompilerParams(
            dimension_semantics=("arbitrary", "arbitrary"), vmem_limit_bytes=VMEM_LIMIT),
        name="final_norm",
    )(pos, x, ys, mods, final_g.reshape(1, d))


def _stacked_weights(norm1_g, norm2_g, w_in, pool_w, pool_scale, conv_dw, conv_db, conv_ln_g,
                     conv_ln_b, conv_pw, conv_pb, sg_ln_g, sg_ln_b, sg_w, sg_b, sc_w, out_norm_g, w_out):
    depth = w_in.shape[0]
    row = lambda v: v.reshape(depth, 1, -1)
    pool_bd = jnp.zeros((depth, GROUP_W, GROUP_W), pool_w.dtype)
    for i in range(len(POOL_WINDOWS)):
        pool_bd = pool_bd.at[:, i * POOL_CG:(i + 1) * POOL_CG, i * POOL_CG:(i + 1) * POOL_CG].set(
            pool_w[:, i])
    return {
        "n1g": row(norm1_g), "n2g": row(norm2_g),
        "w_in": w_in.astype(BF16),
        "pool_w": pool_bd.astype(BF16),
        "pool_scale": row(pool_scale),
        "conv_dw": jnp.pad(conv_dw, ((0, 0), (0, 1), (0, 0))),
        "conv_db": row(conv_db), "conv_ln_g": row(conv_ln_g), "conv_ln_b": row(conv_ln_b),
        "conv_pw": conv_pw.astype(BF16), "conv_pb": row(conv_pb),
        "sg_ln_g": row(sg_ln_g), "sg_ln_b": row(sg_ln_b),
        "sg_w": sg_w,
        "sg_b": jnp.repeat(jnp.swapaxes(sg_b, 1, 2), GROUP_W // SG_HEADS, axis=2),
        "sc_w": jnp.pad(sc_w, ((0, 0), (0, SUBLANES - SC_WIDTH), (0, 0))),
        "out_norm_g": row(out_norm_g),
        "w_out": w_out.astype(BF16),
    }


def kernel(x, c, norm1_g, norm2_g, w_ada, b_ada, w_in, pool_w, pool_scale, conv_dw, conv_db, conv_ln_g, conv_ln_b, conv_pw, conv_pb, sg_ln_g, sg_ln_b, sg_w, sg_b, sc_w, out_norm_g, w_out, router_w, router_bias, exp_w_gate, exp_w_up, exp_w_down, final_g):
    b, s, d = x.shape
    depth = w_in.shape[0]
    t = b * s
    ts = 512
    ts_first = 1024
    sb = 256
    tm = 256
    nt = t // tm + N_CLASSES
    assert d == D_MODEL and t % LANES == 0 and s % ts_first == 0 and t % tm == 0 and nt <= LANES
    assert t % PERMUTE_ROWS == 0 and ts_first % ts == 0 and ts % sb == 0 and sb % SG_CHUNK == 0
    assert nt % MOE_TILES_PER_STEP == 0

    mods = _ada_mod(c, w_ada, b_ada)

    perm = lambda v: v.reshape(N_EXPERT_GROUPS, EXPERTS_PER_GROUP, -1).transpose(1, 0, 2).reshape(
        N_EXPERTS, -1)
    lw = _stacked_weights(norm1_g, norm2_g, w_in, pool_w, pool_scale, conv_dw, conv_db, conv_ln_g,
                          conv_ln_b, conv_pw, conv_pb, sg_ln_g, sg_ln_b, sg_w, sg_b, sc_w, out_norm_g,
                          w_out)
    lw["router_wt"] = perm(router_w.T).astype(BF16)
    lw["router_b"] = perm(router_bias.reshape(N_EXPERTS, 1))
    router_rows = router_w.T.astype(BF16).reshape(N_EXPERTS, 1, d)

    xcur, ys, pos = x.reshape(t, d), None, None
    for l in range(depth):
        xcur, rec, cls = _mixer_layer(l, xcur, ys, pos, mods, lw, ts if l else ts_first, sb, b)
        pos, meta = _route_plan(cls.reshape(t // LANES, LANES), tm)
        hs = _row_dispatch(rec, pos, meta, tm, nt)
        ys = _moe_tiles(l, hs, meta, router_rows, exp_w_gate, exp_w_up, exp_w_down, tm, nt)
    return _final_norm(xcur, ys, pos, mods, final_g, ts, b).reshape(b, s, d)
```
